```python
import math
import jax, jax.numpy as jnp
from jax import lax
import numpy as np

D_MODEL = 1024
BATCH = 8
SEQ = 2048
DEPTH = 2

GRID_W = 64
CTX_LEN = 256
HEAD_DIM = 64
ROPE_FREQS_PER_AXIS = HEAD_DIM // 4
ROPE_BASE = 10000.0
EPS = 1e-6
FNET_GROUPS = 8
FNET_GROUP_DIM = 64
FNET_WIDTH = FNET_GROUPS * FNET_GROUP_DIM
RET_HEADS = 4
RET_QK_DIM = HEAD_DIM
RET_V_DIM = 2 * HEAD_DIM
RET_CHUNK = 128
DIFF_HEADS = 4
DIFF_QK_DIM = HEAD_DIM
DIFF_V_DIM = 2 * HEAD_DIM
Q_BLOCK = 128
N_BRANCHES = 3
N_EXPERTS = 16
EC_FACTOR = 2
EXPERT_HIDDEN = 2 * D_MODEL
N_MOD = 6
IN_WIDTHS = (FNET_WIDTH,
             RET_HEADS * RET_QK_DIM, RET_HEADS * RET_QK_DIM, RET_HEADS * RET_V_DIM, RET_HEADS * RET_V_DIM,
             DIFF_HEADS * 2 * DIFF_QK_DIM, DIFF_HEADS * 2 * DIFF_QK_DIM, DIFF_HEADS * DIFF_V_DIM,
             N_BRANCHES * D_MODEL)
IN_WIDTH = sum(IN_WIDTHS)
IN_SPLITS = tuple(int(s) for s in np.cumsum(IN_WIDTHS)[:-1])

kernel_name = 'hybrid_fnet_retention_diffattn_ecmoe_dit'


def rms_norm(x, gain=None):
    xf = x.astype(jnp.float32)
    y = xf * lax.rsqrt(jnp.mean(xf * xf, axis=-1, keepdims=True) + EPS)
    if gain is not None:
        y = y * gain.astype(jnp.float32)
    return y.astype(x.dtype)


def head_layer_norm(o):
    mu = jnp.mean(o, axis=-1, keepdims=True)
    var = jnp.mean(jnp.square(o - mu), axis=-1, keepdims=True)
    return (o - mu) * lax.rsqrt(var + EPS)


def modulation(cvec, w_mod, b_mod):
    m = jax.nn.silu(cvec) @ w_mod + b_mod
    return jnp.split(m[..., None, :], N_MOD, axis=-1)


def adaln(h, shift, scale):
    return h * (1.0 + scale) + shift


def axial_rope(rows):
    row = jnp.repeat(jnp.arange(rows, dtype=jnp.float32), GRID_W)
    col = jnp.tile(jnp.arange(GRID_W, dtype=jnp.float32), rows)
    inv = ROPE_BASE ** (-jnp.arange(ROPE_FREQS_PER_AXIS, dtype=jnp.float32) / ROPE_FREQS_PER_AXIS)
    ang = jnp.concatenate([row[:, None] * inv, col[:, None] * inv], axis=-1)
    return jnp.cos(ang), jnp.sin(ang)


def apply_rope(x, cos, sin):
    shape = (cos.shape[0],) + (1,) * (x.ndim - 3) + (cos.shape[-1],)
    c = cos.reshape(shape).astype(x.dtype)
    s = sin.reshape(shape).astype(x.dtype)
    x1, x2 = jnp.split(x, 2, axis=-1)
    return jnp.concatenate([x1 * c - x2 * s, x2 * c + x1 * s], axis=-1)


def mixer_inputs(h, w_in, q_gain, k_gain, rope):
    B, N, _ = h.shape
    fo, rq, rk, rv, rg, dq, dk, dv, gates = jnp.split(h @ w_in, IN_SPLITS, axis=-1)
    rq = rq.reshape(B, N, RET_HEADS, RET_QK_DIM) * (RET_QK_DIM ** -0.5)
    rk = rk.reshape(B, N, RET_HEADS, RET_QK_DIM)
    rv = rv.reshape(B, N, RET_HEADS, RET_V_DIM)
    dq = rms_norm(dq.reshape(B, N, DIFF_HEADS, 2, DIFF_QK_DIM), q_gain)
    dk = rms_norm(dk.reshape(B, N, DIFF_HEADS, 2, DIFF_QK_DIM), k_gain)
    dv = dv.reshape(B, N, DIFF_HEADS, DIFF_V_DIM)
    if rope is not None:
        cos, sin = rope
        rq, rk = apply_rope(rq, cos, sin), apply_rope(rk, cos, sin)
        dq, dk = apply_rope(dq, cos, sin), apply_rope(dk, cos, sin)
    return fo, (rq, rk, rv, rg), (dq, dk, dv), gates


def fourier_mix(u):
    B, N, _ = u.shape
    g = u.reshape(B, N, FNET_GROUPS, FNET_GROUP_DIM).astype(jnp.float32)
    f = jnp.fft.fft2(g, axes=(1, 3), norm='ortho').real
    return f.reshape(B, N, FNET_WIDTH).astype(u.dtype)


def retention_scan(q, k, v, log_gamma, state0, include_diag):
    B, N, H, _ = q.shape
    dv = v.shape[-1]
    n_chunks = N // RET_CHUNK

    def to_chunks(t):
        return t.astype(jnp.float32).reshape(B, n_chunks, RET_CHUNK, H, t.shape[-1]).transpose(1, 0, 3, 2, 4)

    qc, kc, vc = to_chunks(q), to_chunks(k), to_chunks(v)
    lg = log_gamma.astype(jnp.float32)
    pos = jnp.arange(RET_CHUNK, dtype=jnp.float32)
    rel = pos[:, None] - pos[None, :]
    keep = (rel >= 0) if include_diag else (rel > 0)
    intra_decay = jnp.where(keep, jnp.exp(lg[:, None, None] * jnp.maximum(rel, 0.0)), 0.0)
    q_decay = jnp.exp(lg[:, None] * (pos + 1.0))[..., None]
    k_decay = jnp.exp(lg[:, None] * (RET_CHUNK - 1.0 - pos))[..., None]
    chunk_decay = jnp.exp(lg * RET_CHUNK)[:, None, None]

    def step(state, blk):
        qb, kb, vb = blk
        scores = jnp.einsum('bhid,bhjd->bhij', qb, kb) * intra_decay
        out = (jnp.einsum('bhij,bhje->bhie', scores, vb)
               + jnp.einsum('bhid,bhde->bhie', qb * q_decay, state))
        state = state * chunk_decay + jnp.einsum('bhjd,bhje->bhde', kb * k_decay, vb)
        return state, out

    state, out = lax.scan(step, state0.astype(jnp.float32), (qc, kc, vc))
    return out.transpose(1, 0, 3, 2, 4).reshape(B, N, H, dv), state


def bidirectional_retention(q, k, v, decays, state_f, state_b):
    out_f, s_f = retention_scan(q, k, v, decays[0], state_f, True)
    out_b, s_b = retention_scan(q[:, ::-1], k[:, ::-1], v[:, ::-1], decays[1], state_b, False)
    return out_f + out_b[:, ::-1], s_f, s_b


def retention_final_states(k, v, decays):
    N = k.shape[1]
    pos = jnp.arange(N, dtype=jnp.float32)
    lg = decays.astype(jnp.float32)
    w_f = jnp.exp(lg[0][None, :] * (N - 1.0 - pos)[:, None])
    w_b = jnp.exp(lg[1][None, :] * pos[:, None])
    kf, vf = k.astype(jnp.float32), v.astype(jnp.float32)
    s_f = jnp.einsum('nh,bnhd,bnhe->bhde', w_f, kf, vf)
    s_b = jnp.einsum('nh,bnhd,bnhe->bhde', w_b, kf, vf)
    return s_f, s_b


def retention_output(o, rg):
    B, N = o.shape[:2]
    return head_layer_norm(o).reshape(B, N, -1).astype(rg.dtype) * jax.nn.silu(rg)


def diff_attention_block(q, k, v, lam):
    s = jnp.einsum('bqhcd,bkhcd->bhcqk', q, k).astype(jnp.float32) * (DIFF_QK_DIM ** -0.5)
    p = jax.nn.softmax(s, axis=-1)
    a = p[:, :, 0] - lam * p[:, :, 1]
    return jnp.einsum('bhqk,bkhe->bqhe', a.astype(v.dtype), v)


def blockwise_diff_attention(q, k, v, lam):
    B, N = q.shape[:2]
    nq = N // Q_BLOCK
    qb = jnp.moveaxis(q.reshape((B, nq, Q_BLOCK) + q.shape[2:]), 1, 0)
    out = lax.map(lambda blk: diff_attention_block(blk, k, v, lam), qb)
    return jnp.moveaxis(out, 0, 1).reshape(B, N, DIFF_HEADS, DIFF_V_DIM)


def diff_output(o, lam_init):
    B, N = o.shape[:2]
    return (rms_norm(o) * (1.0 - lam_init)).reshape(B, N, -1)


def merge_branches(fo, ro, do, gate_logits, w_fo, w_ro, w_do, w_o):
    g_f, g_r, g_d = jnp.split(jax.nn.sigmoid(gate_logits), N_BRANCHES, axis=-1)
    return (g_f * (fo @ w_fo) + g_r * (ro @ w_ro) + g_d * (do @ w_do)) @ w_o


def expert_choice_moe(h, w_router, w_gate, w_up, w_down):
    B, N, D = h.shape
    cap = EC_FACTOR * N // N_EXPERTS
    aff = jax.nn.softmax((h @ w_router).astype(jnp.float32), axis=-1)
    top_w, top_idx = lax.top_k(jnp.swapaxes(aff, 1, 2), cap)
    xs = jax.vmap(lambda hb, ib: hb[ib])(h, top_idx)
    hid = jax.nn.silu(jnp.einsum('becd,edf->becf', xs, w_gate)) * jnp.einsum('becd,edf->becf', xs, w_up)
    ye = jnp.einsum('becf,efd->becd', hid, w_down) * top_w[..., None].astype(h.dtype)
    return jax.vmap(lambda ib, yb: jnp.zeros((N, D), yb.dtype).at[ib.reshape(-1)].add(yb.reshape(-1, D)))(top_idx, ye)


def setup_inputs(seed: int = 0) -> dict:
    key = jax.random.key(seed)
    ks = jax.random.split(key, 24)

    def normal(k, shape, scale):
        return jax.random.normal(k, shape, jnp.float32) * scale

    base_decay = jnp.asarray(np.log(1.0 - 2.0 ** (-5.0 - np.arange(RET_HEADS))), jnp.float32)
    mix_width = FNET_WIDTH
    return {
        'x': normal(ks[0], (BATCH, SEQ, D_MODEL), 1.0),
        'c': normal(ks[1], (BATCH, D_MODEL), 1.0),
        'ctx': normal(ks[2], (BATCH, CTX_LEN, D_MODEL), 1.0),
        'c_ctx': normal(ks[3], (D_MODEL,), 1.0),
        'w_mod': normal(ks[4], (DEPTH, D_MODEL, N_MOD * D_MODEL), 0.5 * D_MODEL ** -0.5),
        'b_mod': normal(ks[5], (DEPTH, N_MOD * D_MODEL), 0.02),
        'g_attn': 1.0 + normal(ks[6], (DEPTH, D_MODEL), 0.02),
        'g_ffn': 1.0 + normal(ks[7], (DEPTH, D_MODEL), 0.02),
        'w_in': normal(ks[8], (DEPTH, D_MODEL, IN_WIDTH), D_MODEL ** -0.5),
        'ret_decay': base_decay * (1.0 + normal(ks[9], (DEPTH, 2, RET_HEADS), 0.05)),
        'diff_qn': 1.0 + normal(ks[10], (DEPTH, DIFF_QK_DIM), 0.02),
        'diff_kn': 1.0 + normal(ks[11], (DEPTH, DIFF_QK_DIM), 0.02),
        'diff_lambda': normal(ks[12], (DEPTH, 4, DIFF_QK_DIM), 0.1),
        'w_fnet_o': normal(ks[13], (DEPTH, mix_width, D_MODEL), mix_width ** -0.5),
        'w_ret_o': normal(ks[14], (DEPTH, RET_HEADS * RET_V_DIM, D_MODEL), (RET_HEADS * RET_V_DIM) ** -0.5),
        'w_diff_o': normal(ks[15], (DEPTH, DIFF_HEADS * DIFF_V_DIM, D_MODEL), (DIFF_HEADS * DIFF_V_DIM) ** -0.5),
        'w_out': normal(ks[16], (DEPTH, D_MODEL, D_MODEL), D_MODEL ** -0.5),
        'w_router': normal(ks[17], (DEPTH, D_MODEL, N_EXPERTS), D_MODEL ** -0.5),
        'w_exp_gate': normal(ks[18], (DEPTH, N_EXPERTS, D_MODEL, EXPERT_HIDDEN), D_MODEL ** -0.5),
        'w_exp_up': normal(ks[19], (DEPTH, N_EXPERTS, D_MODEL, EXPERT_HIDDEN), D_MODEL ** -0.5),
        'w_exp_down': normal(ks[20], (DEPTH, N_EXPERTS, EXPERT_HIDDEN, D_MODEL), EXPERT_HIDDEN ** -0.5),
    }


def reference(x, c, ctx, c_ctx, w_mod, b_mod, g_attn, g_ffn, w_in, ret_decay, diff_qn, diff_kn, diff_lambda,
              w_fnet_o, w_ret_o, w_diff_o, w_out, w_router, w_exp_gate, w_exp_up, w_exp_down):
    B, N, _ = x.shape
    ROWS = N // GRID_W
    rope = axial_rope(ROWS)
    xc = ctx
    zero_state = jnp.zeros((B, RET_HEADS, RET_QK_DIM, RET_V_DIM), jnp.float32)
    for layer in range(DEPTH):
        last = layer == DEPTH - 1
        lam_init = 0.8 - 0.6 * math.exp(-0.3 * layer)
        lq1, lk1, lq2, lk2 = diff_lambda[layer].astype(jnp.float32)
        lam = jnp.exp(jnp.sum(lq1 * lk1)) - jnp.exp(jnp.sum(lq2 * lk2)) + lam_init
        mx = modulation(c, w_mod[layer], b_mod[layer])
        mc = modulation(c_ctx, w_mod[layer], b_mod[layer])
        decays = ret_decay[layer]
        branch_w = (w_fnet_o[layer], w_ret_o[layer], w_diff_o[layer], w_out[layer])
        moe_w = (w_router[layer], w_exp_gate[layer], w_exp_up[layer], w_exp_down[layer])

        hc = adaln(rms_norm(xc, g_attn[layer]), mc[0], mc[1])
        fc, (rqc, rkc, rvc, rgc), (dqc, dkc, dvc), gates_c = mixer_inputs(hc, w_in[layer], diff_qn[layer], diff_kn[layer], None)
        if last:
            s_f, s_b = retention_final_states(rkc, rvc, decays)
        else:
            ro_c, s_f, s_b = bidirectional_retention(rqc, rkc, rvc, decays, zero_state, zero_state)
            do_c = diff_attention_block(dqc, dkc, dvc, lam)
            mix_c = merge_branches(fourier_mix(fc), retention_output(ro_c, rgc), diff_output(do_c, lam_init), gates_c, *branch_w)
            xc_next = xc + mc[2] * mix_c
            h2c = adaln(rms_norm(xc_next, g_ffn[layer]), mc[3], mc[4])
            xc_next = xc_next + mc[5] * expert_choice_moe(h2c, *moe_w)

        h = adaln(rms_norm(x, g_attn[layer]), mx[0], mx[1])
        f, (rq, rk, rv, rg), (dq, dk, dv), gates = mixer_inputs(h, w_in[layer], diff_qn[layer], diff_kn[layer], rope)
        ro, _, _ = bidirectional_retention(rq, rk, rv, decays, s_f, s_b)
        k_all = jnp.concatenate([dk, dkc], axis=1)
        v_all = jnp.concatenate([dv, dvc], axis=1)
        do = blockwise_diff_attention(dq, k_all, v_all, lam)
        mix = merge_branches(fourier_mix(f), retention_output(ro, rg), diff_output(do, lam_init), gates, *branch_w)
        x = x + mx[2] * mix
        h2 = adaln(rms_norm(x, g_ffn[layer]), mx[3], mx[4])
        x = x + mx[5] * expert_choice_moe(h2, *moe_w)
        if not last:
            xc = xc_next
    return x
```

```python
import functools
import math

import jax
import jax.numpy as jnp
import numpy as np
from jax import lax
from jax.experimental import pallas as pl
from jax.experimental.pallas import tpu as pltpu

F32 = jnp.float32
BF = jnp.bfloat16
I32 = jnp.int32

GRID_W = 64
HEAD_DIM = 64
ROPE_FREQS_PER_AXIS = HEAD_DIM // 4
ROPE_BASE = 10000.0
EPS = 1e-6
FNET_GROUP_DIM = 64
N_HEADS = 4
HEAD_V = 128
N_EXPERTS = 16
EC_FACTOR = 2
N_MOD = 6
LANES = 128
VMEM_LIMIT = 56 * 1024 * 1024

_W_FO, _W_RQ, _W_RK, _W_RV, _W_RG, _W_DQ, _W_DK, _W_DV = 512, 256, 256, 512, 512, 512, 512, 512


def _cparams(sem):
    return pltpu.CompilerParams(dimension_semantics=sem, vmem_limit_bytes=VMEM_LIMIT)


def _sigmoid(v):
    return 1.0 / (1.0 + jnp.exp(-v))


def _nt_dot(a, b):
    return lax.dot_general(a, b, (((1,), (1,)), ((), ())), preferred_element_type=F32)


def _tn_dot(a, b):
    return lax.dot_general(a, b, (((0,), (0,)), ((), ())), preferred_element_type=F32)


def _mod_kernel(c_ref, w_ref, b_ref, o_ref):
    cv = c_ref[...]
    s = cv * _sigmoid(cv)
    o_ref[0] = jnp.dot(s.astype(BF), w_ref[0].astype(BF), preferred_element_type=F32) + b_ref[0]


def _modulation(cvecs, w_mod, b_mod):
    depth, d, wd = w_mod.shape
    rows = cvecs.shape[0]
    tn = 1536
    return pl.pallas_call(
        _mod_kernel,
        grid=(depth, wd // tn),
        in_specs=[pl.BlockSpec((rows, d), lambda l, j: (0, 0)),
                  pl.BlockSpec((1, d, tn), lambda l, j: (l, 0, j)),
                  pl.BlockSpec((1, 1, tn), lambda l, j: (l, 0, j))],
        out_specs=pl.BlockSpec((1, rows, tn), lambda l, j: (l, 0, j)),
        out_shape=jax.ShapeDtypeStruct((depth, rows, wd), F32),
        compiler_params=_cparams(("arbitrary", "arbitrary")),
        name="modulation",
    )(cvecs, w_mod, b_mod.reshape(depth, 1, wd))


def _rope(x, cos_t, sin_t):
    lane = lax.broadcasted_iota(I32, (x.shape[0], LANES), 1)
    first = (lane & 63) < 32
    outs = []
    for j in range(x.shape[1] // LANES):
        xc = x[:, j * LANES:(j + 1) * LANES]
        sw = jnp.where(first, pltpu.roll(xc, LANES - 32, 1), pltpu.roll(xc, 32, 1))
        outs.append(xc * cos_t + sw * sin_t)
    return jnp.concatenate(outs, axis=1)


def _group_rms(x, gm, gain):
    sq = x * x
    hi = sq.astype(BF)
    lo = (sq - hi.astype(F32)).astype(BF)
    ms = jnp.dot(hi, gm, preferred_element_type=F32) + jnp.dot(lo, gm, preferred_element_type=F32)
    return x * lax.rsqrt(ms + EPS) * gain


def _inproj_kernel(x_ref, shift_ref, scale_ref, g_ref, w_ref, cos_ref, sin_ref, qg_ref, kg_ref, gm_ref,
                   fo_ref, rq_ref, rk_ref, rv_ref, sg_ref, dq_ref, dk_ref, dv_ref, gt_ref):
    x = x_ref[...]
    d = x.shape[1]
    ms = jnp.mean(x * x, axis=-1, keepdims=True)
    h = x * lax.rsqrt(ms + EPS) * g_ref[...]
    h = h * (1.0 + scale_ref[0]) + shift_ref[0]
    hb = h.astype(BF)
    cos_t = cos_ref[...]
    sin_t = sin_ref[...]
    gm = gm_ref[...]

    def proj(a, width):
        return jnp.dot(hb, w_ref[:, a:a + width], preferred_element_type=F32)

    a = 0
    fo_ref[...] = proj(a, _W_FO).astype(BF)
    a += _W_FO
    rq_ref[...] = _rope(proj(a, _W_RQ) * (HEAD_DIM ** -0.5), cos_t, sin_t).astype(BF)
    a += _W_RQ
    rk_ref[...] = _rope(proj(a, _W_RK), cos_t, sin_t).astype(BF)
    a += _W_RK
    rv_ref[...] = proj(a, _W_RV).astype(BF)
    a += _W_RV
    rg = proj(a, _W_RG)
    sg_ref[...] = (rg * _sigmoid(rg)).astype(BF)
    a += _W_RG
    dq = _group_rms(proj(a, _W_DQ), gm, qg_ref[...])
    dq_ref[...] = (_rope(dq, cos_t, sin_t) * (HEAD_DIM ** -0.5)).astype(BF)
    a += _W_DQ
    dk = _group_rms(proj(a, _W_DK), gm, kg_ref[...])
    dk_ref[...] = _rope(dk, cos_t, sin_t).astype(BF)
    a += _W_DK
    dv_ref[...] = proj(a, _W_DV).astype(BF)
    a += _W_DV
    for j in range(3):
        gl = proj(a + j * d, d)
        gt_ref[:, j * d:(j + 1) * d] = _sigmoid(gl).astype(BF)


def _in_projection(x2d, shift, scale, g, w_in_bf, cos_t, sin_t, qg, kg, gm, n_seq, tm):
    rows, d = x2d.shape
    tiles_per_b = n_seq // tm
    nb = shift.shape[0]
    win = w_in_bf.shape[1]
    widths = (_W_FO, _W_RQ, _W_RK, _W_RV, _W_RG, _W_DQ, _W_DK, _W_DV, 3 * d)

    def mod_map(i):
        return ((i // tiles_per_b) if nb > 1 else 0, 0, 0)

    def pos_map(i):
        return (i % tiles_per_b, 0)

    const2 = lambda i: (0, 0)
    return pl.pallas_call(
        _inproj_kernel,
        grid=(rows // tm,),
        in_specs=[pl.BlockSpec((tm, d), lambda i: (i, 0)),
                  pl.BlockSpec((1, 1, d), mod_map),
                  pl.BlockSpec((1, 1, d), mod_map),
                  pl.BlockSpec((1, d), const2),
                  pl.BlockSpec((d, win), const2, pipeline_mode=pl.Buffered(1)),
                  pl.BlockSpec((tm, LANES), pos_map),
                  pl.BlockSpec((tm, LANES), pos_map),
                  pl.BlockSpec((1, _W_DQ), const2),
                  pl.BlockSpec((1, _W_DK), const2),
                  pl.BlockSpec((_W_DQ, _W_DQ), const2)],
        out_specs=[pl.BlockSpec((tm, w), lambda i: (i, 0)) for w in widths],
        out_shape=[jax.ShapeDtypeStruct((rows, w), BF) for w in widths],
        compiler_params=_cparams(("arbitrary",)),
        name="in_projection",
    )(x2d, shift, scale, g, w_in_bf, cos_t, sin_t, qg, kg, gm)


def _fourier_kernel(x_ref, cs_ref, bc_ref, bs_ref, o_ref, z_ref):
    n = x_ref.shape[1]

    @pl.when(pl.program_id(1) == 0)
    def _():
        x = x_ref[0]
        z_ref[0:n, :] = jnp.dot(x, bc_ref[...], preferred_element_type=F32).astype(BF)
        z_ref[n:2 * n, :] = jnp.dot(x, bs_ref[...], preferred_element_type=F32).astype(BF)

    o_ref[0] = jnp.dot(cs_ref[...], z_ref[...], preferred_element_type=F32).astype(BF)


def _fourier_mix(fo, cs, bdc, bds, tr):
    b, n, w = fo.shape
    return pl.pallas_call(
        _fourier_kernel,
        grid=(b, n // tr),
        in_specs=[pl.BlockSpec((1, n, w), lambda i, j: (i, 0, 0)),
                  pl.BlockSpec((tr, 2 * n), lambda i, j: (j, 0)),
                  pl.BlockSpec((w, w), lambda i, j: (0, 0)),
                  pl.BlockSpec((w, w), lambda i, j: (0, 0))],
        out_specs=pl.BlockSpec((1, tr, w), lambda i, j: (i, j, 0)),
        out_shape=jax.ShapeDtypeStruct((b, n, w), BF),
        scratch_shapes=[pltpu.VMEM((2 * n, w), BF)],
        compiler_params=_cparams(("arbitrary", "arbitrary")),
        name="fourier_mix",
    )(fo, cs, bdc, bds)


def _ret_kernel(lg_ref, q_ref, k_ref, v_ref, sg_ref, sf_ref, sb_ref, o_ref, sfo_ref, sbo_ref,
                acc_ref, st_ref, dm_ref, qd_ref, qm_ref, kd_ref, cd_ref, *, chunk, n_chunks):
    b = pl.program_id(0)
    p = pl.program_id(1)
    c = pl.program_id(2)
    cf = float(chunk)

    @pl.when((b == 0) & (p == 0) & (c == 0))
    def _tables():
        ii = lax.broadcasted_iota(I32, (chunk, chunk), 0).astype(F32)
        jj = lax.broadcasted_iota(I32, (chunk, chunk), 1).astype(F32)
        pos = lax.broadcasted_iota(I32, (chunk, LANES), 0).astype(F32)
        lane = lax.broadcasted_iota(I32, (chunk, LANES), 1)
        for dr in range(2):
            if dr == 0:
                rel, keep = ii - jj, ii >= jj
                qe, ke = pos + 1.0, (cf - 1.0) - pos
            else:
                rel, keep = jj - ii, jj > ii
                qe, ke = cf - pos, pos
            for hp in range(N_HEADS // 2):
                lg0 = jnp.full((chunk, LANES), lg_ref[dr, 2 * hp], F32)
                lg1 = jnp.full((chunk, LANES), lg_ref[dr, 2 * hp + 1], F32)
                lgp = jnp.where(lane < HEAD_DIM, lg0, lg1)
                kd_ref[dr, hp] = jnp.exp(lgp * ke)
            for h in range(N_HEADS):
                lgs = lg_ref[dr, h]
                in_head = (lane >= (h % 2) * HEAD_DIM) & (lane < (h % 2 + 1) * HEAD_DIM)
                dm_ref[dr, h] = jnp.where(keep, jnp.exp(jnp.full((chunk, chunk), lgs, F32) * jnp.maximum(rel, 0.0)), 0.0)
                qd_ref[dr, h] = jnp.where(in_head, jnp.exp(jnp.full((chunk, LANES), lgs, F32) * qe), 0.0)
                cd_ref[dr, h] = jnp.exp(jnp.full((LANES, LANES), lgs, F32) * cf)
        for h in range(N_HEADS):
            in_head = (lane >= (h % 2) * HEAD_DIM) & (lane < (h % 2 + 1) * HEAD_DIM)
            qm_ref[h] = jnp.where(in_head, 1.0, 0.0)

    def load_state(src_ref):
        z = jnp.zeros((HEAD_DIM, HEAD_V), F32)
        for h in range(N_HEADS):
            s = src_ref[0, h]
            st_ref[h] = jnp.concatenate([s, z], axis=0) if h % 2 == 0 else jnp.concatenate([z, s], axis=0)

    @pl.when((c == 0) & (p == 0))
    def _():
        load_state(sf_ref)

    @pl.when((c == 0) & (p == 1))
    def _():
        load_state(sb_ref)

    q = q_ref[0]
    k = k_ref[0]
    v = v_ref[0]
    outs = []
    for hp in range(N_HEADS // 2):
        qp = q[:, hp * LANES:(hp + 1) * LANES].astype(F32)
        kp = k[:, hp * LANES:(hp + 1) * LANES]
        kdp = (kp.astype(F32) * kd_ref[p, hp]).astype(BF)
        for h in (2 * hp, 2 * hp + 1):
            vh = v[:, h * HEAD_V:(h + 1) * HEAD_V]
            qm = (qp * qm_ref[h]).astype(BF)
            qdq = (qp * qd_ref[p, h]).astype(BF)
            s = _nt_dot(qm, kp) * dm_ref[p, h]
            st = st_ref[h]
            o = (jnp.dot(s.astype(BF), vh, preferred_element_type=F32)
                 + jnp.dot(qdq, st.astype(BF), preferred_element_type=F32))
            st_ref[h] = st * cd_ref[p, h] + _tn_dot(kdp, vh)
            outs.append(o)
    o_all = jnp.concatenate(outs, axis=1)
    idx = jnp.where(p == 0, c, n_chunks - 1 - c)

    @pl.when(p == 0)
    def _():
        acc_ref[idx] = o_all

    @pl.when(p == 1)
    def _():
        tot = acc_ref[idx] + o_all
        sg = sg_ref[0].astype(F32)
        ys = []
        for h in range(N_HEADS):
            oh = tot[:, h * HEAD_V:(h + 1) * HEAD_V]
            mu = jnp.mean(oh, axis=-1, keepdims=True)
            dlt = oh - mu
            var = jnp.mean(dlt * dlt, axis=-1, keepdims=True)
            ys.append(dlt * lax.rsqrt(var + EPS))
        o_ref[0] = (jnp.concatenate(ys, axis=1) * sg).astype(BF)

    def store_state(dst_ref):
        for h in range(N_HEADS):
            r0 = (h % 2) * HEAD_DIM
            dst_ref[0, h] = st_ref[h, r0:r0 + HEAD_DIM, :]

    @pl.when((c == n_chunks - 1) & (p == 0))
    def _():
        store_state(sfo_ref)

    @pl.when((c == n_chunks - 1) & (p == 1))
    def _():
        store_state(sbo_ref)


def _retention(rq, rk, rv, sg, decays, s_f, s_b, chunk):
    b, n, _ = rq.shape
    n_chunks = n // chunk
    wv = rv.shape[2]

    def seq_map(i, p, c):
        return (i, jnp.where(p == 0, c, n_chunks - 1 - c), 0)

    def out_map(i, p, c):
        return (i, jnp.where(p == 0, n_chunks - 1, n_chunks - 1 - c), 0)

    st_spec = pl.BlockSpec((1, N_HEADS, HEAD_DIM, HEAD_V), lambda i, p, c: (i, 0, 0, 0))
    st_shape = jax.ShapeDtypeStruct((b, N_HEADS, HEAD_DIM, HEAD_V), F32)
    kern = functools.partial(_ret_kernel, chunk=chunk, n_chunks=n_chunks)
    return pl.pallas_call(
        kern,
        grid=(b, 2, n_chunks),
        in_specs=[pl.BlockSpec(memory_space=pltpu.SMEM),
                  pl.BlockSpec((1, chunk, rq.shape[2]), seq_map),
                  pl.BlockSpec((1, chunk, rk.shape[2]), seq_map),
                  pl.BlockSpec((1, chunk, wv), seq_map),
                  pl.BlockSpec((1, chunk, wv), seq_map),
                  st_spec, st_spec],
        out_specs=[pl.BlockSpec((1, chunk, wv), out_map), st_spec, st_spec],
        out_shape=[jax.ShapeDtypeStruct((b, n, wv), BF), st_shape, st_shape],
        scratch_shapes=[pltpu.VMEM((n_chunks, chunk, wv), F32),
                        pltpu.VMEM((N_HEADS, LANES, HEAD_V), F32),
                        pltpu.VMEM((2, N_HEADS, chunk, chunk), F32),
                        pltpu.VMEM((2, N_HEADS, chunk, LANES), F32),
                        pltpu.VMEM((N_HEADS, chunk, LANES), F32),
                        pltpu.VMEM((2, N_HEADS // 2, chunk, LANES), F32),
                        pltpu.VMEM((2, N_HEADS, LANES, HEAD_V), F32)],
        compiler_params=_cparams(("arbitrary", "arbitrary", "arbitrary")),
        name="retention",
    )(decays, rq, rk, rv, sg, s_f, s_b)


def _dattn_kernel(dl_ref, q_ref, *refs, n_src, lam_init):
    k_refs = refs[:n_src]
    v_refs = refs[n_src:2 * n_src]
    o_ref = refs[2 * n_src]
    dl = dl_ref[...]
    lam = (jnp.exp(jnp.sum(dl[0:1] * dl[1:2], axis=-1, keepdims=True))
           - jnp.exp(jnp.sum(dl[2:3] * dl[3:4], axis=-1, keepdims=True)) + lam_init)
    q = q_ref[0].astype(F32)
    tq = q.shape[0]
    lane = lax.broadcasted_iota(I32, (tq, LANES), 1)
    ys = []
    for h in range(N_HEADS):
        qh = q[:, h * LANES:(h + 1) * LANES]
        qs = jnp.concatenate([jnp.where(lane < HEAD_DIM, qh, 0.0), jnp.where(lane >= HEAD_DIM, qh, 0.0)],
                             axis=0).astype(BF)
        ss = [_nt_dot(qs, kr[0, :, h * LANES:(h + 1) * LANES]) for kr in k_refs]
        m = ss[0].max(axis=-1, keepdims=True)
        for s in ss[1:]:
            m = jnp.maximum(m, s.max(axis=-1, keepdims=True))
        acc = jnp.zeros((2 * tq, HEAD_V), F32)
        lsum = jnp.zeros((2 * tq, 1), F32)
        for s, vr in zip(ss, v_refs):
            pexp = jnp.exp(s - m)
            lsum = lsum + jnp.sum(pexp, axis=-1, keepdims=True)
            acc = acc + jnp.dot(pexp.astype(BF), vr[0, :, h * HEAD_V:(h + 1) * HEAD_V], preferred_element_type=F32)
        o = acc / lsum
        oh = o[:tq] - lam * o[tq:]
        ms = jnp.mean(oh * oh, axis=-1, keepdims=True)
        ys.append(oh * lax.rsqrt(ms + EPS) * (1.0 - lam_init))
    o_ref[0] = jnp.concatenate(ys, axis=1).astype(BF)


def _diff_attention(dq, ks, vs, dlam, lam_init, tq):
    b, n, w = dq.shape
    n_src = len(ks)
    kern = functools.partial(_dattn_kernel, n_src=n_src, lam_init=lam_init)
    kv_specs = [pl.BlockSpec((1, a.shape[1], w), lambda i, j: (i, 0, 0)) for a in (*ks, *vs)]
    return pl.pallas_call(
        kern,
        grid=(b, n // tq),
        in_specs=[pl.BlockSpec(dlam.shape, lambda i, j: (0, 0)),
                  pl.BlockSpec((1, tq, w), lambda i, j: (i, j, 0))] + kv_specs,
        out_specs=pl.BlockSpec((1, tq, w), lambda i, j: (i, j, 0)),
        out_shape=jax.ShapeDtypeStruct((b, n, w), BF),
        compiler_params=_cparams(("arbitrary", "arbitrary")),
        name="diff_attention",
    )(dlam, dq, *ks, *vs)


def _merge_kernel(f_ref, ro_ref, do_ref, gt_ref, x_ref, m2_ref, m3_ref, m4_ref, g_ref,
                  wf_ref, wr_ref, wd_ref, wo_ref, rth_ref, rtl_ref, x1_ref, h2_ref, aff_ref):
    d = x_ref.shape[1]
    t = (gt_ref[:, 0:d].astype(F32) * jnp.dot(f_ref[...], wf_ref[...], preferred_element_type=F32)
         + gt_ref[:, d:2 * d].astype(F32) * jnp.dot(ro_ref[...], wr_ref[...], preferred_element_type=F32)
         + gt_ref[:, 2 * d:3 * d].astype(F32) * jnp.dot(do_ref[...], wd_ref[...], preferred_element_type=F32))
    mix = jnp.dot(t.astype(BF), wo_ref[...], preferred_element_type=F32)
    x1 = x_ref[...] + m2_ref[0] * mix
    x1_ref[...] = x1
    ms = jnp.mean(x1 * x1, axis=-1, keepdims=True)
    h2 = x1 * lax.rsqrt(ms + EPS) * g_ref[...]
    h2 = h2 * (1.0 + m4_ref[0]) + m3_ref[0]
    hi = h2.astype(BF)
    lo = (h2 - hi.astype(F32)).astype(BF)
    h2_ref[...] = hi
    lt = _nt_dot(rth_ref[...], hi) + _nt_dot(rth_ref[...], lo) + _nt_dot(rtl_ref[...], hi)
    mx = lt.max(axis=0, keepdims=True)
    ex = jnp.exp(lt - mx)
    aff_ref[0] = ex / jnp.sum(ex, axis=0, keepdims=True)


def _merge(f, ro, do, gt, x2d, m2, m3, m4, g, wf, wr, wd, wo, rth, rtl, n_seq, tm):
    rows, d = x2d.shape
    tiles_per_b = n_seq // tm
    nb = m2.shape[0]
    bsz = rows // n_seq
    ne = rth.shape[0]

    def mod_map(i):
        return ((i // tiles_per_b) if nb > 1 else 0, 0, 0)

    const2 = lambda i: (0, 0)
    row_spec = lambda w: pl.BlockSpec((tm, w), lambda i: (i, 0))
    return pl.pallas_call(
        _merge_kernel,
        grid=(rows // tm,),
        in_specs=[row_spec(f.shape[1]), row_spec(ro.shape[1]), row_spec(do.shape[1]), row_spec(gt.shape[1]),
                  row_spec(d),
                  pl.BlockSpec((1, 1, d), mod_map), pl.BlockSpec((1, 1, d), mod_map), pl.BlockSpec((1, 1, d), mod_map),
                  pl.BlockSpec((1, d), const2),
                  pl.BlockSpec(wf.shape, const2), pl.BlockSpec(wr.shape, const2), pl.BlockSpec(wd.shape, const2),
                  pl.BlockSpec(wo.shape, const2), pl.BlockSpec(rth.shape, const2), pl.BlockSpec(rtl.shape, const2)],
        out_specs=[row_spec(d), row_spec(d),
                   pl.BlockSpec((1, ne, tm), lambda i: (i // tiles_per_b, 0, i % tiles_per_b))],
        out_shape=[jax.ShapeDtypeStruct((rows, d), F32), jax.ShapeDtypeStruct((rows, d), BF),
                   jax.ShapeDtypeStruct((bsz, ne, n_seq), F32)],
        compiler_params=_cparams(("arbitrary",)),
        name="merge_router",
    )(f, ro, do, gt, x2d, m2, m3, m4, g, wf, wr, wd, wo, rth, rtl)


def _route_kernel(a_ref, slot_ref, *, cap, blk):
    a = a_ref[...]
    ne, n = a.shape
    capf = float(cap)

    def enough(t):
        return jnp.sum(jnp.where(a >= t, 1.0, 0.0), axis=-1, keepdims=True) >= capf

    tiny = jnp.full((ne, 1), 2.0 ** -126, F32)
    found = enough(tiny)
    cur = tiny
    for step in (64, 32, 16, 8, 4, 2, 1):
        cand = cur * (2.0 ** step)
        cur = jnp.where(enough(cand), cand, cur)
    base = cur

    def mantissa_bit(_, carry):
        cur, stepv = carry
        cand = cur + stepv
        return jnp.where(enough(cand), cand, cur), stepv * 0.5

    cur, ulp = lax.fori_loop(0, 23, mantissa_bit, (cur, base * 0.5))
    lo = jnp.where(found, cur, 0.0)
    hi = jnp.where(found, cur + ulp * 2.0, tiny)

    def refine(_, carry):
        lo, hi = carry
        mid = lo + (hi - lo) * 0.5
        ok = enough(mid)
        return jnp.where(ok, mid, lo), jnp.where(ok, hi, mid)

    lo, hi = lax.fori_loop(0, 24, refine, (lo, hi))
    ri = lax.broadcasted_iota(I32, (blk, blk), 0)
    ci = lax.broadcasted_iota(I32, (blk, blk), 1)
    upper = jnp.where(ri < ci, 1.0, 0.0).astype(BF)

    def excl_cumsum(m):
        carry = jnp.zeros((ne, 1), F32)
        outs = []
        for j in range(n // blk):
            mb = m[:, j * blk:(j + 1) * blk]
            outs.append(jnp.dot(mb.astype(BF), upper, preferred_element_type=F32) + carry)
            carry = carry + jnp.sum(mb, axis=-1, keepdims=True)
        return jnp.concatenate(outs, axis=1)

    gt = a >= hi
    tie = (a >= lo) & (a < hi)
    need = capf - jnp.sum(jnp.where(gt, 1.0, 0.0), axis=-1, keepdims=True)
    sel = gt | (tie & (excl_cumsum(jnp.where(tie, 1.0, 0.0)) < need))
    slot = excl_cumsum(jnp.where(sel, 1.0, 0.0))
    slot_ref[...] = jnp.where(sel, slot.astype(I32), -1)


def _route(aff_t, cap):
    b, ne, n = aff_t.shape
    kern = functools.partial(_route_kernel, cap=cap, blk=min(256, n))
    slot = pl.pallas_call(
        kern,
        grid=(1,),
        in_specs=[pl.BlockSpec((b * ne, n), lambda i: (0, 0))],
        out_specs=pl.BlockSpec((b * ne, n), lambda i: (0, 0)),
        out_shape=jax.ShapeDtypeStruct((b * ne, n), I32),
        compiler_params=_cparams(("arbitrary",)),
        name="route",
    )(aff_t.reshape(b * ne, n))
    return slot.reshape(b, ne, n)


def _gather_kernel(slot_ref, h_ref, xs_ref, *, cap):
    s = slot_ref[0, 0]
    n = s.shape[1]
    sub = lax.broadcasted_iota(I32, (cap, n), 0)
    onehot = jnp.where(sub == s, 1.0, 0.0).astype(BF)
    xs_ref[0, 0] = jnp.dot(onehot, h_ref[0], preferred_element_type=F32).astype(BF)


def _gather(slot, h2, cap):
    b, ne, n = slot.shape
    d = h2.shape[2]
    kern = functools.partial(_gather_kernel, cap=cap)
    return pl.pallas_call(
        kern,
        grid=(b, ne),
        in_specs=[pl.BlockSpec((1, 1, 1, n), lambda i, e: (i, e, 0, 0)),
                  pl.BlockSpec((1, n, d), lambda i, e: (i, 0, 0))],
        out_specs=pl.BlockSpec((1, 1, cap, d), lambda i, e: (i, e, 0, 0)),
        out_shape=jax.ShapeDtypeStruct((b, ne, cap, d), BF),
        compiler_params=_cparams(("arbitrary", "arbitrary")),
        name="moe_gather",
    )(slot.reshape(b, ne, 1, n), h2)


def _ffn_kernel(*refs, n_src, n_fc):
    xs_refs = refs[:n_src]
    wg_ref, wu_ref, wd_ref = refs[n_src:n_src + 3]
    ye_refs = refs[n_src + 3:2 * n_src + 3]
    x_ref, hid_ref = refs[2 * n_src + 3:]
    j = pl.program_id(1)
    fc = wg_ref.shape[3]
    row_spans = []
    r0 = 0
    for xr in xs_refs:
        rows = xr.shape[0] * xr.shape[2]
        row_spans.append((r0, rows))
        r0 += rows

    @pl.when(j == 0)
    def _():
        for xr, (s0, rows) in zip(xs_refs, row_spans):
            x_ref[s0:s0 + rows, :] = xr[:, 0].reshape(rows, xr.shape[3])

    @pl.when(j < n_fc)
    def _():
        x = x_ref[...]
        g = jnp.dot(x, wg_ref[0, 0].astype(BF), preferred_element_type=F32)
        u = jnp.dot(x, wu_ref[0, 0].astype(BF), preferred_element_type=F32)
        hid_ref[j] = (g * _sigmoid(g) * u).astype(BF)

    @pl.when(j >= n_fc)
    def _():
        y = jnp.dot(hid_ref[0], wd_ref[0, 0, 0:fc, :].astype(BF), preferred_element_type=F32)
        for k in range(1, n_fc):
            y = y + jnp.dot(hid_ref[k], wd_ref[0, 0, k * fc:(k + 1) * fc, :].astype(BF), preferred_element_type=F32)
        for yr, (s0, rows) in zip(ye_refs, row_spans):
            yr[:, 0] = y[s0:s0 + rows].reshape(yr.shape[0], yr.shape[2], yr.shape[3]).astype(BF)


def _expert_ffn(xs_list, w_gate, w_up, w_down, layer, fchunk, ochunk):
    ne, d = xs_list[0].shape[1], xs_list[0].shape[3]
    f = w_gate.shape[3]
    n_fc, n_oc = f // fchunk, d // ochunk
    n_src = len(xs_list)
    total_rows = sum(x.shape[0] * x.shape[2] for x in xs_list)
    kern = functools.partial(_ffn_kernel, n_src=n_src, n_fc=n_fc)
    up_map = lambda e, j: (layer, e, 0, jnp.minimum(j, n_fc - 1))
    down_map = lambda e, j: (layer, e, 0, jnp.maximum(j - n_fc, 0))
    out_map = lambda e, j: (0, e, 0, jnp.maximum(j - n_fc, 0))
    outs = pl.pallas_call(
        kern,
        grid=(ne, n_fc + n_oc),
        in_specs=[pl.BlockSpec((x.shape[0], 1, x.shape[2], d), lambda e, j: (0, e, 0, 0)) for x in xs_list]
        + [pl.BlockSpec((1, 1, d, fchunk), up_map),
           pl.BlockSpec((1, 1, d, fchunk), up_map),
           pl.BlockSpec((1, 1, f, ochunk), down_map)],
        out_specs=[pl.BlockSpec((x.shape[0], 1, x.shape[2], ochunk), out_map) for x in xs_list],
        out_shape=[jax.ShapeDtypeStruct(x.shape, BF) for x in xs_list],
        scratch_shapes=[pltpu.VMEM((total_rows, d), BF), pltpu.VMEM((n_fc, total_rows, fchunk), BF)],
        compiler_params=_cparams(("arbitrary", "arbitrary")),
        name="expert_ffn",
    )(*xs_list, w_gate, w_up, w_down)
    return list(outs)


def _combine_kernel(x1_ref, m5_ref, slot_ref, aff_ref, ye_ref, o_ref, *, cap):
    st = slot_ref[0]
    at = aff_ref[0]
    tn, ne = st.shape
    lane = lax.broadcasted_iota(I32, (tn, cap), 1)
    acc = jnp.zeros(x1_ref.shape[1:], F32)
    for e in range(ne):
        onehot = jnp.where(st[:, e:e + 1] == lane, 1.0, 0.0).astype(BF)
        acc = acc + jnp.dot(onehot, ye_ref[0, e], preferred_element_type=F32) * at[:, e:e + 1]
    o_ref[0] = x1_ref[0] + m5_ref[0] * acc


def _combine(x1, m5, slot_tok, aff_tok, ye, tn):
    b, n, d = x1.shape
    ne, cap = ye.shape[1], ye.shape[2]
    nb = m5.shape[0]
    kern = functools.partial(_combine_kernel, cap=cap)
    return pl.pallas_call(
        kern,
        grid=(b, n // tn),
        in_specs=[pl.BlockSpec((1, tn, d), lambda i, j: (i, j, 0)),
                  pl.BlockSpec((1, 1, d), lambda i, j: (i if nb > 1 else 0, 0, 0)),
                  pl.BlockSpec((1, tn, ne), lambda i, j: (i, j, 0)),
                  pl.BlockSpec((1, tn, ne), lambda i, j: (i, j, 0)),
                  pl.BlockSpec((1, ne, cap, d), lambda i, j: (i, 0, 0, 0))],
        out_specs=pl.BlockSpec((1, tn, d), lambda i, j: (i, j, 0)),
        out_shape=jax.ShapeDtypeStruct((b, n, d), F32),
        compiler_params=_cparams(("arbitrary", "arbitrary")),
        name="moe_combine",
    )(x1, m5, slot_tok, aff_tok, ye)


def _rope_tables(n, use_rope):
    if not use_rope:
        return jnp.ones((n, LANES), F32), jnp.zeros((n, LANES), F32)
    rows = n // GRID_W
    row = jnp.repeat(jnp.arange(rows, dtype=F32), GRID_W)
    col = jnp.tile(jnp.arange(GRID_W, dtype=F32), rows)
    inv = ROPE_BASE ** (-jnp.arange(ROPE_FREQS_PER_AXIS, dtype=F32) / ROPE_FREQS_PER_AXIS)
    ang = jnp.concatenate([row[:, None] * inv, col[:, None] * inv], axis=-1)
    cos, sin = jnp.cos(ang), jnp.sin(ang)
    cos_t = jnp.tile(cos, (1, LANES // (HEAD_DIM // 2)))
    sin_t = jnp.tile(jnp.concatenate([-sin, sin], axis=-1), (1, LANES // HEAD_DIM))
    return cos_t, sin_t


def _dft_tables(n):
    n0 = 64
    n1 = n // n0
    k = np.arange(n, dtype=np.int64)[:, None]
    ang1 = jnp.asarray(2.0 * np.pi * ((k * np.arange(n1)[None, :]) % n1) / n1, F32)
    ang0 = jnp.asarray(2.0 * np.pi * ((k * np.arange(n0)[None, :]) % n) / n, F32)
    c1, s1, c0, s0 = jnp.cos(ang1), jnp.sin(ang1), jnp.cos(ang0), jnp.sin(ang0)
    scale = 1.0 / math.sqrt(n)
    cos_n = (c1[:, :, None] * c0[:, None, :] - s1[:, :, None] * s0[:, None, :]).reshape(n, n) * scale
    sin_n = (s1[:, :, None] * c0[:, None, :] + c1[:, :, None] * s0[:, None, :]).reshape(n, n) * scale
    return jnp.concatenate([cos_n, -sin_n], axis=1).astype(BF)


def _group_dft_tables(width):
    g = FNET_GROUP_DIM
    idx = np.arange(width)
    same = (idx[:, None] // g) == (idx[None, :] // g)
    ang = 2.0 * np.pi * (((idx[:, None] % g) * (idx[None, :] % g)) % g) / g
    scale = 1.0 / math.sqrt(g)
    bdc = np.where(same, np.cos(ang), 0.0) * scale
    bds = np.where(same, np.sin(ang), 0.0) * scale
    return jnp.asarray(bdc, F32).astype(BF), jnp.asarray(bds, F32).astype(BF)


def _group_mean_matrix(width):
    idx = np.arange(width)
    same = (idx[:, None] // HEAD_DIM) == (idx[None, :] // HEAD_DIM)
    return jnp.asarray(np.where(same, 1.0 / HEAD_DIM, 0.0), BF)


def _mixer_inputs(x3, mods, g_attn_l, w_in_bf, tables, qg, kg, gm):
    b, n, d = x3.shape
    tm = min(512, n)
    outs = _in_projection(x3.reshape(b * n, d), mods[0], mods[1], g_attn_l, w_in_bf, tables[0], tables[1],
                          qg, kg, gm, n, tm)
    return [o.reshape(b, n, o.shape[1]) for o in outs]


def _moe(sets, w_gate, w_up, w_down, layer):
    slots, xss = [], []
    for x1, h2, aff_t, _ in sets:
        cap = EC_FACTOR * x1.shape[1] // N_EXPERTS
        slot = _route(aff_t, cap)
        slots.append(slot)
        xss.append(_gather(slot, h2, cap))
    yes = _expert_ffn(xss, w_gate, w_up, w_down, layer, 256, 256)
    return [_combine(x1, m5, jnp.swapaxes(slot, 1, 2), jnp.swapaxes(aff_t, 1, 2), ye, min(512, x1.shape[1]))
            for (x1, _, aff_t, m5), slot, ye in zip(sets, slots, yes)]


def kernel(x, c, ctx, c_ctx, w_mod, b_mod, g_attn, g_ffn, w_in, ret_decay, diff_qn, diff_kn, diff_lambda,
           w_fnet_o, w_ret_o, w_diff_o, w_out, w_router, w_exp_gate, w_exp_up, w_exp_down):
    bsz, n, d = x.shape
    n_ctx = ctx.shape[1]
    depth = w_mod.shape[0]

    pad = (-(bsz + 1)) % 8
    cvecs = jnp.concatenate([c, c_ctx[None, :], jnp.zeros((pad, d), F32)], axis=0)
    mods = _modulation(cvecs, w_mod, b_mod)

    rope_lat = _rope_tables(n, True)
    rope_ctx = _rope_tables(n_ctx, False)
    cs_lat, cs_ctx = _dft_tables(n), _dft_tables(n_ctx)
    bdc, bds = _group_dft_tables(_W_FO)
    gm = _group_mean_matrix(_W_DQ)
    zero_state = jnp.zeros((bsz, N_HEADS, HEAD_DIM, HEAD_V), F32)
    ret_chunk = 256

    xc = ctx
    for layer in range(depth):
        last = layer == depth - 1
        lam_init = 0.8 - 0.6 * math.exp(-0.3 * layer)
        mx = [mods[layer, :bsz, j * d:(j + 1) * d].reshape(bsz, 1, d) for j in range(N_MOD)]
        mc = [mods[layer, bsz:bsz + 1, j * d:(j + 1) * d].reshape(1, 1, d) for j in range(N_MOD)]
        w_in_bf = w_in[layer].astype(BF)
        qg = jnp.tile(diff_qn[layer], _W_DQ // HEAD_DIM)[None, :]
        kg = jnp.tile(diff_kn[layer], _W_DK // HEAD_DIM)[None, :]
        g_a = g_attn[layer][None, :]
        g_f = g_ffn[layer][None, :]
        decays = ret_decay[layer]
        dlam = diff_lambda[layer]
        branch_w = (w_fnet_o[layer].astype(BF), w_ret_o[layer].astype(BF), w_diff_o[layer].astype(BF),
                    w_out[layer].astype(BF))
        rt = w_router[layer].T
        rth = rt.astype(BF)
        rtl = (rt - rth.astype(F32)).astype(BF)
        moe_sets = []

        fo_c, rq_c, rk_c, rv_c, sg_c, dq_c, dk_c, dv_c, gt_c = _mixer_inputs(xc, mc, g_a, w_in_bf, rope_ctx, qg, kg, gm)
        ro_c, s_f, s_b = _retention(rq_c, rk_c, rv_c, sg_c, decays, zero_state, zero_state, min(ret_chunk, n_ctx))
        if not last:
            f_c = _fourier_mix(fo_c, cs_ctx, bdc, bds, min(512, n_ctx))
            do_c = _diff_attention(dq_c, [dk_c], [dv_c], dlam, lam_init, min(256, n_ctx))
            rows_c = bsz * n_ctx
            x1_c, h2_c, aff_c = _merge(f_c.reshape(rows_c, -1), ro_c.reshape(rows_c, -1), do_c.reshape(rows_c, -1),
                                       gt_c.reshape(rows_c, -1), xc.reshape(rows_c, d), mc[2], mc[3], mc[4], g_f,
                                       *branch_w, rth, rtl, n_ctx, min(512, n_ctx))
            moe_sets.append((x1_c.reshape(bsz, n_ctx, d), h2_c.reshape(bsz, n_ctx, d), aff_c, mc[5]))

        fo, rq, rk, rv, sg, dq, dk, dv, gt = _mixer_inputs(x, mx, g_a, w_in_bf, rope_lat, qg, kg, gm)
        ro, _, _ = _retention(rq, rk, rv, sg, decays, s_f, s_b, ret_chunk)
        f = _fourier_mix(fo, cs_lat, bdc, bds, min(512, n))
        do = _diff_attention(dq, [dk, dk_c], [dv, dv_c], dlam, lam_init, 256)
        rows = bsz * n
        x1, h2, aff = _merge(f.reshape(rows, -1), ro.reshape(rows, -1), do.reshape(rows, -1), gt.reshape(rows, -1),
                             x.reshape(rows, d), mx[2], mx[3], mx[4], g_f, *branch_w, rth, rtl, n, min(512, n))
        moe_sets.append((x1.reshape(bsz, n, d), h2.reshape(bsz, n, d), aff, mx[5]))
        moe_out = _moe(moe_sets, w_exp_gate, w_exp_up, w_exp_down, layer)
        x = moe_out[-1]
        if not last:
            xc = moe_out[0]
    return x
```

```python
import functools
import math

import jax
import jax.numpy as jnp
import numpy as np
from jax import lax
from jax.experimental import pallas as pl
from jax.experimental.pallas import tpu as pltpu

F32 = jnp.float32
BF = jnp.bfloat16
I32 = jnp.int32

GRID_W = 64
HEAD_DIM = 64
ROPE_FREQS_PER_AXIS = HEAD_DIM // 4
ROPE_BASE = 10000.0
EPS = 1e-6
LOG2E = 1.4426950408889634
FNET_GROUP_DIM = 64
N_HEADS = 4
HEAD_V = 128
N_EXPERTS = 16
EC_FACTOR = 2
N_MOD = 6
LANES = 128
VMEM_LIMIT = 56 * 1024 * 1024

_W_FO, _W_RQ, _W_RK, _W_RV, _W_RG, _W_DQ, _W_DK, _W_DV = 512, 256, 256, 512, 512, 512, 512, 512


def _cparams(sem):
    return pltpu.CompilerParams(dimension_semantics=sem, vmem_limit_bytes=VMEM_LIMIT)


def _sigmoid(v):
    return 1.0 / (1.0 + jnp.exp(-v))


def _nt_dot(a, b):
    return lax.dot_general(a, b, (((1,), (1,)), ((), ())), preferred_element_type=F32)


def _tn_dot(a, b):
    return lax.dot_general(a, b, (((0,), (0,)), ((), ())), preferred_element_type=F32)


def _mod_kernel(c_ref, w_ref, b_ref, o_ref):
    cv = c_ref[...]
    s = cv * _sigmoid(cv)
    o_ref[0] = jnp.dot(s.astype(BF), w_ref[0].astype(BF), preferred_element_type=F32) + b_ref[0]


def _modulation(cvecs, w_mod, b_mod):
    depth, d, wd = w_mod.shape
    rows = cvecs.shape[0]
    tn = 1536
    return pl.pallas_call(
        _mod_kernel,
        grid=(depth, wd // tn),
        in_specs=[pl.BlockSpec((rows, d), lambda l, j: (0, 0)),
                  pl.BlockSpec((1, d, tn), lambda l, j: (l, 0, j)),
                  pl.BlockSpec((1, 1, tn), lambda l, j: (l, 0, j))],
        out_specs=pl.BlockSpec((1, rows, tn), lambda l, j: (l, 0, j)),
        out_shape=jax.ShapeDtypeStruct((depth, rows, wd), F32),
        compiler_params=_cparams(("arbitrary", "arbitrary")),
        name="modulation",
    )(cvecs, w_mod, b_mod.reshape(depth, 1, wd))


def _rope(x, cos_t, sin_t):
    lane = lax.broadcasted_iota(I32, (x.shape[0], LANES), 1)
    first = (lane & 63) < 32
    outs = []
    for j in range(x.shape[1] // LANES):
        xc = x[:, j * LANES:(j + 1) * LANES]
        sw = jnp.where(first, pltpu.roll(xc, LANES - 32, 1), pltpu.roll(xc, 32, 1))
        outs.append(xc * cos_t + sw * sin_t)
    return jnp.concatenate(outs, axis=1)


def _group_rms(x, gm, gain):
    ms = jnp.dot((x * x).astype(BF), gm, preferred_element_type=F32)
    return x * lax.rsqrt(ms + EPS) * gain


def _inproj_kernel(x_ref, shift_ref, scale_ref, g_ref, w_ref, cos_ref, sin_ref, qg_ref, kg_ref, gm_ref,
                   fo_ref, rq_ref, rk_ref, rv_ref, sg_ref, dq_ref, dk_ref, dv_ref, gt_ref):
    x = x_ref[...]
    d = x.shape[1]
    ms = jnp.mean(x * x, axis=-1, keepdims=True)
    h = x * lax.rsqrt(ms + EPS) * g_ref[...]
    h = h * (1.0 + scale_ref[0]) + shift_ref[0]
    hb = h.astype(BF)
    cos_t = cos_ref[...]
    sin_t = sin_ref[...]
    gm = gm_ref[...]

    def proj(a, width):
        return jnp.dot(hb, w_ref[:, a:a + width], preferred_element_type=F32)

    a = 0
    fo_ref[...] = proj(a, _W_FO).astype(BF)
    a += _W_FO
    rq_ref[...] = _rope(proj(a, _W_RQ) * (HEAD_DIM ** -0.5), cos_t, sin_t).astype(BF)
    a += _W_RQ
    rk_ref[...] = _rope(proj(a, _W_RK), cos_t, sin_t).astype(BF)
    a += _W_RK
    rv_ref[...] = proj(a, _W_RV).astype(BF)
    a += _W_RV
    rg = proj(a, _W_RG)
    sg_ref[...] = (rg * _sigmoid(rg)).astype(BF)
    a += _W_RG
    dq = _group_rms(proj(a, _W_DQ), gm, qg_ref[...])
    dq_ref[...] = (_rope(dq, cos_t, sin_t) * (HEAD_DIM ** -0.5 * LOG2E)).astype(BF)
    a += _W_DQ
    dk = _group_rms(proj(a, _W_DK), gm, kg_ref[...])
    dk_ref[...] = _rope(dk, cos_t, sin_t).astype(BF)
    a += _W_DK
    dv_ref[...] = proj(a, _W_DV).astype(BF)
    a += _W_DV
    for j in range(3):
        gl = proj(a + j * d, d)
        gt_ref[:, j * d:(j + 1) * d] = _sigmoid(gl).astype(BF)


def _in_projection(x2d, shift, scale, g, w_in_bf, cos_t, sin_t, qg, kg, gm, n_seq, tm):
    rows, d = x2d.shape
    tiles_per_b = n_seq // tm
    nb = shift.shape[0]
    win = w_in_bf.shape[1]
    widths = (_W_FO, _W_RQ, _W_RK, _W_RV, _W_RG, _W_DQ, _W_DK, _W_DV, 3 * d)

    def mod_map(i):
        return ((i // tiles_per_b) if nb > 1 else 0, 0, 0)

    def pos_map(i):
        return (i % tiles_per_b, 0)

    const2 = lambda i: (0, 0)
    return pl.pallas_call(
        _inproj_kernel,
        grid=(rows // tm,),
        in_specs=[pl.BlockSpec((tm, d), lambda i: (i, 0)),
                  pl.BlockSpec((1, 1, d), mod_map),
                  pl.BlockSpec((1, 1, d), mod_map),
                  pl.BlockSpec((1, d), const2),
                  pl.BlockSpec((d, win), const2, pipeline_mode=pl.Buffered(1)),
                  pl.BlockSpec((tm, LANES), pos_map),
                  pl.BlockSpec((tm, LANES), pos_map),
                  pl.BlockSpec((1, _W_DQ), const2),
                  pl.BlockSpec((1, _W_DK), const2),
                  pl.BlockSpec((_W_DQ, _W_DQ), const2)],
        out_specs=[pl.BlockSpec((tm, w), lambda i: (i, 0)) for w in widths],
        out_shape=[jax.ShapeDtypeStruct((rows, w), BF) for w in widths],
        compiler_params=_cparams(("arbitrary",)),
        name="in_projection",
    )(x2d, shift, scale, g, w_in_bf, cos_t, sin_t, qg, kg, gm)


def _fourier_kernel(x_ref, cs_ref, bc_ref, bs_ref, o_ref, z_ref):
    n = x_ref.shape[1]

    @pl.when(pl.program_id(1) == 0)
    def _():
        x = x_ref[0]
        z_ref[0:n, :] = jnp.dot(x, bc_ref[...], preferred_element_type=F32).astype(BF)
        z_ref[n:2 * n, :] = jnp.dot(x, bs_ref[...], preferred_element_type=F32).astype(BF)

    o_ref[0] = jnp.dot(cs_ref[...], z_ref[...], preferred_element_type=F32).astype(BF)


def _fourier_mix(fo, cs, bdc, bds, tr):
    b, n, w = fo.shape
    return pl.pallas_call(
        _fourier_kernel,
        grid=(b, n // tr),
        in_specs=[pl.BlockSpec((1, n, w), lambda i, j: (i, 0, 0)),
                  pl.BlockSpec((tr, 2 * n), lambda i, j: (j, 0)),
                  pl.BlockSpec((w, w), lambda i, j: (0, 0)),
                  pl.BlockSpec((w, w), lambda i, j: (0, 0))],
        out_specs=pl.BlockSpec((1, tr, w), lambda i, j: (i, j, 0)),
        out_shape=jax.ShapeDtypeStruct((b, n, w), BF),
        scratch_shapes=[pltpu.VMEM((2 * n, w), BF)],
        compiler_params=_cparams(("arbitrary", "arbitrary")),
        name="fourier_mix",
    )(fo, cs, bdc, bds)


def _ret_kernel(lg_ref, q_ref, k_ref, v_ref, sg_ref, sf_ref, sb_ref, o_ref, sfo_ref, sbo_ref,
                acc_ref, st_ref, dm_ref, qd_ref, qm_ref, kd_ref, cd_ref, *, chunk, n_chunks):
    b = pl.program_id(0)
    p = pl.program_id(1)
    c = pl.program_id(2)
    cf = float(chunk)

    @pl.when((b == 0) & (p == 0) & (c == 0))
    def _tables():
        ii = lax.broadcasted_iota(I32, (chunk, chunk), 0).astype(F32)
        jj = lax.broadcasted_iota(I32, (chunk, chunk), 1).astype(F32)
        pos = lax.broadcasted_iota(I32, (chunk, LANES), 0).astype(F32)
        lane = lax.broadcasted_iota(I32, (chunk, LANES), 1)
        for dr in range(2):
            if dr == 0:
                rel, keep = ii - jj, ii >= jj
                qe, ke = pos + 1.0, (cf - 1.0) - pos
            else:
                rel, keep = jj - ii, jj > ii
                qe, ke = cf - pos, pos
            for hp in range(N_HEADS // 2):
                lg0 = jnp.full((chunk, LANES), lg_ref[dr, 2 * hp], F32)
                lg1 = jnp.full((chunk, LANES), lg_ref[dr, 2 * hp + 1], F32)
                lgp = jnp.where(lane < HEAD_DIM, lg0, lg1)
                kd_ref[dr, hp] = jnp.exp(lgp * ke)
            for h in range(N_HEADS):
                lgs = lg_ref[dr, h]
                in_head = (lane >= (h % 2) * HEAD_DIM) & (lane < (h % 2 + 1) * HEAD_DIM)
                dm_ref[dr, h] = jnp.where(keep, jnp.exp(jnp.full((chunk, chunk), lgs, F32) * jnp.maximum(rel, 0.0)), 0.0)
                qd_ref[dr, h] = jnp.where(in_head, jnp.exp(jnp.full((chunk, LANES), lgs, F32) * qe), 0.0)
                cd_ref[dr, h] = jnp.exp(jnp.full((LANES, LANES), lgs, F32) * cf)
        for h in range(N_HEADS):
            in_head = (lane >= (h % 2) * HEAD_DIM) & (lane < (h % 2 + 1) * HEAD_DIM)
            qm_ref[h] = jnp.where(in_head, 1.0, 0.0)

    def load_state(src_ref):
        z = jnp.zeros((HEAD_DIM, HEAD_V), F32)
        for h in range(N_HEADS):
            s = src_ref[0, h]
            st_ref[h] = jnp.concatenate([s, z], axis=0) if h % 2 == 0 else jnp.concatenate([z, s], axis=0)

    @pl.when((c == 0) & (p == 0))
    def _():
        load_state(sf_ref)

    @pl.when((c == 0) & (p == 1))
    def _():
        load_state(sb_ref)

    q = q_ref[0]
    k = k_ref[0]
    v = v_ref[0]
    outs = []
    for hp in range(N_HEADS // 2):
        qp = q[:, hp * LANES:(hp + 1) * LANES].astype(F32)
        kp = k[:, hp * LANES:(hp + 1) * LANES]
        kdp = (kp.astype(F32) * kd_ref[p, hp]).astype(BF)
        for h in (2 * hp, 2 * hp + 1):
            vh = v[:, h * HEAD_V:(h + 1) * HEAD_V]
            qm = (qp * qm_ref[h]).astype(BF)
            qdq = (qp * qd_ref[p, h]).astype(BF)
            s = _nt_dot(qm, kp) * dm_ref[p, h]
            st = st_ref[h]
            o = (jnp.dot(s.astype(BF), vh, preferred_element_type=F32)
                 + jnp.dot(qdq, st.astype(BF), preferred_element_type=F32))
            st_ref[h] = st * cd_ref[p, h] + _tn_dot(kdp, vh)
            outs.append(o)
    o_all = jnp.concatenate(outs, axis=1)
    idx = jnp.where(p == 0, c, n_chunks - 1 - c)

    @pl.when(p == 0)
    def _():
        acc_ref[idx] = o_all

    @pl.when(p == 1)
    def _():
        tot = acc_ref[idx] + o_all
        sg = sg_ref[0].astype(F32)
        ys = []
        for h in range(N_HEADS):
            oh = tot[:, h * HEAD_V:(h + 1) * HEAD_V]
            mu = jnp.mean(oh, axis=-1, keepdims=True)
            dlt = oh - mu
            var = jnp.mean(dlt * dlt, axis=-1, keepdims=True)
            ys.append(dlt * lax.rsqrt(var + EPS))
        o_ref[0] = (jnp.concatenate(ys, axis=1) * sg).astype(BF)

    def store_state(dst_ref):
        for h in range(N_HEADS):
            r0 = (h % 2) * HEAD_DIM
            dst_ref[0, h] = st_ref[h, r0:r0 + HEAD_DIM, :]

    @pl.when((c == n_chunks - 1) & (p == 0))
    def _():
        store_state(sfo_ref)

    @pl.when((c == n_chunks - 1) & (p == 1))
    def _():
        store_state(sbo_ref)


def _retention(rq, rk, rv, sg, decays, s_f, s_b, chunk):
    b, n, _ = rq.shape
    n_chunks = n // chunk
    wv = rv.shape[2]

    def seq_map(i, p, c):
        return (i, jnp.where(p == 0, c, n_chunks - 1 - c), 0)

    def out_map(i, p, c):
        return (i, jnp.where(p == 0, n_chunks - 1, n_chunks - 1 - c), 0)

    st_spec = pl.BlockSpec((1, N_HEADS, HEAD_DIM, HEAD_V), lambda i, p, c: (i, 0, 0, 0))
    st_shape = jax.ShapeDtypeStruct((b, N_HEADS, HEAD_DIM, HEAD_V), F32)
    kern = functools.partial(_ret_kernel, chunk=chunk, n_chunks=n_chunks)
    return pl.pallas_call(
        kern,
        grid=(b, 2, n_chunks),
        in_specs=[pl.BlockSpec(memory_space=pltpu.SMEM),
                  pl.BlockSpec((1, chunk, rq.shape[2]), seq_map),
                  pl.BlockSpec((1, chunk, rk.shape[2]), seq_map),
                  pl.BlockSpec((1, chunk, wv), seq_map),
                  pl.BlockSpec((1, chunk, wv), seq_map),
                  st_spec, st_spec],
        out_specs=[pl.BlockSpec((1, chunk, wv), out_map), st_spec, st_spec],
        out_shape=[jax.ShapeDtypeStruct((b, n, wv), BF), st_shape, st_shape],
        scratch_shapes=[pltpu.VMEM((n_chunks, chunk, wv), F32),
                        pltpu.VMEM((N_HEADS, LANES, HEAD_V), F32),
                        pltpu.VMEM((2, N_HEADS, chunk, chunk), F32),
                        pltpu.VMEM((2, N_HEADS, chunk, LANES), F32),
                        pltpu.VMEM((N_HEADS, chunk, LANES), F32),
                        pltpu.VMEM((2, N_HEADS // 2, chunk, LANES), F32),
                        pltpu.VMEM((2, N_HEADS, LANES, HEAD_V), F32)],
        compiler_params=_cparams(("arbitrary", "arbitrary", "arbitrary")),
        name="retention",
    )(decays, rq, rk, rv, sg, s_f, s_b)


def _dattn_kernel(dl_ref, q_ref, *refs, n_src, lam_init):
    k_refs = refs[:n_src]
    v_refs = refs[n_src:2 * n_src]
    o_ref = refs[2 * n_src]
    dl = dl_ref[...]
    lam = (jnp.exp(jnp.sum(dl[0:1] * dl[1:2], axis=-1, keepdims=True))
           - jnp.exp(jnp.sum(dl[2:3] * dl[3:4], axis=-1, keepdims=True)) + lam_init)
    q = q_ref[0].astype(F32)
    tq = q.shape[0]
    lane = lax.broadcasted_iota(I32, (tq, LANES), 1)
    ys = []
    for h in range(N_HEADS):
        qh = q[:, h * LANES:(h + 1) * LANES]
        qs = jnp.concatenate([jnp.where(lane < HEAD_DIM, qh, 0.0), jnp.where(lane >= HEAD_DIM, qh, 0.0)],
                             axis=0).astype(BF)
        ss = [_nt_dot(qs, kr[0, :, h * LANES:(h + 1) * LANES]) for kr in k_refs]
        m = ss[0].max(axis=-1, keepdims=True)
        for s in ss[1:]:
            m = jnp.maximum(m, s.max(axis=-1, keepdims=True))
        acc = jnp.zeros((2 * tq, 2 * HEAD_V), F32)
        for s, vr in zip(ss, v_refs):
            pexp = jnp.exp2(s - m).astype(BF)
            vh = vr[0, :, h * HEAD_V:(h + 1) * HEAD_V]
            acc = acc + jnp.dot(pexp, jnp.concatenate([vh, jnp.ones_like(vh)], axis=1), preferred_element_type=F32)
        o = acc[:, :HEAD_V] / acc[:, HEAD_V:]
        oh = o[:tq] - lam * o[tq:]
        ms = jnp.mean(oh * oh, axis=-1, keepdims=True)
        ys.append(oh * lax.rsqrt(ms + EPS) * (1.0 - lam_init))
    o_ref[0] = jnp.concatenate(ys, axis=1).astype(BF)


def _diff_attention(dq, ks, vs, dlam, lam_init, tq):
    b, n, w = dq.shape
    n_src = len(ks)
    kern = functools.partial(_dattn_kernel, n_src=n_src, lam_init=lam_init)
    kv_specs = [pl.BlockSpec((1, a.shape[1], w), lambda i, j: (i, 0, 0)) for a in (*ks, *vs)]
    return pl.pallas_call(
        kern,
        grid=(b, n // tq),
        in_specs=[pl.BlockSpec(dlam.shape, lambda i, j: (0, 0)),
                  pl.BlockSpec((1, tq, w), lambda i, j: (i, j, 0))] + kv_specs,
        out_specs=pl.BlockSpec((1, tq, w), lambda i, j: (i, j, 0)),
        out_shape=jax.ShapeDtypeStruct((b, n, w), BF),
        compiler_params=_cparams(("arbitrary", "arbitrary")),
        name="diff_attention",
    )(dlam, dq, *ks, *vs)


def _merge_kernel(f_ref, ro_ref, do_ref, gt_ref, x_ref, m2_ref, m3_ref, m4_ref, g_ref,
                  wf_ref, wr_ref, wd_ref, wo_ref, rth_ref, rtl_ref, x1_ref, h2_ref, aff_ref):
    d = x_ref.shape[1]
    t = (gt_ref[:, 0:d].astype(F32) * jnp.dot(f_ref[...], wf_ref[...], preferred_element_type=F32)
         + gt_ref[:, d:2 * d].astype(F32) * jnp.dot(ro_ref[...], wr_ref[...], preferred_element_type=F32)
         + gt_ref[:, 2 * d:3 * d].astype(F32) * jnp.dot(do_ref[...], wd_ref[...], preferred_element_type=F32))
    mix = jnp.dot(t.astype(BF), wo_ref[...], preferred_element_type=F32)
    x1 = x_ref[...] + m2_ref[0] * mix
    x1_ref[...] = x1
    ms = jnp.mean(x1 * x1, axis=-1, keepdims=True)
    h2 = x1 * lax.rsqrt(ms + EPS) * g_ref[...]
    h2 = h2 * (1.0 + m4_ref[0]) + m3_ref[0]
    hi = h2.astype(BF)
    lo = (h2 - hi.astype(F32)).astype(BF)
    h2_ref[...] = hi
    lt = _nt_dot(rth_ref[...], hi) + _nt_dot(rth_ref[...], lo) + _nt_dot(rtl_ref[...], hi)
    mx = lt.max(axis=0, keepdims=True)
    ex = jnp.exp(lt - mx)
    aff_ref[0] = ex / jnp.sum(ex, axis=0, keepdims=True)


def _merge(f, ro, do, gt, x2d, m2, m3, m4, g, wf, wr, wd, wo, rth, rtl, n_seq, tm):
    rows, d = x2d.shape
    tiles_per_b = n_seq // tm
    nb = m2.shape[0]
    bsz = rows // n_seq
    ne = rth.shape[0]

    def mod_map(i):
        return ((i // tiles_per_b) if nb > 1 else 0, 0, 0)

    const2 = lambda i: (0, 0)
    row_spec = lambda w: pl.BlockSpec((tm, w), lambda i: (i, 0))
    return pl.pallas_call(
        _merge_kernel,
        grid=(rows // tm,),
        in_specs=[row_spec(f.shape[1]), row_spec(ro.shape[1]), row_spec(do.shape[1]), row_spec(gt.shape[1]),
                  row_spec(d),
                  pl.BlockSpec((1, 1, d), mod_map), pl.BlockSpec((1, 1, d), mod_map), pl.BlockSpec((1, 1, d), mod_map),
                  pl.BlockSpec((1, d), const2),
                  pl.BlockSpec(wf.shape, const2), pl.BlockSpec(wr.shape, const2), pl.BlockSpec(wd.shape, const2),
                  pl.BlockSpec(wo.shape, const2), pl.BlockSpec(rth.shape, const2), pl.BlockSpec(rtl.shape, const2)],
        out_specs=[row_spec(d), row_spec(d),
                   pl.BlockSpec((1, ne, tm), lambda i: (i // tiles_per_b, 0, i % tiles_per_b))],
        out_shape=[jax.ShapeDtypeStruct((rows, d), F32), jax.ShapeDtypeStruct((rows, d), BF),
                   jax.ShapeDtypeStruct((bsz, ne, n_seq), F32)],
        compiler_params=_cparams(("arbitrary",)),
        name="merge_router",
    )(f, ro, do, gt, x2d, m2, m3, m4, g, wf, wr, wd, wo, rth, rtl)


def _route_kernel(a_ref, slot_ref, *, cap, blk):
    a = a_ref[...]
    ne, n = a.shape
    capf = float(cap)

    def enough(t):
        return jnp.sum(jnp.where(a >= t, 1.0, 0.0), axis=-1, keepdims=True) >= capf

    tiny = jnp.full((ne, 1), 2.0 ** -126, F32)
    found = enough(tiny)
    cur = tiny
    for step in (64, 32, 16, 8, 4, 2, 1):
        cand = cur * (2.0 ** step)
        cur = jnp.where(enough(cand), cand, cur)
    base = cur

    def mantissa_bit(_, carry):
        cur, stepv = carry
        cand = cur + stepv
        return jnp.where(enough(cand), cand, cur), stepv * 0.5

    cur, ulp = lax.fori_loop(0, 23, mantissa_bit, (cur, base * 0.5))
    lo = jnp.where(found, cur, 0.0)
    hi = jnp.where(found, cur + ulp * 2.0, tiny)

    def refine(_, carry):
        lo, hi = carry
        mid = lo + (hi - lo) * 0.5
        ok = enough(mid)
        return jnp.where(ok, mid, lo), jnp.where(ok, hi, mid)

    lo, hi = lax.fori_loop(0, 24, refine, (lo, hi))
    ri = lax.broadcasted_iota(I32, (blk, blk), 0)
    ci = lax.broadcasted_iota(I32, (blk, blk), 1)
    upper = jnp.where(ri < ci, 1.0, 0.0).astype(BF)

    def excl_cumsum(m):
        carry = jnp.zeros((ne, 1), F32)
        outs = []
        for j in range(n // blk):
            mb = m[:, j * blk:(j + 1) * blk]
            outs.append(jnp.dot(mb.astype(BF), upper, preferred_element_type=F32) + carry)
            carry = carry + jnp.sum(mb, axis=-1, keepdims=True)
        return jnp.concatenate(outs, axis=1)

    gt = a >= hi
    tie = (a >= lo) & (a < hi)
    need = capf - jnp.sum(jnp.where(gt, 1.0, 0.0), axis=-1, keepdims=True)
    sel = gt | (tie & (excl_cumsum(jnp.where(tie, 1.0, 0.0)) < need))
    slot = excl_cumsum(jnp.where(sel, 1.0, 0.0))
    slot_ref[...] = jnp.where(sel, slot.astype(I32), -1)


def _route(aff_t, cap):
    b, ne, n = aff_t.shape
    kern = functools.partial(_route_kernel, cap=cap, blk=min(256, n))
    slot = pl.pallas_call(
        kern,
        grid=(1,),
        in_specs=[pl.BlockSpec((b * ne, n), lambda i: (0, 0))],
        out_specs=pl.BlockSpec((b * ne, n), lambda i: (0, 0)),
        out_shape=jax.ShapeDtypeStruct((b * ne, n), I32),
        compiler_params=_cparams(("arbitrary",)),
        name="route",
    )(aff_t.reshape(b * ne, n))
    return slot.reshape(b, ne, n)


def _gather_kernel(slot_ref, h_ref, xs_ref, *, cap):
    s = slot_ref[0, 0]
    n = s.shape[1]
    sub = lax.broadcasted_iota(I32, (cap, n), 0)
    onehot = jnp.where(sub == s, 1.0, 0.0).astype(BF)
    xs_ref[0, 0] = jnp.dot(onehot, h_ref[0], preferred_element_type=F32).astype(BF)


def _gather(slot, h2, cap):
    b, ne, n = slot.shape
    d = h2.shape[2]
    kern = functools.partial(_gather_kernel, cap=cap)
    return pl.pallas_call(
        kern,
        grid=(b, ne),
        in_specs=[pl.BlockSpec((1, 1, 1, n), lambda i, e: (i, e, 0, 0)),
                  pl.BlockSpec((1, n, d), lambda i, e: (i, 0, 0))],
        out_specs=pl.BlockSpec((1, 1, cap, d), lambda i, e: (i, e, 0, 0)),
        out_shape=jax.ShapeDtypeStruct((b, ne, cap, d), BF),
        compiler_params=_cparams(("arbitrary", "arbitrary")),
        name="moe_gather",
    )(slot.reshape(b, ne, 1, n), h2)


def _ffn_kernel(*refs, n_src, n_fc):
    xs_refs = refs[:n_src]
    wg_ref, wu_ref, wd_ref = refs[n_src:n_src + 3]
    ye_refs = refs[n_src + 3:2 * n_src + 3]
    x_ref, hid_ref = refs[2 * n_src + 3:]
    j = pl.program_id(1)
    fc = wg_ref.shape[3]
    row_spans = []
    r0 = 0
    for xr in xs_refs:
        rows = xr.shape[0] * xr.shape[2]
        row_spans.append((r0, rows))
        r0 += rows

    @pl.when(j == 0)
    def _():
        for xr, (s0, rows) in zip(xs_refs, row_spans):
            x_ref[s0:s0 + rows, :] = xr[:, 0].reshape(rows, xr.shape[3])

    @pl.when(j < n_fc)
    def _():
        x = x_ref[...]
        g = jnp.dot(x, wg_ref[0, 0].astype(BF), preferred_element_type=F32)
        u = jnp.dot(x, wu_ref[0, 0].astype(BF), preferred_element_type=F32)
        hid_ref[j] = (g * _sigmoid(g) * u).astype(BF)

    @pl.when(j >= n_fc)
    def _():
        y = jnp.dot(hid_ref[0], wd_ref[0, 0, 0:fc, :].astype(BF), preferred_element_type=F32)
        for k in range(1, n_fc):
            y = y + jnp.dot(hid_ref[k], wd_ref[0, 0, k * fc:(k + 1) * fc, :].astype(BF), preferred_element_type=F32)
        for yr, (s0, rows) in zip(ye_refs, row_spans):
            yr[:, 0] = y[s0:s0 + rows].reshape(yr.shape[0], yr.shape[2], yr.shape[3]).astype(BF)


def _expert_ffn(xs_list, w_gate, w_up, w_down, layer, fchunk, ochunk):
    ne, d = xs_list[0].shape[1], xs_list[0].shape[3]
    f = w_gate.shape[3]
    n_fc, n_oc = f // fchunk, d // ochunk
    n_src = len(xs_list)
    total_rows = sum(x.shape[0] * x.shape[2] for x in xs_list)
    kern = functools.partial(_ffn_kernel, n_src=n_src, n_fc=n_fc)
    up_map = lambda e, j: (layer, e, 0, jnp.minimum(j, n_fc - 1))
    down_map = lambda e, j: (layer, e, 0, jnp.maximum(j - n_fc, 0))
    out_map = lambda e, j: (0, e, 0, jnp.maximum(j - n_fc, 0))
    outs = pl.pallas_call(
        kern,
        grid=(ne, n_fc + n_oc),
        in_specs=[pl.BlockSpec((x.shape[0], 1, x.shape[2], d), lambda e, j: (0, e, 0, 0), pipeline_mode=pl.Buffered(1))
                  for x in xs_list]
        + [pl.BlockSpec((1, 1, d, fchunk), up_map),
           pl.BlockSpec((1, 1, d, fchunk), up_map),
           pl.BlockSpec((1, 1, f, ochunk), down_map)],
        out_specs=[pl.BlockSpec((x.shape[0], 1, x.shape[2], ochunk), out_map) for x in xs_list],
        out_shape=[jax.ShapeDtypeStruct(x.shape, BF) for x in xs_list],
        scratch_shapes=[pltpu.VMEM((total_rows, d), BF), pltpu.VMEM((n_fc, total_rows, fchunk), BF)],
        compiler_params=_cparams(("arbitrary", "arbitrary")),
        name="expert_ffn",
    )(*xs_list, w_gate, w_up, w_down)
    return list(outs)


def _combine_kernel(x1_ref, m5_ref, slot_ref, aff_ref, ye_ref, o_ref, *, cap):
    st = slot_ref[0]
    at = aff_ref[0]
    tn, ne = st.shape
    lane = lax.broadcasted_iota(I32, (tn, cap), 1)
    acc = jnp.zeros(x1_ref.shape[1:], F32)
    for e in range(ne):
        onehot = jnp.where(st[:, e:e + 1] == lane, 1.0, 0.0).astype(BF)
        acc = acc + jnp.dot(onehot, ye_ref[0, e], preferred_element_type=F32) * at[:, e:e + 1]
    o_ref[0] = x1_ref[0] + m5_ref[0] * acc


def _combine(x1, m5, slot_tok, aff_tok, ye, tn):
    b, n, d = x1.shape
    ne, cap = ye.shape[1], ye.shape[2]
    nb = m5.shape[0]
    kern = functools.partial(_combine_kernel, cap=cap)
    return pl.pallas_call(
        kern,
        grid=(b, n // tn),
        in_specs=[pl.BlockSpec((1, tn, d), lambda i, j: (i, j, 0)),
                  pl.BlockSpec((1, 1, d), lambda i, j: (i if nb > 1 else 0, 0, 0)),
                  pl.BlockSpec((1, tn, ne), lambda i, j: (i, j, 0)),
                  pl.BlockSpec((1, tn, ne), lambda i, j: (i, j, 0)),
                  pl.BlockSpec((1, ne, cap, d), lambda i, j: (i, 0, 0, 0))],
        out_specs=pl.BlockSpec((1, tn, d), lambda i, j: (i, j, 0)),
        out_shape=jax.ShapeDtypeStruct((b, n, d), F32),
        compiler_params=_cparams(("arbitrary", "arbitrary")),
        name="moe_combine",
    )(x1, m5, slot_tok, aff_tok, ye)


def _rope_tables(n, use_rope):
    if not use_rope:
        return jnp.ones((n, LANES), F32), jnp.zeros((n, LANES), F32)
    rows = n // GRID_W
    row = jnp.repeat(jnp.arange(rows, dtype=F32), GRID_W)
    col = jnp.tile(jnp.arange(GRID_W, dtype=F32), rows)
    inv = ROPE_BASE ** (-jnp.arange(ROPE_FREQS_PER_AXIS, dtype=F32) / ROPE_FREQS_PER_AXIS)
    ang = jnp.concatenate([row[:, None] * inv, col[:, None] * inv], axis=-1)
    cos, sin = jnp.cos(ang), jnp.sin(ang)
    cos_t = jnp.tile(cos, (1, LANES // (HEAD_DIM // 2)))
    sin_t = jnp.tile(jnp.concatenate([-sin, sin], axis=-1), (1, LANES // HEAD_DIM))
    return cos_t, sin_t


def _dft_tables(n):
    n0 = 64
    n1 = n // n0
    k = np.arange(n, dtype=np.int64)[:, None]
    ang1 = jnp.asarray(2.0 * np.pi * ((k * np.arange(n1)[None, :]) % n1) / n1, F32)
    ang0 = jnp.asarray(2.0 * np.pi * ((k * np.arange(n0)[None, :]) % n) / n, F32)
    c1, s1, c0, s0 = jnp.cos(ang1), jnp.sin(ang1), jnp.cos(ang0), jnp.sin(ang0)
    scale = 1.0 / math.sqrt(n)
    cos_n = (c1[:, :, None] * c0[:, None, :] - s1[:, :, None] * s0[:, None, :]).reshape(n, n) * scale
    sin_n = (s1[:, :, None] * c0[:, None, :] + c1[:, :, None] * s0[:, None, :]).reshape(n, n) * scale
    return jnp.concatenate([cos_n, -sin_n], axis=1).astype(BF)


def _group_dft_tables(width):
    g = FNET_GROUP_DIM
    idx = np.arange(width)
    same = (idx[:, None] // g) == (idx[None, :] // g)
    ang = 2.0 * np.pi * (((idx[:, None] % g) * (idx[None, :] % g)) % g) / g
    scale = 1.0 / math.sqrt(g)
    bdc = np.where(same, np.cos(ang), 0.0) * scale
    bds = np.where(same, np.sin(ang), 0.0) * scale
    return jnp.asarray(bdc, F32).astype(BF), jnp.asarray(bds, F32).astype(BF)


def _group_mean_matrix(width):
    idx = np.arange(width)
    same = (idx[:, None] // HEAD_DIM) == (idx[None, :] // HEAD_DIM)
    return jnp.asarray(np.where(same, 1.0 / HEAD_DIM, 0.0), BF)


def _mixer_inputs(x3, mods, g_attn_l, w_in_bf, tables, qg, kg, gm):
    b, n, d = x3.shape
    tm = min(512, n)
    outs = _in_projection(x3.reshape(b * n, d), mods[0], mods[1], g_attn_l, w_in_bf, tables[0], tables[1],
                          qg, kg, gm, n, tm)
    return [o.reshape(b, n, o.shape[1]) for o in outs]


def _moe(sets, w_gate, w_up, w_down, layer):
    slots, xss = [], []
    for x1, h2, aff_t, _ in sets:
        cap = EC_FACTOR * x1.shape[1] // N_EXPERTS
        slot = _route(aff_t, cap)
        slots.append(slot)
        xss.append(_gather(slot, h2, cap))
    yes = _expert_ffn(xss, w_gate, w_up, w_down, layer, 512, 256)
    return [_combine(x1, m5, jnp.swapaxes(slot, 1, 2), jnp.swapaxes(aff_t, 1, 2), ye, min(512, x1.shape[1]))
            for (x1, _, aff_t, m5), slot, ye in zip(sets, slots, yes)]


def kernel(x, c, ctx, c_ctx, w_mod, b_mod, g_attn, g_ffn, w_in, ret_decay, diff_qn, diff_kn, diff_lambda,
           w_fnet_o, w_ret_o, w_diff_o, w_out, w_router, w_exp_gate, w_exp_up, w_exp_down):
    bsz, n, d = x.shape
    n_ctx = ctx.shape[1]
    depth = w_mod.shape[0]

    pad = (-(bsz + 1)) % 8
    cvecs = jnp.concatenate([c, c_ctx[None, :], jnp.zeros((pad, d), F32)], axis=0)
    mods = _modulation(cvecs, w_mod, b_mod)

    rope_lat = _rope_tables(n, True)
    rope_ctx = _rope_tables(n_ctx, False)
    cs_lat, cs_ctx = _dft_tables(n), _dft_tables(n_ctx)
    bdc, bds = _group_dft_tables(_W_FO)
    gm = _group_mean_matrix(_W_DQ)
    zero_state = jnp.zeros((bsz, N_HEADS, HEAD_DIM, HEAD_V), F32)
    ret_chunk = 512

    xc = ctx
    for layer in range(depth):
        last = layer == depth - 1
        lam_init = 0.8 - 0.6 * math.exp(-0.3 * layer)
        mx = [mods[layer, :bsz, j * d:(j + 1) * d].reshape(bsz, 1, d) for j in range(N_MOD)]
        mc = [mods[layer, bsz:bsz + 1, j * d:(j + 1) * d].reshape(1, 1, d) for j in range(N_MOD)]
        w_in_bf = w_in[layer].astype(BF)
        qg = jnp.tile(diff_qn[layer], _W_DQ // HEAD_DIM)[None, :]
        kg = jnp.tile(diff_kn[layer], _W_DK // HEAD_DIM)[None, :]
        g_a = g_attn[layer][None, :]
        g_f = g_ffn[layer][None, :]
        decays = ret_decay[layer]
        dlam = diff_lambda[layer]
        branch_w = (w_fnet_o[layer].astype(BF), w_ret_o[layer].astype(BF), w_diff_o[layer].astype(BF),
                    w_out[layer].astype(BF))
        rt = w_router[layer].T
        rth = rt.astype(BF)
        rtl = (rt - rth.astype(F32)).astype(BF)
        moe_sets = []

        fo_c, rq_c, rk_c, rv_c, sg_c, dq_c, dk_c, dv_c, gt_c = _mixer_inputs(xc, mc, g_a, w_in_bf, rope_ctx, qg, kg, gm)
        ro_c, s_f, s_b = _retention(rq_c, rk_c, rv_c, sg_c, decays, zero_state, zero_state, min(ret_chunk, n_ctx))
        if not last:
            f_c = _fourier_mix(fo_c, cs_ctx, bdc, bds, min(512, n_ctx))
            do_c = _diff_attention(dq_c, [dk_c], [dv_c], dlam, lam_init, min(256, n_ctx))
            rows_c = bsz * n_ctx
            x1_c, h2_c, aff_c = _merge(f_c.reshape(rows_c, -1), ro_c.reshape(rows_c, -1), do_c.reshape(rows_c, -1),
                                       gt_c.reshape(rows_c, -1), xc.reshape(rows_c, d), mc[2], mc[3], mc[4], g_f,
                                       *branch_w, rth, rtl, n_ctx, min(512, n_ctx))
            moe_sets.append((x1_c.reshape(bsz, n_ctx, d), h2_c.reshape(bsz, n_ctx, d), aff_c, mc[5]))

        fo, rq, rk, rv, sg, dq, dk, dv, gt = _mixer_inputs(x, mx, g_a, w_in_bf, rope_lat, qg, kg, gm)
        ro, _, _ = _retention(rq, rk, rv, sg, decays, s_f, s_b, min(ret_chunk, n))
        f = _fourier_mix(fo, cs_lat, bdc, bds, min(512, n))
        do = _diff_attention(dq, [dk, dk_c], [dv, dv_c], dlam, lam_init, 256)
        rows = bsz * n
        x1, h2, aff = _merge(f.reshape(rows, -1), ro.reshape(rows, -1), do.reshape(rows, -1), gt.reshape(rows, -1),
                             x.reshape(rows, d), mx[2], mx[3], mx[4], g_f, *branch_w, rth, rtl, n, min(512, n))
        moe_sets.append((x1.reshape(bsz, n, d), h2.reshape(bsz, n, d), aff, mx[5]))
        moe_out = _moe(moe_sets, w_exp_gate, w_exp_up, w_exp_down, layer)
        x = moe_out[-1]
        if not last:
            xc = moe_out[0]
    return x
```

```python
import functools
import math

import jax
import jax.numpy as jnp
import numpy as np
from jax import lax
from jax.experimental import pallas as pl
from jax.experimental.pallas import tpu as pltpu

F32 = jnp.float32
BF = jnp.bfloat16
I32 = jnp.int32

GRID_W = 64
HEAD_DIM = 64
ROPE_FREQS_PER_AXIS = HEAD_DIM // 4
ROPE_BASE = 10000.0
EPS = 1e-6
LOG2E = 1.4426950408889634
FNET_GROUP_DIM = 64
N_HEADS = 4
HEAD_V = 128
N_EXPERTS = 16
EC_FACTOR = 2
N_MOD = 6
RET_SUB = 256
LANES = 128
VMEM_LIMIT = 56 * 1024 * 1024

_W_FO, _W_RQ, _W_RK, _W_RV, _W_RG, _W_DQ, _W_DK, _W_DV = 512, 256, 256, 512, 512, 512, 512, 512


def _cparams(sem):
    return pltpu.CompilerParams(dimension_semantics=sem, vmem_limit_bytes=VMEM_LIMIT)


def _sigmoid(v):
    return 1.0 / (1.0 + jnp.exp(-v))


def _nt_dot(a, b):
    return lax.dot_general(a, b, (((1,), (1,)), ((), ())), preferred_element_type=F32)


def _tn_dot(a, b):
    return lax.dot_general(a, b, (((0,), (0,)), ((), ())), preferred_element_type=F32)


def _mod_kernel(c_ref, w_ref, b_ref, o_ref):
    cv = c_ref[...]
    s = cv * _sigmoid(cv)
    o_ref[0] = jnp.dot(s.astype(BF), w_ref[0].astype(BF), preferred_element_type=F32) + b_ref[0]


def _modulation(cvecs, w_mod, b_mod):
    depth, d, wd = w_mod.shape
    rows = cvecs.shape[0]
    tn = 1536
    return pl.pallas_call(
        _mod_kernel,
        grid=(depth, wd // tn),
        in_specs=[pl.BlockSpec((rows, d), lambda l, j: (0, 0)),
                  pl.BlockSpec((1, d, tn), lambda l, j: (l, 0, j)),
                  pl.BlockSpec((1, 1, tn), lambda l, j: (l, 0, j))],
        out_specs=pl.BlockSpec((1, rows, tn), lambda l, j: (l, 0, j)),
        out_shape=jax.ShapeDtypeStruct((depth, rows, wd), F32),
        compiler_params=_cparams(("arbitrary", "arbitrary")),
        name="modulation",
    )(cvecs, w_mod, b_mod.reshape(depth, 1, wd))


def _rope(x, cos_t, sin_t):
    lane = lax.broadcasted_iota(I32, (x.shape[0], LANES), 1)
    first = (lane & 63) < 32
    outs = []
    for j in range(x.shape[1] // LANES):
        xc = x[:, j * LANES:(j + 1) * LANES]
        sw = jnp.where(first, pltpu.roll(xc, LANES - 32, 1), pltpu.roll(xc, 32, 1))
        outs.append(xc * cos_t + sw * sin_t)
    return jnp.concatenate(outs, axis=1)


def _group_rms(x, gm, gain):
    ms = jnp.dot((x * x).astype(BF), gm, preferred_element_type=F32)
    return x * lax.rsqrt(ms + EPS) * gain


def _inproj_kernel(x_ref, shift_ref, scale_ref, g_ref, w_ref, cos_ref, sin_ref, qg_ref, kg_ref, gm_ref,
                   fo_ref, rq_ref, rk_ref, rv_ref, sg_ref, dq_ref, dk_ref, dv_ref, gt_ref):
    x = x_ref[...]
    d = x.shape[1]
    ms = jnp.mean(x * x, axis=-1, keepdims=True)
    h = x * lax.rsqrt(ms + EPS) * g_ref[...]
    h = h * (1.0 + scale_ref[0]) + shift_ref[0]
    hb = h.astype(BF)
    cos_t = cos_ref[...]
    sin_t = sin_ref[...]
    gm = gm_ref[...]

    def proj(a, width):
        return jnp.dot(hb, w_ref[:, a:a + width], preferred_element_type=F32)

    a = 0
    fo_ref[...] = proj(a, _W_FO).astype(BF)
    a += _W_FO
    rq_ref[...] = _rope(proj(a, _W_RQ) * (HEAD_DIM ** -0.5), cos_t, sin_t).astype(BF)
    a += _W_RQ
    rk_ref[...] = _rope(proj(a, _W_RK), cos_t, sin_t).astype(BF)
    a += _W_RK
    rv_ref[...] = proj(a, _W_RV).astype(BF)
    a += _W_RV
    rg = proj(a, _W_RG)
    sg_ref[...] = (rg * _sigmoid(rg)).astype(BF)
    a += _W_RG
    dq = _group_rms(proj(a, _W_DQ), gm, qg_ref[...])
    dq_ref[...] = (_rope(dq, cos_t, sin_t) * (HEAD_DIM ** -0.5 * LOG2E)).astype(BF)
    a += _W_DQ
    dk = _group_rms(proj(a, _W_DK), gm, kg_ref[...])
    dk_ref[...] = _rope(dk, cos_t, sin_t).astype(BF)
    a += _W_DK
    dv_ref[...] = proj(a, _W_DV).astype(BF)
    a += _W_DV
    for j in range(3):
        gl = proj(a + j * d, d)
        gt_ref[:, j * d:(j + 1) * d] = _sigmoid(gl).astype(BF)


def _in_projection(x2d, shift, scale, g, w_in_bf, cos_t, sin_t, qg, kg, gm, n_seq, tm):
    rows, d = x2d.shape
    tiles_per_b = n_seq // tm
    nb = shift.shape[0]
    win = w_in_bf.shape[1]
    widths = (_W_FO, _W_RQ, _W_RK, _W_RV, _W_RG, _W_DQ, _W_DK, _W_DV, 3 * d)

    def mod_map(i):
        return ((i // tiles_per_b) if nb > 1 else 0, 0, 0)

    def pos_map(i):
        return (i % tiles_per_b, 0)

    const2 = lambda i: (0, 0)
    return pl.pallas_call(
        _inproj_kernel,
        grid=(rows // tm,),
        in_specs=[pl.BlockSpec((tm, d), lambda i: (i, 0)),
                  pl.BlockSpec((1, 1, d), mod_map),
                  pl.BlockSpec((1, 1, d), mod_map),
                  pl.BlockSpec((1, d), const2),
                  pl.BlockSpec((d, win), const2, pipeline_mode=pl.Buffered(1)),
                  pl.BlockSpec((tm, LANES), pos_map),
                  pl.BlockSpec((tm, LANES), pos_map),
                  pl.BlockSpec((1, _W_DQ), const2),
                  pl.BlockSpec((1, _W_DK), const2),
                  pl.BlockSpec((_W_DQ, _W_DQ), const2)],
        out_specs=[pl.BlockSpec((tm, w), lambda i: (i, 0)) for w in widths],
        out_shape=[jax.ShapeDtypeStruct((rows, w), BF) for w in widths],
        compiler_params=_cparams(("arbitrary",)),
        name="in_projection",
    )(x2d, shift, scale, g, w_in_bf, cos_t, sin_t, qg, kg, gm)


def _fourier_kernel(x_ref, cs_ref, bc_ref, bs_ref, o_ref, z_ref):
    n = x_ref.shape[1]

    @pl.when(pl.program_id(1) == 0)
    def _():
        x = x_ref[0]
        z_ref[0:n, :] = jnp.dot(x, bc_ref[...], preferred_element_type=F32).astype(BF)
        z_ref[n:2 * n, :] = jnp.dot(x, bs_ref[...], preferred_element_type=F32).astype(BF)

    o_ref[0] = jnp.dot(cs_ref[...], z_ref[...], preferred_element_type=F32).astype(BF)


def _fourier_mix(fo, cs, bdc, bds, tr):
    b, n, w = fo.shape
    return pl.pallas_call(
        _fourier_kernel,
        grid=(b, n // tr),
        in_specs=[pl.BlockSpec((1, n, w), lambda i, j: (i, 0, 0)),
                  pl.BlockSpec((tr, 2 * n), lambda i, j: (j, 0)),
                  pl.BlockSpec((w, w), lambda i, j: (0, 0)),
                  pl.BlockSpec((w, w), lambda i, j: (0, 0))],
        out_specs=pl.BlockSpec((1, tr, w), lambda i, j: (i, j, 0)),
        out_shape=jax.ShapeDtypeStruct((b, n, w), BF),
        scratch_shapes=[pltpu.VMEM((2 * n, w), BF)],
        compiler_params=_cparams(("arbitrary", "arbitrary")),
        name="fourier_mix",
    )(fo, cs, bdc, bds)


def _ret_kernel(lg_ref, q_ref, k_ref, v_ref, sg_ref, sf_ref, sb_ref, o_ref, sfo_ref, sbo_ref,
                acc_ref, tmp_ref, st_ref, dm_ref, qd_ref, qm_ref, kd_ref, cd_ref, *, chunk, n_chunks, sub):
    b = pl.program_id(0)
    p = pl.program_id(1)
    c = pl.program_id(2)
    sf = float(sub)
    n_sub = chunk // sub

    @pl.when((b == 0) & (p == 0) & (c == 0))
    def _tables():
        ii = lax.broadcasted_iota(I32, (sub, sub), 0).astype(F32)
        jj = lax.broadcasted_iota(I32, (sub, sub), 1).astype(F32)
        pos = lax.broadcasted_iota(I32, (sub, LANES), 0).astype(F32)
        lane = lax.broadcasted_iota(I32, (sub, LANES), 1)
        for dr in range(2):
            if dr == 0:
                rel, keep = ii - jj, ii >= jj
                qe, ke = pos + 1.0, (sf - 1.0) - pos
            else:
                rel, keep = jj - ii, jj > ii
                qe, ke = sf - pos, pos
            for hp in range(N_HEADS // 2):
                lg0 = jnp.full((sub, LANES), lg_ref[dr, 2 * hp], F32)
                lg1 = jnp.full((sub, LANES), lg_ref[dr, 2 * hp + 1], F32)
                lgp = jnp.where(lane < HEAD_DIM, lg0, lg1)
                kd_ref[dr, hp] = jnp.exp(lgp * ke)
            for h in range(N_HEADS):
                lgs = lg_ref[dr, h]
                in_head = (lane >= (h % 2) * HEAD_DIM) & (lane < (h % 2 + 1) * HEAD_DIM)
                dm_ref[dr, h] = jnp.where(keep, jnp.exp(jnp.full((sub, sub), lgs, F32) * jnp.maximum(rel, 0.0)), 0.0)
                qd_ref[dr, h] = jnp.where(in_head, jnp.exp(jnp.full((sub, LANES), lgs, F32) * qe), 0.0)
                cd_ref[dr, h] = jnp.exp(jnp.full((LANES, LANES), lgs, F32) * sf)
        for h in range(N_HEADS):
            in_head = (lane >= (h % 2) * HEAD_DIM) & (lane < (h % 2 + 1) * HEAD_DIM)
            qm_ref[h] = jnp.where(in_head, 1.0, 0.0)

    def load_state(src_ref):
        z = jnp.zeros((HEAD_DIM, HEAD_V), F32)
        for h in range(N_HEADS):
            s = src_ref[0, h]
            st_ref[h] = jnp.concatenate([s, z], axis=0) if h % 2 == 0 else jnp.concatenate([z, s], axis=0)

    @pl.when((c == 0) & (p == 0))
    def _():
        load_state(sf_ref)

    @pl.when((c == 0) & (p == 1))
    def _():
        load_state(sb_ref)

    sts = [st_ref[h] for h in range(N_HEADS)]
    for u in range(n_sub):
        r0 = pl.multiple_of(jnp.where(p == 0, u, n_sub - 1 - u) * sub, sub)
        rows = pl.ds(r0, sub)
        q = q_ref[0, rows, :]
        k = k_ref[0, rows, :]
        v = v_ref[0, rows, :]
        outs = []
        for hp in range(N_HEADS // 2):
            qp = q[:, hp * LANES:(hp + 1) * LANES].astype(F32)
            kp = k[:, hp * LANES:(hp + 1) * LANES]
            kdp = (kp.astype(F32) * kd_ref[p, hp]).astype(BF)
            for h in (2 * hp, 2 * hp + 1):
                vh = v[:, h * HEAD_V:(h + 1) * HEAD_V]
                qm = (qp * qm_ref[h]).astype(BF)
                qdq = (qp * qd_ref[p, h]).astype(BF)
                s = _nt_dot(qm, kp) * dm_ref[p, h]
                o = (jnp.dot(s.astype(BF), vh, preferred_element_type=F32)
                     + jnp.dot(qdq, sts[h].astype(BF), preferred_element_type=F32))
                sts[h] = sts[h] * cd_ref[p, h] + _tn_dot(kdp, vh)
                outs.append(o)
        tmp_ref[rows, :] = jnp.concatenate(outs, axis=1)
    for h in range(N_HEADS):
        st_ref[h] = sts[h]
    idx = jnp.where(p == 0, c, n_chunks - 1 - c)

    @pl.when(p == 0)
    def _():
        acc_ref[idx] = tmp_ref[...]

    @pl.when(p == 1)
    def _():
        tot = acc_ref[idx] + tmp_ref[...]
        sg = sg_ref[0].astype(F32)
        ys = []
        for h in range(N_HEADS):
            oh = tot[:, h * HEAD_V:(h + 1) * HEAD_V]
            mu = jnp.mean(oh, axis=-1, keepdims=True)
            dlt = oh - mu
            var = jnp.mean(dlt * dlt, axis=-1, keepdims=True)
            ys.append(dlt * lax.rsqrt(var + EPS))
        o_ref[0] = (jnp.concatenate(ys, axis=1) * sg).astype(BF)

    def store_state(dst_ref):
        for h in range(N_HEADS):
            r0 = (h % 2) * HEAD_DIM
            dst_ref[0, h] = st_ref[h, r0:r0 + HEAD_DIM, :]

    @pl.when((c == n_chunks - 1) & (p == 0))
    def _():
        store_state(sfo_ref)

    @pl.when((c == n_chunks - 1) & (p == 1))
    def _():
        store_state(sbo_ref)


def _retention(rq, rk, rv, sg, decays, s_f, s_b, chunk):
    b, n, _ = rq.shape
    n_chunks = n // chunk
    wv = rv.shape[2]

    def seq_map(i, p, c):
        return (i, jnp.where(p == 0, c, n_chunks - 1 - c), 0)

    def out_map(i, p, c):
        return (i, jnp.where(p == 0, n_chunks - 1, n_chunks - 1 - c), 0)

    st_spec = pl.BlockSpec((1, N_HEADS, HEAD_DIM, HEAD_V), lambda i, p, c: (i, 0, 0, 0))
    st_shape = jax.ShapeDtypeStruct((b, N_HEADS, HEAD_DIM, HEAD_V), F32)
    sub = min(RET_SUB, chunk)
    kern = functools.partial(_ret_kernel, chunk=chunk, n_chunks=n_chunks, sub=sub)
    return pl.pallas_call(
        kern,
        grid=(b, 2, n_chunks),
        in_specs=[pl.BlockSpec(memory_space=pltpu.SMEM),
                  pl.BlockSpec((1, chunk, rq.shape[2]), seq_map),
                  pl.BlockSpec((1, chunk, rk.shape[2]), seq_map),
                  pl.BlockSpec((1, chunk, wv), seq_map),
                  pl.BlockSpec((1, chunk, wv), seq_map),
                  st_spec, st_spec],
        out_specs=[pl.BlockSpec((1, chunk, wv), out_map), st_spec, st_spec],
        out_shape=[jax.ShapeDtypeStruct((b, n, wv), BF), st_shape, st_shape],
        scratch_shapes=[pltpu.VMEM((n_chunks, chunk, wv), F32),
                        pltpu.VMEM((chunk, wv), F32),
                        pltpu.VMEM((N_HEADS, LANES, HEAD_V), F32),
                        pltpu.VMEM((2, N_HEADS, sub, sub), F32),
                        pltpu.VMEM((2, N_HEADS, sub, LANES), F32),
                        pltpu.VMEM((N_HEADS, sub, LANES), F32),
                        pltpu.VMEM((2, N_HEADS // 2, sub, LANES), F32),
                        pltpu.VMEM((2, N_HEADS, LANES, HEAD_V), F32)],
        compiler_params=_cparams(("arbitrary", "arbitrary", "arbitrary")),
        name="retention",
    )(decays, rq, rk, rv, sg, s_f, s_b)


def _dattn_kernel(dl_ref, q_ref, *refs, n_src, lam_init):
    k_refs = refs[:n_src]
    v_refs = refs[n_src:2 * n_src]
    o_ref = refs[2 * n_src]
    dl = dl_ref[...]
    lam = (jnp.exp(jnp.sum(dl[0:1] * dl[1:2], axis=-1, keepdims=True))
           - jnp.exp(jnp.sum(dl[2:3] * dl[3:4], axis=-1, keepdims=True)) + lam_init)
    q = q_ref[0].astype(F32)
    tq = q.shape[0]
    lane = lax.broadcasted_iota(I32, (tq, LANES), 1)
    ys = []
    for h in range(N_HEADS):
        qh = q[:, h * LANES:(h + 1) * LANES]
        qs = jnp.concatenate([jnp.where(lane < HEAD_DIM, qh, 0.0), jnp.where(lane >= HEAD_DIM, qh, 0.0)],
                             axis=0).astype(BF)
        ss = [_nt_dot(qs, kr[0, :, h * LANES:(h + 1) * LANES]) for kr in k_refs]
        m = ss[0].max(axis=-1, keepdims=True)
        for s in ss[1:]:
            m = jnp.maximum(m, s.max(axis=-1, keepdims=True))
        acc = jnp.zeros((2 * tq, 2 * HEAD_V), F32)
        for s, vr in zip(ss, v_refs):
            pexp = jnp.exp2(s - m).astype(BF)
            vh = vr[0, :, h * HEAD_V:(h + 1) * HEAD_V]
            acc = acc + jnp.dot(pexp, jnp.concatenate([vh, jnp.ones_like(vh)], axis=1), preferred_element_type=F32)
        o = acc[:, :HEAD_V] / acc[:, HEAD_V:]
        oh = o[:tq] - lam * o[tq:]
        ms = jnp.mean(oh * oh, axis=-1, keepdims=True)
        ys.append(oh * lax.rsqrt(ms + EPS) * (1.0 - lam_init))
    o_ref[0] = jnp.concatenate(ys, axis=1).astype(BF)


def _diff_attention(dq, ks, vs, dlam, lam_init, tq):
    b, n, w = dq.shape
    n_src = len(ks)
    kern = functools.partial(_dattn_kernel, n_src=n_src, lam_init=lam_init)
    kv_specs = [pl.BlockSpec((1, a.shape[1], w), lambda i, j: (i, 0, 0)) for a in (*ks, *vs)]
    return pl.pallas_call(
        kern,
        grid=(b, n // tq),
        in_specs=[pl.BlockSpec(dlam.shape, lambda i, j: (0, 0)),
                  pl.BlockSpec((1, tq, w), lambda i, j: (i, j, 0))] + kv_specs,
        out_specs=pl.BlockSpec((1, tq, w), lambda i, j: (i, j, 0)),
        out_shape=jax.ShapeDtypeStruct((b, n, w), BF),
        compiler_params=_cparams(("arbitrary", "arbitrary")),
        name="diff_attention",
    )(dlam, dq, *ks, *vs)


def _merge_kernel(f_ref, ro_ref, do_ref, gt_ref, x_ref, m2_ref, m3_ref, m4_ref, g_ref,
                  wf_ref, wr_ref, wd_ref, wo_ref, rth_ref, rtl_ref, x1_ref, h2_ref, aff_ref):
    d = x_ref.shape[1]
    t = (gt_ref[:, 0:d].astype(F32) * jnp.dot(f_ref[...], wf_ref[...], preferred_element_type=F32)
         + gt_ref[:, d:2 * d].astype(F32) * jnp.dot(ro_ref[...], wr_ref[...], preferred_element_type=F32)
         + gt_ref[:, 2 * d:3 * d].astype(F32) * jnp.dot(do_ref[...], wd_ref[...], preferred_element_type=F32))
    mix = jnp.dot(t.astype(BF), wo_ref[...], preferred_element_type=F32)
    x1 = x_ref[...] + m2_ref[0] * mix
    x1_ref[...] = x1
    ms = jnp.mean(x1 * x1, axis=-1, keepdims=True)
    h2 = x1 * lax.rsqrt(ms + EPS) * g_ref[...]
    h2 = h2 * (1.0 + m4_ref[0]) + m3_ref[0]
    hi = h2.astype(BF)
    lo = (h2 - hi.astype(F32)).astype(BF)
    h2_ref[...] = hi
    lt = _nt_dot(rth_ref[...], hi) + _nt_dot(rth_ref[...], lo) + _nt_dot(rtl_ref[...], hi)
    mx = lt.max(axis=0, keepdims=True)
    ex = jnp.exp(lt - mx)
    aff_ref[0] = ex / jnp.sum(ex, axis=0, keepdims=True)


def _merge(f, ro, do, gt, x2d, m2, m3, m4, g, wf, wr, wd, wo, rth, rtl, n_seq, tm):
    rows, d = x2d.shape
    tiles_per_b = n_seq // tm
    nb = m2.shape[0]
    bsz = rows // n_seq
    ne = rth.shape[0]

    def mod_map(i):
        return ((i // tiles_per_b) if nb > 1 else 0, 0, 0)

    const2 = lambda i: (0, 0)
    row_spec = lambda w: pl.BlockSpec((tm, w), lambda i: (i, 0))
    return pl.pallas_call(
        _merge_kernel,
        grid=(rows // tm,),
        in_specs=[row_spec(f.shape[1]), row_spec(ro.shape[1]), row_spec(do.shape[1]), row_spec(gt.shape[1]),
                  row_spec(d),
                  pl.BlockSpec((1, 1, d), mod_map), pl.BlockSpec((1, 1, d), mod_map), pl.BlockSpec((1, 1, d), mod_map),
                  pl.BlockSpec((1, d), const2),
                  pl.BlockSpec(wf.shape, const2), pl.BlockSpec(wr.shape, const2), pl.BlockSpec(wd.shape, const2),
                  pl.BlockSpec(wo.shape, const2), pl.BlockSpec(rth.shape, const2), pl.BlockSpec(rtl.shape, const2)],
        out_specs=[row_spec(d), row_spec(d),
                   pl.BlockSpec((1, ne, tm), lambda i: (i // tiles_per_b, 0, i % tiles_per_b))],
        out_shape=[jax.ShapeDtypeStruct((rows, d), F32), jax.ShapeDtypeStruct((rows, d), BF),
                   jax.ShapeDtypeStruct((bsz, ne, n_seq), F32)],
        compiler_params=_cparams(("arbitrary",)),
        name="merge_router",
    )(f, ro, do, gt, x2d, m2, m3, m4, g, wf, wr, wd, wo, rth, rtl)


def _route_kernel(a_ref, slot_ref, *, cap, blk):
    a = a_ref[...]
    ne, n = a.shape
    capf = float(cap)

    def enough(t):
        return jnp.sum(jnp.where(a >= t, 1.0, 0.0), axis=-1, keepdims=True) >= capf

    tiny = jnp.full((ne, 1), 2.0 ** -126, F32)
    found = enough(tiny)
    cur = tiny
    for step in (64, 32, 16, 8, 4, 2, 1):
        cand = cur * (2.0 ** step)
        cur = jnp.where(enough(cand), cand, cur)
    base = cur

    def mantissa_bit(_, carry):
        cur, stepv = carry
        cand = cur + stepv
        return jnp.where(enough(cand), cand, cur), stepv * 0.5

    cur, ulp = lax.fori_loop(0, 23, mantissa_bit, (cur, base * 0.5))
    lo = jnp.where(found, cur, 0.0)
    hi = jnp.where(found, cur + ulp * 2.0, tiny)

    def refine(_, carry):
        lo, hi = carry
        mid = lo + (hi - lo) * 0.5
        ok = enough(mid)
        return jnp.where(ok, mid, lo), jnp.where(ok, hi, mid)

    lo, hi = lax.fori_loop(0, 24, refine, (lo, hi))
    ri = lax.broadcasted_iota(I32, (blk, blk), 0)
    ci = lax.broadcasted_iota(I32, (blk, blk), 1)
    upper = jnp.where(ri < ci, 1.0, 0.0).astype(BF)

    def excl_cumsum(m):
        carry = jnp.zeros((ne, 1), F32)
        outs = []
        for j in range(n // blk):
            mb = m[:, j * blk:(j + 1) * blk]
            outs.append(jnp.dot(mb.astype(BF), upper, preferred_element_type=F32) + carry)
            carry = carry + jnp.sum(mb, axis=-1, keepdims=True)
        return jnp.concatenate(outs, axis=1)

    gt = a >= hi
    tie = (a >= lo) & (a < hi)
    need = capf - jnp.sum(jnp.where(gt, 1.0, 0.0), axis=-1, keepdims=True)
    sel = gt | (tie & (excl_cumsum(jnp.where(tie, 1.0, 0.0)) < need))
    slot = excl_cumsum(jnp.where(sel, 1.0, 0.0))
    slot_ref[...] = jnp.where(sel, slot.astype(I32), -1)


def _route(aff_t, cap):
    b, ne, n = aff_t.shape
    kern = functools.partial(_route_kernel, cap=cap, blk=min(256, n))
    slot = pl.pallas_call(
        kern,
        grid=(1,),
        in_specs=[pl.BlockSpec((b * ne, n), lambda i: (0, 0))],
        out_specs=pl.BlockSpec((b * ne, n), lambda i: (0, 0)),
        out_shape=jax.ShapeDtypeStruct((b * ne, n), I32),
        compiler_params=_cparams(("arbitrary",)),
        name="route",
    )(aff_t.reshape(b * ne, n))
    return slot.reshape(b, ne, n)


def _gather_kernel(slot_ref, h_ref, xs_ref, *, cap):
    n = slot_ref.shape[3]
    sub = lax.broadcasted_iota(I32, (cap, n), 0)
    hb = h_ref[0]
    for g in range(slot_ref.shape[1]):
        onehot = jnp.where(sub == slot_ref[0, g], 1.0, 0.0).astype(BF)
        xs_ref[0, g] = jnp.dot(onehot, hb, preferred_element_type=F32).astype(BF)


def _gather(slot, h2, cap):
    b, ne, n = slot.shape
    d = h2.shape[2]
    eg = max(1, min(ne, 1024 // cap))
    kern = functools.partial(_gather_kernel, cap=cap)
    return pl.pallas_call(
        kern,
        grid=(b, ne // eg),
        in_specs=[pl.BlockSpec((1, eg, 1, n), lambda i, e: (i, e, 0, 0)),
                  pl.BlockSpec((1, n, d), lambda i, e: (i, 0, 0))],
        out_specs=pl.BlockSpec((1, eg, cap, d), lambda i, e: (i, e, 0, 0)),
        out_shape=jax.ShapeDtypeStruct((b, ne, cap, d), BF),
        compiler_params=_cparams(("arbitrary", "arbitrary")),
        name="moe_gather",
    )(slot.reshape(b, ne, 1, n), h2)


def _ffn_kernel(*refs, n_src, n_fc):
    xs_refs = refs[:n_src]
    wg_ref, wu_ref, wd_ref = refs[n_src:n_src + 3]
    ye_refs = refs[n_src + 3:2 * n_src + 3]
    x_ref, hid_ref = refs[2 * n_src + 3:]
    j = pl.program_id(1)
    fc = wg_ref.shape[3]
    row_spans = []
    r0 = 0
    for xr in xs_refs:
        rows = xr.shape[0] * xr.shape[2]
        row_spans.append((r0, rows))
        r0 += rows

    @pl.when(j == 0)
    def _():
        for xr, (s0, rows) in zip(xs_refs, row_spans):
            x_ref[s0:s0 + rows, :] = xr[:, 0].reshape(rows, xr.shape[3])

    @pl.when(j < n_fc)
    def _():
        x = x_ref[...]
        g = jnp.dot(x, wg_ref[0, 0].astype(BF), preferred_element_type=F32)
        u = jnp.dot(x, wu_ref[0, 0].astype(BF), preferred_element_type=F32)
        hid_ref[j] = (g * _sigmoid(g) * u).astype(BF)

    @pl.when(j >= n_fc)
    def _():
        y = jnp.dot(hid_ref[0], wd_ref[0, 0, 0:fc, :].astype(BF), preferred_element_type=F32)
        for k in range(1, n_fc):
            y = y + jnp.dot(hid_ref[k], wd_ref[0, 0, k * fc:(k + 1) * fc, :].astype(BF), preferred_element_type=F32)
        for yr, (s0, rows) in zip(ye_refs, row_spans):
            yr[:, 0] = y[s0:s0 + rows].reshape(yr.shape[0], yr.shape[2], yr.shape[3]).astype(BF)


def _expert_ffn(xs_list, w_gate, w_up, w_down, layer, fchunk, ochunk):
    ne, d = xs_list[0].shape[1], xs_list[0].shape[3]
    f = w_gate.shape[3]
    n_fc, n_oc = f // fchunk, d // ochunk
    n_src = len(xs_list)
    total_rows = sum(x.shape[0] * x.shape[2] for x in xs_list)
    kern = functools.partial(_ffn_kernel, n_src=n_src, n_fc=n_fc)
    up_map = lambda e, j: (layer, e, 0, jnp.minimum(j, n_fc - 1))
    down_map = lambda e, j: (layer, e, 0, jnp.maximum(j - n_fc, 0))
    out_map = lambda e, j: (0, e, 0, jnp.maximum(j - n_fc, 0))
    outs = pl.pallas_call(
        kern,
        grid=(ne, n_fc + n_oc),
        in_specs=[pl.BlockSpec((x.shape[0], 1, x.shape[2], d), lambda e, j: (0, e, 0, 0), pipeline_mode=pl.Buffered(1))
                  for x in xs_list]
        + [pl.BlockSpec((1, 1, d, fchunk), up_map),
           pl.BlockSpec((1, 1, d, fchunk), up_map),
           pl.BlockSpec((1, 1, f, ochunk), down_map)],
        out_specs=[pl.BlockSpec((x.shape[0], 1, x.shape[2], ochunk), out_map) for x in xs_list],
        out_shape=[jax.ShapeDtypeStruct(x.shape, BF) for x in xs_list],
        scratch_shapes=[pltpu.VMEM((total_rows, d), BF), pltpu.VMEM((n_fc, total_rows, fchunk), BF)],
        compiler_params=_cparams(("arbitrary", "arbitrary")),
        name="expert_ffn",
    )(*xs_list, w_gate, w_up, w_down)
    return list(outs)


def _combine_kernel(x1_ref, m5_ref, slot_ref, aff_ref, ye_ref, o_ref, *, cap):
    st = slot_ref[0]
    at = aff_ref[0]
    tn, ne = st.shape
    lane = lax.broadcasted_iota(I32, (tn, cap), 1)
    acc = jnp.zeros(x1_ref.shape[1:], F32)
    for e in range(ne):
        onehot = jnp.where(st[:, e:e + 1] == lane, 1.0, 0.0).astype(BF)
        acc = acc + jnp.dot(onehot, ye_ref[0, e], preferred_element_type=F32) * at[:, e:e + 1]
    o_ref[0] = x1_ref[0] + m5_ref[0] * acc


def _combine(x1, m5, slot_tok, aff_tok, ye, tn):
    b, n, d = x1.shape
    ne, cap = ye.shape[1], ye.shape[2]
    nb = m5.shape[0]
    kern = functools.partial(_combine_kernel, cap=cap)
    return pl.pallas_call(
        kern,
        grid=(b, n // tn),
        in_specs=[pl.BlockSpec((1, tn, d), lambda i, j: (i, j, 0)),
                  pl.BlockSpec((1, 1, d), lambda i, j: (i if nb > 1 else 0, 0, 0)),
                  pl.BlockSpec((1, tn, ne), lambda i, j: (i, j, 0)),
                  pl.BlockSpec((1, tn, ne), lambda i, j: (i, j, 0)),
                  pl.BlockSpec((1, ne, cap, d), lambda i, j: (i, 0, 0, 0))],
        out_specs=pl.BlockSpec((1, tn, d), lambda i, j: (i, j, 0)),
        out_shape=jax.ShapeDtypeStruct((b, n, d), F32),
        compiler_params=_cparams(("arbitrary", "arbitrary")),
        name="moe_combine",
    )(x1, m5, slot_tok, aff_tok, ye)


def _rope_tables(n, use_rope):
    if not use_rope:
        return jnp.ones((n, LANES), F32), jnp.zeros((n, LANES), F32)
    rows = n // GRID_W
    row = jnp.repeat(jnp.arange(rows, dtype=F32), GRID_W)
    col = jnp.tile(jnp.arange(GRID_W, dtype=F32), rows)
    inv = ROPE_BASE ** (-jnp.arange(ROPE_FREQS_PER_AXIS, dtype=F32) / ROPE_FREQS_PER_AXIS)
    ang = jnp.concatenate([row[:, None] * inv, col[:, None] * inv], axis=-1)
    cos, sin = jnp.cos(ang), jnp.sin(ang)
    cos_t = jnp.tile(cos, (1, LANES // (HEAD_DIM // 2)))
    sin_t = jnp.tile(jnp.concatenate([-sin, sin], axis=-1), (1, LANES // HEAD_DIM))
    return cos_t, sin_t


def _dft_tables(n):
    n0 = 64
    n1 = n // n0
    k = np.arange(n, dtype=np.int64)[:, None]
    ang1 = jnp.asarray(2.0 * np.pi * ((k * np.arange(n1)[None, :]) % n1) / n1, F32)
    ang0 = jnp.asarray(2.0 * np.pi * ((k * np.arange(n0)[None, :]) % n) / n, F32)
    c1, s1, c0, s0 = jnp.cos(ang1), jnp.sin(ang1), jnp.cos(ang0), jnp.sin(ang0)
    scale = 1.0 / math.sqrt(n)
    cos_n = (c1[:, :, None] * c0[:, None, :] - s1[:, :, None] * s0[:, None, :]).reshape(n, n) * scale
    sin_n = (s1[:, :, None] * c0[:, None, :] + c1[:, :, None] * s0[:, None, :]).reshape(n, n) * scale
    return jnp.concatenate([cos_n, -sin_n], axis=1).astype(BF)


def _group_dft_tables(width):
    g = FNET_GROUP_DIM
    idx = np.arange(width)
    same = (idx[:, None] // g) == (idx[None, :] // g)
    ang = 2.0 * np.pi * (((idx[:, None] % g) * (idx[None, :] % g)) % g) / g
    scale = 1.0 / math.sqrt(g)
    bdc = np.where(same, np.cos(ang), 0.0) * scale
    bds = np.where(same, np.sin(ang), 0.0) * scale
    return jnp.asarray(bdc, F32).astype(BF), jnp.asarray(bds, F32).astype(BF)


def _group_mean_matrix(width):
    idx = np.arange(width)
    same = (idx[:, None] // HEAD_DIM) == (idx[None, :] // HEAD_DIM)
    return jnp.asarray(np.where(same, 1.0 / HEAD_DIM, 0.0), BF)


def _mixer_inputs(x3, mods, g_attn_l, w_in_bf, tables, qg, kg, gm):
    b, n, d = x3.shape
    tm = min(512, n)
    outs = _in_projection(x3.reshape(b * n, d), mods[0], mods[1], g_attn_l, w_in_bf, tables[0], tables[1],
                          qg, kg, gm, n, tm)
    return [o.reshape(b, n, o.shape[1]) for o in outs]


def _moe(sets, w_gate, w_up, w_down, layer):
    slots, xss = [], []
    for x1, h2, aff_t, _ in sets:
        cap = EC_FACTOR * x1.shape[1] // N_EXPERTS
        slot = _route(aff_t, cap)
        slots.append(slot)
        xss.append(_gather(slot, h2, cap))
    yes = _expert_ffn(xss, w_gate, w_up, w_down, layer, 512, 512)
    return [_combine(x1, m5, jnp.swapaxes(slot, 1, 2), jnp.swapaxes(aff_t, 1, 2), ye, min(512, x1.shape[1]))
            for (x1, _, aff_t, m5), slot, ye in zip(sets, slots, yes)]


def kernel(x, c, ctx, c_ctx, w_mod, b_mod, g_attn, g_ffn, w_in, ret_decay, diff_qn, diff_kn, diff_lambda,
           w_fnet_o, w_ret_o, w_diff_o, w_out, w_router, w_exp_gate, w_exp_up, w_exp_down):
    bsz, n, d = x.shape
    n_ctx = ctx.shape[1]
    depth = w_mod.shape[0]

    pad = (-(bsz + 1)) % 8
    cvecs = jnp.concatenate([c, c_ctx[None, :], jnp.zeros((pad, d), F32)], axis=0)
    mods = _modulation(cvecs, w_mod, b_mod)

    rope_lat = _rope_tables(n, True)
    rope_ctx = _rope_tables(n_ctx, False)
    cs_lat, cs_ctx = _dft_tables(n), _dft_tables(n_ctx)
    bdc, bds = _group_dft_tables(_W_FO)
    gm = _group_mean_matrix(_W_DQ)
    zero_state = jnp.zeros((bsz, N_HEADS, HEAD_DIM, HEAD_V), F32)
    ret_chunk = 1024

    xc = ctx
    for layer in range(depth):
        last = layer == depth - 1
        lam_init = 0.8 - 0.6 * math.exp(-0.3 * layer)
        mx = [mods[layer, :bsz, j * d:(j + 1) * d].reshape(bsz, 1, d) for j in range(N_MOD)]
        mc = [mods[layer, bsz:bsz + 1, j * d:(j + 1) * d].reshape(1, 1, d) for j in range(N_MOD)]
        w_in_bf = w_in[layer].astype(BF)
        qg = jnp.tile(diff_qn[layer], _W_DQ // HEAD_DIM)[None, :]
        kg = jnp.tile(diff_kn[layer], _W_DK // HEAD_DIM)[None, :]
        g_a = g_attn[layer][None, :]
        g_f = g_ffn[layer][None, :]
        decays = ret_decay[layer]
        dlam = diff_lambda[layer]
        branch_w = (w_fnet_o[layer].astype(BF), w_ret_o[layer].astype(BF), w_diff_o[layer].astype(BF),
                    w_out[layer].astype(BF))
        rt = w_router[layer].T
        rth = rt.astype(BF)
        rtl = (rt - rth.astype(F32)).astype(BF)
        moe_sets = []

        fo_c, rq_c, rk_c, rv_c, sg_c, dq_c, dk_c, dv_c, gt_c = _mixer_inputs(xc, mc, g_a, w_in_bf, rope_ctx, qg, kg, gm)
        ro_c, s_f, s_b = _retention(rq_c, rk_c, rv_c, sg_c, decays, zero_state, zero_state, min(ret_chunk, n_ctx))
        if not last:
            f_c = _fourier_mix(fo_c, cs_ctx, bdc, bds, min(512, n_ctx))
            do_c = _diff_attention(dq_c, [dk_c], [dv_c], dlam, lam_init, min(256, n_ctx))
            rows_c = bsz * n_ctx
            x1_c, h2_c, aff_c = _merge(f_c.reshape(rows_c, -1), ro_c.reshape(rows_c, -1), do_c.reshape(rows_c, -1),
                                       gt_c.reshape(rows_c, -1), xc.reshape(rows_c, d), mc[2], mc[3], mc[4], g_f,
                                       *branch_w, rth, rtl, n_ctx, min(512, n_ctx))
            moe_sets.append((x1_c.reshape(bsz, n_ctx, d), h2_c.reshape(bsz, n_ctx, d), aff_c, mc[5]))

        fo, rq, rk, rv, sg, dq, dk, dv, gt = _mixer_inputs(x, mx, g_a, w_in_bf, rope_lat, qg, kg, gm)
        ro, _, _ = _retention(rq, rk, rv, sg, decays, s_f, s_b, min(ret_chunk, n))
        f = _fourier_mix(fo, cs_lat, bdc, bds, min(512, n))
        do = _diff_attention(dq, [dk, dk_c], [dv, dv_c], dlam, lam_init, 256)
        rows = bsz * n
        x1, h2, aff = _merge(f.reshape(rows, -1), ro.reshape(rows, -1), do.reshape(rows, -1), gt.reshape(rows, -1),
                             x.reshape(rows, d), mx[2], mx[3], mx[4], g_f, *branch_w, rth, rtl, n, min(512, n))
        moe_sets.append((x1.reshape(bsz, n, d), h2.reshape(bsz, n, d), aff, mx[5]))
        moe_out = _moe(moe_sets, w_exp_gate, w_exp_up, w_exp_down, layer)
        x = moe_out[-1]
        if not last:
            xc = moe_out[0]
    return x
```

```python
import functools
import math

import jax
import jax.numpy as jnp
import numpy as np
from jax import lax
from jax.experimental import pallas as pl
from jax.experimental.pallas import tpu as pltpu

F32 = jnp.float32
BF = jnp.bfloat16
I32 = jnp.int32

GRID_W = 64
HEAD_DIM = 64
ROPE_FREQS_PER_AXIS = HEAD_DIM // 4
ROPE_BASE = 10000.0
EPS = 1e-6
LOG2E = 1.4426950408889634
FNET_GROUP_DIM = 64
N_HEADS = 4
HEAD_V = 128
N_EXPERTS = 16
EC_FACTOR = 2
N_MOD = 6
RET_SUB = 256
MERGE_SUB = 1024
LANES = 128
VMEM_LIMIT = 56 * 1024 * 1024

_W_FO, _W_RQ, _W_RK, _W_RV, _W_RG, _W_DQ, _W_DK, _W_DV = 512, 256, 256, 512, 512, 512, 512, 512


def _cparams(sem):
    return pltpu.CompilerParams(dimension_semantics=sem, vmem_limit_bytes=VMEM_LIMIT)


def _sigmoid(v):
    return 1.0 / (1.0 + jnp.exp(-v))


def _nt_dot(a, b):
    return lax.dot_general(a, b, (((1,), (1,)), ((), ())), preferred_element_type=F32)


def _tn_dot(a, b):
    return lax.dot_general(a, b, (((0,), (0,)), ((), ())), preferred_element_type=F32)


def _mod_kernel(c_ref, w_ref, b_ref, o_ref):
    cv = c_ref[...]
    s = cv * _sigmoid(cv)
    o_ref[0] = jnp.dot(s.astype(BF), w_ref[0].astype(BF), preferred_element_type=F32) + b_ref[0]


def _modulation(cvecs, w_mod, b_mod):
    depth, d, wd = w_mod.shape
    rows = cvecs.shape[0]
    tn = 1536
    return pl.pallas_call(
        _mod_kernel,
        grid=(depth, wd // tn),
        in_specs=[pl.BlockSpec((rows, d), lambda l, j: (0, 0)),
                  pl.BlockSpec((1, d, tn), lambda l, j: (l, 0, j)),
                  pl.BlockSpec((1, 1, tn), lambda l, j: (l, 0, j))],
        out_specs=pl.BlockSpec((1, rows, tn), lambda l, j: (l, 0, j)),
        out_shape=jax.ShapeDtypeStruct((depth, rows, wd), F32),
        compiler_params=_cparams(("arbitrary", "arbitrary")),
        name="modulation",
    )(cvecs, w_mod, b_mod.reshape(depth, 1, wd))


def _rope(x, cos_t, sin_t):
    lane = lax.broadcasted_iota(I32, (x.shape[0], LANES), 1)
    first = (lane & 63) < 32
    outs = []
    for j in range(x.shape[1] // LANES):
        xc = x[:, j * LANES:(j + 1) * LANES]
        sw = jnp.where(first, pltpu.roll(xc, LANES - 32, 1), pltpu.roll(xc, 32, 1))
        outs.append(xc * cos_t + sw * sin_t)
    return jnp.concatenate(outs, axis=1)


def _group_rms(x, gm, gain):
    ms = jnp.dot((x * x).astype(BF), gm, preferred_element_type=F32)
    return x * lax.rsqrt(ms + EPS) * gain


def _inproj_kernel(x_ref, shift_ref, scale_ref, g_ref, w_ref, cos_ref, sin_ref, qg_ref, kg_ref, gm_ref,
                   fo_ref, rq_ref, rk_ref, rv_ref, sg_ref, dq_ref, dk_ref, dv_ref, gt_ref):
    x = x_ref[...]
    d = x.shape[1]
    ms = jnp.mean(x * x, axis=-1, keepdims=True)
    h = x * lax.rsqrt(ms + EPS) * g_ref[...]
    h = h * (1.0 + scale_ref[0]) + shift_ref[0]
    hb = h.astype(BF)
    cos_t = cos_ref[...]
    sin_t = sin_ref[...]
    gm = gm_ref[...]

    def proj(a, width):
        return jnp.dot(hb, w_ref[:, a:a + width], preferred_element_type=F32)

    a = 0
    fo_ref[...] = proj(a, _W_FO).astype(BF)
    a += _W_FO
    rq_ref[...] = _rope(proj(a, _W_RQ) * (HEAD_DIM ** -0.5), cos_t, sin_t).astype(BF)
    a += _W_RQ
    rk_ref[...] = _rope(proj(a, _W_RK), cos_t, sin_t).astype(BF)
    a += _W_RK
    rv_ref[...] = proj(a, _W_RV).astype(BF)
    a += _W_RV
    rg = proj(a, _W_RG)
    sg_ref[...] = (rg * _sigmoid(rg)).astype(BF)
    a += _W_RG
    dq = _group_rms(proj(a, _W_DQ), gm, qg_ref[...])
    dq_ref[...] = (_rope(dq, cos_t, sin_t) * (HEAD_DIM ** -0.5 * LOG2E)).astype(BF)
    a += _W_DQ
    dk = _group_rms(proj(a, _W_DK), gm, kg_ref[...])
    dk_ref[...] = _rope(dk, cos_t, sin_t).astype(BF)
    a += _W_DK
    dv_ref[...] = proj(a, _W_DV).astype(BF)
    a += _W_DV
    for j in range(3):
        gl = proj(a + j * d, d)
        gt_ref[:, j * d:(j + 1) * d] = _sigmoid(gl).astype(BF)


def _in_projection(x2d, shift, scale, g, w_in_bf, cos_t, sin_t, qg, kg, gm, n_seq, tm):
    rows, d = x2d.shape
    tiles_per_b = n_seq // tm
    nb = shift.shape[0]
    win = w_in_bf.shape[1]
    widths = (_W_FO, _W_RQ, _W_RK, _W_RV, _W_RG, _W_DQ, _W_DK, _W_DV, 3 * d)

    def mod_map(i):
        return ((i // tiles_per_b) if nb > 1 else 0, 0, 0)

    def pos_map(i):
        return (i % tiles_per_b, 0)

    const2 = lambda i: (0, 0)
    return pl.pallas_call(
        _inproj_kernel,
        grid=(rows // tm,),
        in_specs=[pl.BlockSpec((tm, d), lambda i: (i, 0)),
                  pl.BlockSpec((1, 1, d), mod_map),
                  pl.BlockSpec((1, 1, d), mod_map),
                  pl.BlockSpec((1, d), const2),
                  pl.BlockSpec((d, win), const2, pipeline_mode=pl.Buffered(1)),
                  pl.BlockSpec((tm, LANES), pos_map),
                  pl.BlockSpec((tm, LANES), pos_map),
                  pl.BlockSpec((1, _W_DQ), const2),
                  pl.BlockSpec((1, _W_DK), const2),
                  pl.BlockSpec((_W_DQ, _W_DQ), const2)],
        out_specs=[pl.BlockSpec((tm, w), lambda i: (i, 0)) for w in widths],
        out_shape=[jax.ShapeDtypeStruct((rows, w), BF) for w in widths],
        compiler_params=_cparams(("arbitrary",)),
        name="in_projection",
    )(x2d, shift, scale, g, w_in_bf, cos_t, sin_t, qg, kg, gm)


def _fourier_kernel(x_ref, cs_ref, bc_ref, bs_ref, jsh_ref, o_ref, z_ref, eo_ref):
    n, w = x_ref.shape[1], x_ref.shape[2]
    half = n // 2
    gw = bc_ref.shape[0]
    fb = jsh_ref.shape[0]
    tr = o_ref.shape[1]

    @pl.when(pl.program_id(1) == 0)
    def _():
        x = x_ref[0]
        for g in range(w // gw):
            xs = x[:, g * gw:(g + 1) * gw]
            z_ref[0, :, g * gw:(g + 1) * gw] = jnp.dot(xs, bc_ref[...], preferred_element_type=F32).astype(BF)
            z_ref[1, :, g * gw:(g + 1) * gw] = jnp.dot(xs, bs_ref[...], preferred_element_type=F32).astype(BF)
        jsh = jsh_ref[...]
        n_blocks = half // fb
        for blk in range(n_blocks):
            hi_blk = 2 * n_blocks - 1 - blk
            for t in range(2):
                top = z_ref[t, hi_blk * fb:(hi_blk + 1) * fb, :]
                nxt = jnp.zeros_like(top) if blk == 0 else z_ref[t, (hi_blk + 1) * fb:(hi_blk + 2) * fb, :]
                partner = jnp.dot(jsh, jnp.concatenate([top, nxt], axis=0), preferred_element_type=F32)
                own = z_ref[t, blk * fb:(blk + 1) * fb, :].astype(F32)
                folded = own + partner if t == 0 else own - partner
                eo_ref[t * half + blk * fb:t * half + (blk + 1) * fb, :] = folded.astype(BF)

    y = jnp.dot(cs_ref[...], eo_ref[...], preferred_element_type=F32)
    parity = lax.broadcasted_iota(I32, (tr, 1), 0) & 1
    sign = (1.0 - 2.0 * parity.astype(F32)) * (1.0 / math.sqrt(n))
    o_ref[0] = (y + sign * z_ref[0, half:half + 1, :].astype(F32)).astype(BF)


def _fourier_mix(fo, cs, bdc, bds, jsh, tr):
    b, n, w = fo.shape
    assert n % (2 * jsh.shape[0]) == 0 and tr % 2 == 0
    return pl.pallas_call(
        _fourier_kernel,
        grid=(b, n // tr),
        in_specs=[pl.BlockSpec((1, n, w), lambda i, j: (i, 0, 0)),
                  pl.BlockSpec((tr, n), lambda i, j: (j, 0)),
                  pl.BlockSpec(bdc.shape, lambda i, j: (0, 0)),
                  pl.BlockSpec(bds.shape, lambda i, j: (0, 0)),
                  pl.BlockSpec(jsh.shape, lambda i, j: (0, 0))],
        out_specs=pl.BlockSpec((1, tr, w), lambda i, j: (i, j, 0)),
        out_shape=jax.ShapeDtypeStruct((b, n, w), BF),
        scratch_shapes=[pltpu.VMEM((2, n, w), BF), pltpu.VMEM((n, w), BF)],
        compiler_params=_cparams(("arbitrary", "arbitrary")),
        name="fourier_mix",
    )(fo, cs, bdc, bds, jsh)


def _ret_kernel(lg_ref, q_ref, k_ref, v_ref, sg_ref, sf_ref, sb_ref, o_ref, sfo_ref, sbo_ref,
                acc_ref, tmp_ref, st_ref, dm_ref, qd_ref, qm_ref, kd_ref, cd_ref, *, chunk, n_chunks, sub):
    b = pl.program_id(0)
    p = pl.program_id(1)
    c = pl.program_id(2)
    sf = float(sub)
    n_sub = chunk // sub

    @pl.when((b == 0) & (p == 0) & (c == 0))
    def _tables():
        ii = lax.broadcasted_iota(I32, (sub, sub), 0).astype(F32)
        jj = lax.broadcasted_iota(I32, (sub, sub), 1).astype(F32)
        pos = lax.broadcasted_iota(I32, (sub, LANES), 0).astype(F32)
        lane = lax.broadcasted_iota(I32, (sub, LANES), 1)
        for dr in range(2):
            if dr == 0:
                rel, keep = ii - jj, ii >= jj
                qe, ke = pos + 1.0, (sf - 1.0) - pos
            else:
                rel, keep = jj - ii, jj > ii
                qe, ke = sf - pos, pos
            for hp in range(N_HEADS // 2):
                lg0 = jnp.full((sub, LANES), lg_ref[dr, 2 * hp], F32)
                lg1 = jnp.full((sub, LANES), lg_ref[dr, 2 * hp + 1], F32)
                lgp = jnp.where(lane < HEAD_DIM, lg0, lg1)
                kd_ref[dr, hp] = jnp.exp(lgp * ke)
            for h in range(N_HEADS):
                lgs = lg_ref[dr, h]
                in_head = (lane >= (h % 2) * HEAD_DIM) & (lane < (h % 2 + 1) * HEAD_DIM)
                dm_ref[dr, h] = jnp.where(keep, jnp.exp(jnp.full((sub, sub), lgs, F32) * jnp.maximum(rel, 0.0)), 0.0)
                qd_ref[dr, h] = jnp.where(in_head, jnp.exp(jnp.full((sub, LANES), lgs, F32) * qe), 0.0)
                cd_ref[dr, h] = jnp.exp(jnp.full((LANES, LANES), lgs, F32) * sf)
        for h in range(N_HEADS):
            in_head = (lane >= (h % 2) * HEAD_DIM) & (lane < (h % 2 + 1) * HEAD_DIM)
            qm_ref[h] = jnp.where(in_head, 1.0, 0.0)

    def load_state(src_ref):
        z = jnp.zeros((HEAD_DIM, HEAD_V), F32)
        for h in range(N_HEADS):
            s = src_ref[0, h]
            st_ref[h] = jnp.concatenate([s, z], axis=0) if h % 2 == 0 else jnp.concatenate([z, s], axis=0)

    @pl.when((c == 0) & (p == 0))
    def _():
        load_state(sf_ref)

    @pl.when((c == 0) & (p == 1))
    def _():
        load_state(sb_ref)

    sts = [st_ref[h] for h in range(N_HEADS)]
    for u in range(n_sub):
        r0 = pl.multiple_of(jnp.where(p == 0, u, n_sub - 1 - u) * sub, sub)
        rows = pl.ds(r0, sub)
        q = q_ref[0, rows, :]
        k = k_ref[0, rows, :]
        v = v_ref[0, rows, :]
        outs = []
        for hp in range(N_HEADS // 2):
            qp = q[:, hp * LANES:(hp + 1) * LANES].astype(F32)
            kp = k[:, hp * LANES:(hp + 1) * LANES]
            kdp = (kp.astype(F32) * kd_ref[p, hp]).astype(BF)
            for h in (2 * hp, 2 * hp + 1):
                vh = v[:, h * HEAD_V:(h + 1) * HEAD_V]
                qm = (qp * qm_ref[h]).astype(BF)
                qdq = (qp * qd_ref[p, h]).astype(BF)
                s = _nt_dot(qm, kp) * dm_ref[p, h]
                o = (jnp.dot(s.astype(BF), vh, preferred_element_type=F32)
                     + jnp.dot(qdq, sts[h].astype(BF), preferred_element_type=F32))
                sts[h] = sts[h] * cd_ref[p, h] + _tn_dot(kdp, vh)
                outs.append(o)
        tmp_ref[rows, :] = jnp.concatenate(outs, axis=1)
    for h in range(N_HEADS):
        st_ref[h] = sts[h]
    idx = jnp.where(p == 0, c, n_chunks - 1 - c)

    @pl.when(p == 0)
    def _():
        acc_ref[idx] = tmp_ref[...]

    @pl.when(p == 1)
    def _():
        tot = acc_ref[idx] + tmp_ref[...]
        sg = sg_ref[0].astype(F32)
        ys = []
        for h in range(N_HEADS):
            oh = tot[:, h * HEAD_V:(h + 1) * HEAD_V]
            mu = jnp.mean(oh, axis=-1, keepdims=True)
            dlt = oh - mu
            var = jnp.mean(dlt * dlt, axis=-1, keepdims=True)
            ys.append(dlt * lax.rsqrt(var + EPS))
        o_ref[0] = (jnp.concatenate(ys, axis=1) * sg).astype(BF)

    def store_state(dst_ref):
        for h in range(N_HEADS):
            r0 = (h % 2) * HEAD_DIM
            dst_ref[0, h] = st_ref[h, r0:r0 + HEAD_DIM, :]

    @pl.when((c == n_chunks - 1) & (p == 0))
    def _():
        store_state(sfo_ref)

    @pl.when((c == n_chunks - 1) & (p == 1))
    def _():
        store_state(sbo_ref)


def _retention(rq, rk, rv, sg, decays, s_f, s_b, chunk):
    b, n, _ = rq.shape
    n_chunks = n // chunk
    wv = rv.shape[2]

    def seq_map(i, p, c):
        return (i, jnp.where(p == 0, c, n_chunks - 1 - c), 0)

    def out_map(i, p, c):
        return (i, jnp.where(p == 0, n_chunks - 1, n_chunks - 1 - c), 0)

    st_spec = pl.BlockSpec((1, N_HEADS, HEAD_DIM, HEAD_V), lambda i, p, c: (i, 0, 0, 0))
    st_shape = jax.ShapeDtypeStruct((b, N_HEADS, HEAD_DIM, HEAD_V), F32)
    sub = min(RET_SUB, chunk)
    kern = functools.partial(_ret_kernel, chunk=chunk, n_chunks=n_chunks, sub=sub)
    return pl.pallas_call(
        kern,
        grid=(b, 2, n_chunks),
        in_specs=[pl.BlockSpec(memory_space=pltpu.SMEM),
                  pl.BlockSpec((1, chunk, rq.shape[2]), seq_map),
                  pl.BlockSpec((1, chunk, rk.shape[2]), seq_map),
                  pl.BlockSpec((1, chunk, wv), seq_map),
                  pl.BlockSpec((1, chunk, wv), seq_map),
                  st_spec, st_spec],
        out_specs=[pl.BlockSpec((1, chunk, wv), out_map), st_spec, st_spec],
        out_shape=[jax.ShapeDtypeStruct((b, n, wv), BF), st_shape, st_shape],
        scratch_shapes=[pltpu.VMEM((n_chunks, chunk, wv), F32),
                        pltpu.VMEM((chunk, wv), F32),
                        pltpu.VMEM((N_HEADS, LANES, HEAD_V), F32),
                        pltpu.VMEM((2, N_HEADS, sub, sub), F32),
                        pltpu.VMEM((2, N_HEADS, sub, LANES), F32),
                        pltpu.VMEM((N_HEADS, sub, LANES), F32),
                        pltpu.VMEM((2, N_HEADS // 2, sub, LANES), F32),
                        pltpu.VMEM((2, N_HEADS, LANES, HEAD_V), F32)],
        compiler_params=_cparams(("arbitrary", "arbitrary", "arbitrary")),
        name="retention",
    )(decays, rq, rk, rv, sg, s_f, s_b)


def _dattn_kernel(dl_ref, q_ref, *refs, n_src, lam_init):
    k_refs = refs[:n_src]
    v_refs = refs[n_src:2 * n_src]
    o_ref = refs[2 * n_src]
    dl = dl_ref[...]
    lam = (jnp.exp(jnp.sum(dl[0:1] * dl[1:2], axis=-1, keepdims=True))
           - jnp.exp(jnp.sum(dl[2:3] * dl[3:4], axis=-1, keepdims=True)) + lam_init)
    q = q_ref[0].astype(F32)
    tq = q.shape[0]
    lane = lax.broadcasted_iota(I32, (tq, LANES), 1)
    ys = []
    for h in range(N_HEADS):
        qh = q[:, h * LANES:(h + 1) * LANES]
        qs = jnp.concatenate([jnp.where(lane < HEAD_DIM, qh, 0.0), jnp.where(lane >= HEAD_DIM, qh, 0.0)],
                             axis=0).astype(BF)
        ss = [_nt_dot(qs, kr[0, :, h * LANES:(h + 1) * LANES]) for kr in k_refs]
        m = ss[0].max(axis=-1, keepdims=True)
        for s in ss[1:]:
            m = jnp.maximum(m, s.max(axis=-1, keepdims=True))
        acc = jnp.zeros((2 * tq, 2 * HEAD_V), F32)
        for s, vr in zip(ss, v_refs):
            pexp = jnp.exp2(s - m).astype(BF)
            vh = vr[0, :, h * HEAD_V:(h + 1) * HEAD_V]
            acc = acc + jnp.dot(pexp, jnp.concatenate([vh, jnp.ones_like(vh)], axis=1), preferred_element_type=F32)
        o = acc[:, :HEAD_V] / acc[:, HEAD_V:]
        oh = o[:tq] - lam * o[tq:]
        ms = jnp.mean(oh * oh, axis=-1, keepdims=True)
        ys.append(oh * lax.rsqrt(ms + EPS) * (1.0 - lam_init))
    o_ref[0] = jnp.concatenate(ys, axis=1).astype(BF)


def _diff_attention(dq, ks, vs, dlam, lam_init, tq):
    b, n, w = dq.shape
    n_src = len(ks)
    kern = functools.partial(_dattn_kernel, n_src=n_src, lam_init=lam_init)
    kv_specs = [pl.BlockSpec((1, a.shape[1], w), lambda i, j: (i, 0, 0)) for a in (*ks, *vs)]
    return pl.pallas_call(
        kern,
        grid=(b, n // tq),
        in_specs=[pl.BlockSpec(dlam.shape, lambda i, j: (0, 0)),
                  pl.BlockSpec((1, tq, w), lambda i, j: (i, j, 0))] + kv_specs,
        out_specs=pl.BlockSpec((1, tq, w), lambda i, j: (i, j, 0)),
        out_shape=jax.ShapeDtypeStruct((b, n, w), BF),
        compiler_params=_cparams(("arbitrary", "arbitrary")),
        name="diff_attention",
    )(dlam, dq, *ks, *vs)


def _merge_kernel(f_ref, ro_ref, do_ref, gt_ref, x_ref, m2_ref, m3_ref, m4_ref, g_ref,
                  wf_ref, wr_ref, wd_ref, wo_ref, rth_ref, rtl_ref, x1_ref, h2_ref, aff_ref):
    tm, d = x_ref.shape
    st = min(MERGE_SUB, tm)
    for r0 in range(0, tm, st):
        rows = slice(r0, r0 + st)
        t = (gt_ref[rows, 0:d].astype(F32) * jnp.dot(f_ref[rows, :], wf_ref[...], preferred_element_type=F32)
             + gt_ref[rows, d:2 * d].astype(F32) * jnp.dot(ro_ref[rows, :], wr_ref[...], preferred_element_type=F32)
             + gt_ref[rows, 2 * d:3 * d].astype(F32) * jnp.dot(do_ref[rows, :], wd_ref[...],
                                                                preferred_element_type=F32))
        mix = jnp.dot(t.astype(BF), wo_ref[...], preferred_element_type=F32)
        x1 = x_ref[rows, :] + m2_ref[0] * mix
        x1_ref[rows, :] = x1
        ms = jnp.mean(x1 * x1, axis=-1, keepdims=True)
        h2 = x1 * lax.rsqrt(ms + EPS) * g_ref[...]
        h2 = h2 * (1.0 + m4_ref[0]) + m3_ref[0]
        hi = h2.astype(BF)
        lo = (h2 - hi.astype(F32)).astype(BF)
        h2_ref[rows, :] = hi
        lt = _nt_dot(rth_ref[...], hi) + _nt_dot(rth_ref[...], lo) + _nt_dot(rtl_ref[...], hi)
        mx = lt.max(axis=0, keepdims=True)
        ex = jnp.exp(lt - mx)
        aff_ref[0, :, rows] = ex / jnp.sum(ex, axis=0, keepdims=True)


def _merge(f, ro, do, gt, x2d, m2, m3, m4, g, wf, wr, wd, wo, rth, rtl, n_seq, tm):
    rows, d = x2d.shape
    tiles_per_b = n_seq // tm
    nb = m2.shape[0]
    bsz = rows // n_seq
    ne = rth.shape[0]

    def mod_map(i):
        return ((i // tiles_per_b) if nb > 1 else 0, 0, 0)

    const2 = lambda i: (0, 0)
    row_spec = lambda w: pl.BlockSpec((tm, w), lambda i: (i, 0))
    return pl.pallas_call(
        _merge_kernel,
        grid=(rows // tm,),
        in_specs=[row_spec(f.shape[1]), row_spec(ro.shape[1]), row_spec(do.shape[1]), row_spec(gt.shape[1]),
                  row_spec(d),
                  pl.BlockSpec((1, 1, d), mod_map), pl.BlockSpec((1, 1, d), mod_map), pl.BlockSpec((1, 1, d), mod_map),
                  pl.BlockSpec((1, d), const2),
                  pl.BlockSpec(wf.shape, const2), pl.BlockSpec(wr.shape, const2), pl.BlockSpec(wd.shape, const2),
                  pl.BlockSpec(wo.shape, const2), pl.BlockSpec(rth.shape, const2), pl.BlockSpec(rtl.shape, const2)],
        out_specs=[row_spec(d), row_spec(d),
                   pl.BlockSpec((1, ne, tm), lambda i: (i // tiles_per_b, 0, i % tiles_per_b))],
        out_shape=[jax.ShapeDtypeStruct((rows, d), F32), jax.ShapeDtypeStruct((rows, d), BF),
                   jax.ShapeDtypeStruct((bsz, ne, n_seq), F32)],
        compiler_params=_cparams(("arbitrary",)),
        name="merge_router",
    )(f, ro, do, gt, x2d, m2, m3, m4, g, wf, wr, wd, wo, rth, rtl)


def _route_kernel(a_ref, slot_ref, *, cap, blk):
    a = a_ref[...]
    ne, n = a.shape
    capf = float(cap)

    def enough(t):
        return jnp.sum(jnp.where(a >= t, 1.0, 0.0), axis=-1, keepdims=True) >= capf

    tiny = jnp.full((ne, 1), 2.0 ** -126, F32)
    found = enough(tiny)
    cur = tiny
    for step in (64, 32, 16, 8, 4, 2, 1):
        cand = cur * (2.0 ** step)
        cur = jnp.where(enough(cand), cand, cur)
    base = cur

    def mantissa_bit(_, carry):
        cur, stepv = carry
        cand = cur + stepv
        return jnp.where(enough(cand), cand, cur), stepv * 0.5

    cur, ulp = lax.fori_loop(0, 23, mantissa_bit, (cur, base * 0.5))
    lo = jnp.where(found, cur, 0.0)
    hi = jnp.where(found, cur + ulp * 2.0, tiny)

    def refine(_, carry):
        lo, hi = carry
        mid = lo + (hi - lo) * 0.5
        ok = enough(mid)
        return jnp.where(ok, mid, lo), jnp.where(ok, hi, mid)

    lo, hi = lax.fori_loop(0, 24, refine, (lo, hi))
    ri = lax.broadcasted_iota(I32, (blk, blk), 0)
    ci = lax.broadcasted_iota(I32, (blk, blk), 1)
    upper = jnp.where(ri < ci, 1.0, 0.0).astype(BF)

    def excl_cumsum(m):
        carry = jnp.zeros((ne, 1), F32)
        outs = []
        for j in range(n // blk):
            mb = m[:, j * blk:(j + 1) * blk]
            outs.append(jnp.dot(mb.astype(BF), upper, preferred_element_type=F32) + carry)
            carry = carry + jnp.sum(mb, axis=-1, keepdims=True)
        return jnp.concatenate(outs, axis=1)

    gt = a >= hi
    tie = (a >= lo) & (a < hi)
    need = capf - jnp.sum(jnp.where(gt, 1.0, 0.0), axis=-1, keepdims=True)
    sel = gt | (tie & (excl_cumsum(jnp.where(tie, 1.0, 0.0)) < need))
    slot = excl_cumsum(jnp.where(sel, 1.0, 0.0))
    slot_ref[...] = jnp.where(sel, slot.astype(I32), -1)


def _route(aff_t, cap):
    b, ne, n = aff_t.shape
    kern = functools.partial(_route_kernel, cap=cap, blk=min(256, n))
    slot = pl.pallas_call(
        kern,
        grid=(1,),
        in_specs=[pl.BlockSpec((b * ne, n), lambda i: (0, 0))],
        out_specs=pl.BlockSpec((b * ne, n), lambda i: (0, 0)),
        out_shape=jax.ShapeDtypeStruct((b * ne, n), I32),
        compiler_params=_cparams(("arbitrary",)),
        name="route",
    )(aff_t.reshape(b * ne, n))
    return slot.reshape(b, ne, n)


def _gather_kernel(slot_ref, h_ref, xs_ref, *, cap):
    n = slot_ref.shape[3]
    sub = lax.broadcasted_iota(I32, (cap, n), 0)
    hb = h_ref[0]
    for g in range(slot_ref.shape[1]):
        onehot = jnp.where(sub == slot_ref[0, g], 1.0, 0.0).astype(BF)
        xs_ref[0, g] = jnp.dot(onehot, hb, preferred_element_type=F32).astype(BF)


def _gather(slot, h2, cap):
    b, ne, n = slot.shape
    d = h2.shape[2]
    eg = max(1, min(ne, 1024 // cap))
    kern = functools.partial(_gather_kernel, cap=cap)
    return pl.pallas_call(
        kern,
        grid=(b, ne // eg),
        in_specs=[pl.BlockSpec((1, eg, 1, n), lambda i, e: (i, e, 0, 0)),
                  pl.BlockSpec((1, n, d), lambda i, e: (i, 0, 0))],
        out_specs=pl.BlockSpec((1, eg, cap, d), lambda i, e: (i, e, 0, 0)),
        out_shape=jax.ShapeDtypeStruct((b, ne, cap, d), BF),
        compiler_params=_cparams(("arbitrary", "arbitrary")),
        name="moe_gather",
    )(slot.reshape(b, ne, 1, n), h2)


def _ffn_kernel(*refs, n_src, n_fc):
    xs_refs = refs[:n_src]
    wg_ref, wu_ref, wd_ref = refs[n_src:n_src + 3]
    ye_refs = refs[n_src + 3:2 * n_src + 3]
    x_ref, hid_ref = refs[2 * n_src + 3:]
    j = pl.program_id(1)
    fc = wg_ref.shape[3]
    row_spans = []
    r0 = 0
    for xr in xs_refs:
        rows = xr.shape[0] * xr.shape[2]
        row_spans.append((r0, rows))
        r0 += rows

    @pl.when(j == 0)
    def _():
        for xr, (s0, rows) in zip(xs_refs, row_spans):
            x_ref[s0:s0 + rows, :] = xr[:, 0].reshape(rows, xr.shape[3])

    @pl.when(j < n_fc)
    def _():
        x = x_ref[...]
        g = jnp.dot(x, wg_ref[0, 0].astype(BF), preferred_element_type=F32)
        u = jnp.dot(x, wu_ref[0, 0].astype(BF), preferred_element_type=F32)
        hid_ref[j] = (g * _sigmoid(g) * u).astype(BF)

    @pl.when(j >= n_fc)
    def _():
        y = jnp.dot(hid_ref[0], wd_ref[0, 0, 0:fc, :].astype(BF), preferred_element_type=F32)
        for k in range(1, n_fc):
            y = y + jnp.dot(hid_ref[k], wd_ref[0, 0, k * fc:(k + 1) * fc, :].astype(BF), preferred_element_type=F32)
        for yr, (s0, rows) in zip(ye_refs, row_spans):
            yr[:, 0] = y[s0:s0 + rows].reshape(yr.shape[0], yr.shape[2], yr.shape[3]).astype(BF)


def _expert_ffn(xs_list, w_gate, w_up, w_down, layer, fchunk, ochunk):
    ne, d = xs_list[0].shape[1], xs_list[0].shape[3]
    f = w_gate.shape[3]
    n_fc, n_oc = f // fchunk, d // ochunk
    n_src = len(xs_list)
    total_rows = sum(x.shape[0] * x.shape[2] for x in xs_list)
    kern = functools.partial(_ffn_kernel, n_src=n_src, n_fc=n_fc)
    up_map = lambda e, j: (layer, e, 0, jnp.minimum(j, n_fc - 1))
    down_map = lambda e, j: (layer, e, 0, jnp.maximum(j - n_fc, 0))
    out_map = lambda e, j: (0, e, 0, jnp.maximum(j - n_fc, 0))
    outs = pl.pallas_call(
        kern,
        grid=(ne, n_fc + n_oc),
        in_specs=[pl.BlockSpec((x.shape[0], 1, x.shape[2], d), lambda e, j: (0, e, 0, 0), pipeline_mode=pl.Buffered(1))
                  for x in xs_list]
        + [pl.BlockSpec((1, 1, d, fchunk), up_map),
           pl.BlockSpec((1, 1, d, fchunk), up_map),
           pl.BlockSpec((1, 1, f, ochunk), down_map)],
        out_specs=[pl.BlockSpec((x.shape[0], 1, x.shape[2], ochunk), out_map) for x in xs_list],
        out_shape=[jax.ShapeDtypeStruct(x.shape, BF) for x in xs_list],
        scratch_shapes=[pltpu.VMEM((total_rows, d), BF), pltpu.VMEM((n_fc, total_rows, fchunk), BF)],
        compiler_params=_cparams(("arbitrary", "arbitrary")),
        name="expert_ffn",
    )(*xs_list, w_gate, w_up, w_down)
    return list(outs)


def _combine_kernel(x1_ref, m5_ref, slot_ref, aff_ref, ye_ref, o_ref, *, cap):
    st = slot_ref[0]
    at = aff_ref[0]
    tn, ne = st.shape
    lane = lax.broadcasted_iota(I32, (tn, cap), 1)
    acc = jnp.zeros(x1_ref.shape[1:], F32)
    for e in range(ne):
        onehot = jnp.where(st[:, e:e + 1] == lane, 1.0, 0.0).astype(BF)
        acc = acc + jnp.dot(onehot, ye_ref[0, e], preferred_element_type=F32) * at[:, e:e + 1]
    o_ref[0] = x1_ref[0] + m5_ref[0] * acc


def _combine(x1, m5, slot_tok, aff_tok, ye, tn):
    b, n, d = x1.shape
    ne, cap = ye.shape[1], ye.shape[2]
    nb = m5.shape[0]
    kern = functools.partial(_combine_kernel, cap=cap)
    return pl.pallas_call(
        kern,
        grid=(b, n // tn),
        in_specs=[pl.BlockSpec((1, tn, d), lambda i, j: (i, j, 0)),
                  pl.BlockSpec((1, 1, d), lambda i, j: (i if nb > 1 else 0, 0, 0)),
                  pl.BlockSpec((1, tn, ne), lambda i, j: (i, j, 0)),
                  pl.BlockSpec((1, tn, ne), lambda i, j: (i, j, 0)),
                  pl.BlockSpec((1, ne, cap, d), lambda i, j: (i, 0, 0, 0))],
        out_specs=pl.BlockSpec((1, tn, d), lambda i, j: (i, j, 0)),
        out_shape=jax.ShapeDtypeStruct((b, n, d), F32),
        compiler_params=_cparams(("arbitrary", "arbitrary")),
        name="moe_combine",
    )(x1, m5, slot_tok, aff_tok, ye)


def _rope_tables(n, use_rope):
    if not use_rope:
        return jnp.ones((n, LANES), F32), jnp.zeros((n, LANES), F32)
    rows = n // GRID_W
    row = jnp.repeat(jnp.arange(rows, dtype=F32), GRID_W)
    col = jnp.tile(jnp.arange(GRID_W, dtype=F32), rows)
    inv = ROPE_BASE ** (-jnp.arange(ROPE_FREQS_PER_AXIS, dtype=F32) / ROPE_FREQS_PER_AXIS)
    ang = jnp.concatenate([row[:, None] * inv, col[:, None] * inv], axis=-1)
    cos, sin = jnp.cos(ang), jnp.sin(ang)
    cos_t = jnp.tile(cos, (1, LANES // (HEAD_DIM // 2)))
    sin_t = jnp.tile(jnp.concatenate([-sin, sin], axis=-1), (1, LANES // HEAD_DIM))
    return cos_t, sin_t


def _dft_tables(n):
    n0 = 64
    n1 = n // n0
    k = np.arange(n, dtype=np.int64)[:, None]
    ang1 = jnp.asarray(2.0 * np.pi * ((k * np.arange(n1 // 2)[None, :]) % n1) / n1, F32)
    ang0 = jnp.asarray(2.0 * np.pi * ((k * np.arange(n0)[None, :]) % n) / n, F32)
    c1, s1, c0, s0 = jnp.cos(ang1), jnp.sin(ang1), jnp.cos(ang0), jnp.sin(ang0)
    scale = 1.0 / math.sqrt(n)
    cos_n = (c1[:, :, None] * c0[:, None, :] - s1[:, :, None] * s0[:, None, :]).reshape(n, n // 2) * scale
    sin_n = (s1[:, :, None] * c0[:, None, :] + c1[:, :, None] * s0[:, None, :]).reshape(n, n // 2) * scale
    return jnp.concatenate([cos_n, -sin_n], axis=1).astype(BF)


def _fold_matrix(fb):
    i = np.arange(fb)[:, None]
    s = np.arange(2 * fb)[None, :]
    return jnp.asarray(np.where(s == fb - i, 1.0, 0.0), BF)


def _group_dft_tables(width):
    g = FNET_GROUP_DIM
    idx = np.arange(width)
    same = (idx[:, None] // g) == (idx[None, :] // g)
    ang = 2.0 * np.pi * (((idx[:, None] % g) * (idx[None, :] % g)) % g) / g
    scale = 1.0 / math.sqrt(g)
    bdc = np.where(same, np.cos(ang), 0.0) * scale
    bds = np.where(same, np.sin(ang), 0.0) * scale
    return jnp.asarray(bdc, F32).astype(BF), jnp.asarray(bds, F32).astype(BF)


def _group_mean_matrix(width):
    idx = np.arange(width)
    same = (idx[:, None] // HEAD_DIM) == (idx[None, :] // HEAD_DIM)
    return jnp.asarray(np.where(same, 1.0 / HEAD_DIM, 0.0), BF)


def _mixer_inputs(x3, mods, g_attn_l, w_in_bf, tables, qg, kg, gm):
    b, n, d = x3.shape
    tm = min(512, n)
    outs = _in_projection(x3.reshape(b * n, d), mods[0], mods[1], g_attn_l, w_in_bf, tables[0], tables[1],
                          qg, kg, gm, n, tm)
    return [o.reshape(b, n, o.shape[1]) for o in outs]


def _moe(sets, w_gate, w_up, w_down, layer):
    slots, xss = [], []
    for x1, h2, aff_t, _ in sets:
        cap = EC_FACTOR * x1.shape[1] // N_EXPERTS
        slot = _route(aff_t, cap)
        slots.append(slot)
        xss.append(_gather(slot, h2, cap))
    yes = _expert_ffn(xss, w_gate, w_up, w_down, layer, 512, 512)
    return [_combine(x1, m5, jnp.swapaxes(slot, 1, 2), jnp.swapaxes(aff_t, 1, 2), ye, min(512, x1.shape[1]))
            for (x1, _, aff_t, m5), slot, ye in zip(sets, slots, yes)]


def kernel(x, c, ctx, c_ctx, w_mod, b_mod, g_attn, g_ffn, w_in, ret_decay, diff_qn, diff_kn, diff_lambda,
           w_fnet_o, w_ret_o, w_diff_o, w_out, w_router, w_exp_gate, w_exp_up, w_exp_down):
    bsz, n, d = x.shape
    n_ctx = ctx.shape[1]
    depth = w_mod.shape[0]

    pad = (-(bsz + 1)) % 8
    cvecs = jnp.concatenate([c, c_ctx[None, :], jnp.zeros((pad, d), F32)], axis=0)
    mods = _modulation(cvecs, w_mod, b_mod)

    rope_lat = _rope_tables(n, True)
    rope_ctx = _rope_tables(n_ctx, False)
    cs_lat, cs_ctx = _dft_tables(n), _dft_tables(n_ctx)
    bdc, bds = _group_dft_tables(2 * LANES)
    jsh = _fold_matrix(LANES)
    gm = _group_mean_matrix(_W_DQ)
    zero_state = jnp.zeros((bsz, N_HEADS, HEAD_DIM, HEAD_V), F32)
    ret_chunk = 1024

    xc = ctx
    for layer in range(depth):
        last = layer == depth - 1
        lam_init = 0.8 - 0.6 * math.exp(-0.3 * layer)
        mx = [mods[layer, :bsz, j * d:(j + 1) * d].reshape(bsz, 1, d) for j in range(N_MOD)]
        mc = [mods[layer, bsz:bsz + 1, j * d:(j + 1) * d].reshape(1, 1, d) for j in range(N_MOD)]
        w_in_bf = w_in[layer].astype(BF)
        qg = jnp.tile(diff_qn[layer], _W_DQ // HEAD_DIM)[None, :]
        kg = jnp.tile(diff_kn[layer], _W_DK // HEAD_DIM)[None, :]
        g_a = g_attn[layer][None, :]
        g_f = g_ffn[layer][None, :]
        decays = ret_decay[layer]
        dlam = diff_lambda[layer]
        branch_w = (w_fnet_o[layer].astype(BF), w_ret_o[layer].astype(BF), w_diff_o[layer].astype(BF),
                    w_out[layer].astype(BF))
        rt = w_router[layer].T
        rth = rt.astype(BF)
        rtl = (rt - rth.astype(F32)).astype(BF)
        moe_sets = []

        fo_c, rq_c, rk_c, rv_c, sg_c, dq_c, dk_c, dv_c, gt_c = _mixer_inputs(xc, mc, g_a, w_in_bf, rope_ctx, qg, kg, gm)
        ro_c, s_f, s_b = _retention(rq_c, rk_c, rv_c, sg_c, decays, zero_state, zero_state, min(ret_chunk, n_ctx))
        if not last:
            f_c = _fourier_mix(fo_c, cs_ctx, bdc, bds, jsh, min(512, n_ctx))
            do_c = _diff_attention(dq_c, [dk_c], [dv_c], dlam, lam_init, min(256, n_ctx))
            rows_c = bsz * n_ctx
            x1_c, h2_c, aff_c = _merge(f_c.reshape(rows_c, -1), ro_c.reshape(rows_c, -1), do_c.reshape(rows_c, -1),
                                       gt_c.reshape(rows_c, -1), xc.reshape(rows_c, d), mc[2], mc[3], mc[4], g_f,
                                       *branch_w, rth, rtl, n_ctx, min(512, n_ctx))
            moe_sets.append((x1_c.reshape(bsz, n_ctx, d), h2_c.reshape(bsz, n_ctx, d), aff_c, mc[5]))

        fo, rq, rk, rv, sg, dq, dk, dv, gt = _mixer_inputs(x, mx, g_a, w_in_bf, rope_lat, qg, kg, gm)
        ro, _, _ = _retention(rq, rk, rv, sg, decays, s_f, s_b, min(ret_chunk, n))
        f = _fourier_mix(fo, cs_lat, bdc, bds, jsh, min(512, n))
        do = _diff_attention(dq, [dk, dk_c], [dv, dv_c], dlam, lam_init, min(512, n))
        rows = bsz * n
        x1, h2, aff = _merge(f.reshape(rows, -1), ro.reshape(rows, -1), do.reshape(rows, -1), gt.reshape(rows, -1),
                             x.reshape(rows, d), mx[2], mx[3], mx[4], g_f, *branch_w, rth, rtl, n, min(1024, n))
        moe_sets.append((x1.reshape(bsz, n, d), h2.reshape(bsz, n, d), aff, mx[5]))
        moe_out = _moe(moe_sets, w_exp_gate, w_exp_up, w_exp_down, layer)
        x = moe_out[-1]
        if not last:
            xc = moe_out[0]
    return x
```

```python
import functools
import math

import jax
import jax.numpy as jnp
import numpy as np
from jax import lax
from jax.experimental import pallas as pl
from jax.experimental.pallas import tpu as pltpu

F32 = jnp.float32
BF = jnp.bfloat16
I32 = jnp.int32

GRID_W = 64
HEAD_DIM = 64
ROPE_FREQS_PER_AXIS = HEAD_DIM // 4
ROPE_BASE = 10000.0
EPS = 1e-6
LOG2E = 1.4426950408889634
FNET_GROUP_DIM = 64
N_HEADS = 4
HEAD_V = 128
N_EXPERTS = 16
EC_FACTOR = 2
N_MOD = 6
RET_SUB = 256
MERGE_SUB = 1024
LANES = 128
VMEM_LIMIT = 56 * 1024 * 1024

_W_FO, _W_RQ, _W_RK, _W_RV, _W_RG, _W_DQ, _W_DK, _W_DV = 512, 256, 256, 512, 512, 512, 512, 512


def _cparams(sem):
    return pltpu.CompilerParams(dimension_semantics=sem, vmem_limit_bytes=VMEM_LIMIT)


def _sigmoid(v):
    return 1.0 / (1.0 + jnp.exp(-v))


def _nt_dot(a, b):
    return lax.dot_general(a, b, (((1,), (1,)), ((), ())), preferred_element_type=F32)


def _tn_dot(a, b):
    return lax.dot_general(a, b, (((0,), (0,)), ((), ())), preferred_element_type=F32)


def _mod_kernel(c_ref, w_ref, b_ref, o_ref):
    cv = c_ref[...]
    s = cv * _sigmoid(cv)
    o_ref[0] = jnp.dot(s.astype(BF), w_ref[0].astype(BF), preferred_element_type=F32) + b_ref[0]


def _modulation(cvecs, w_mod, b_mod):
    depth, d, wd = w_mod.shape
    rows = cvecs.shape[0]
    tn = 1536
    return pl.pallas_call(
        _mod_kernel,
        grid=(depth, wd // tn),
        in_specs=[pl.BlockSpec((rows, d), lambda l, j: (0, 0)),
                  pl.BlockSpec((1, d, tn), lambda l, j: (l, 0, j)),
                  pl.BlockSpec((1, 1, tn), lambda l, j: (l, 0, j))],
        out_specs=pl.BlockSpec((1, rows, tn), lambda l, j: (l, 0, j)),
        out_shape=jax.ShapeDtypeStruct((depth, rows, wd), F32),
        compiler_params=_cparams(("arbitrary", "arbitrary")),
        name="modulation",
    )(cvecs, w_mod, b_mod.reshape(depth, 1, wd))


def _rope(x, cos_t, sin_t):
    lane = lax.broadcasted_iota(I32, (x.shape[0], LANES), 1)
    first = (lane & 63) < 32
    outs = []
    for j in range(x.shape[1] // LANES):
        xc = x[:, j * LANES:(j + 1) * LANES]
        sw = jnp.where(first, pltpu.roll(xc, LANES - 32, 1), pltpu.roll(xc, 32, 1))
        outs.append(xc * cos_t + sw * sin_t)
    return jnp.concatenate(outs, axis=1)


def _group_rms(x, gm, gain):
    ms = jnp.dot((x * x).astype(BF), gm, preferred_element_type=F32)
    return x * lax.rsqrt(ms + EPS) * gain


def _inproj_kernel(x_ref, shift_ref, scale_ref, g_ref, w_ref, cos_ref, sin_ref, qg_ref, kg_ref, gm_ref,
                   fo_ref, rq_ref, rk_ref, rv_ref, sg_ref, dq_ref, dk_ref, dv_ref, gt_ref):
    x = x_ref[...]
    d = x.shape[1]
    ms = jnp.mean(x * x, axis=-1, keepdims=True)
    h = x * lax.rsqrt(ms + EPS) * g_ref[...]
    h = h * (1.0 + scale_ref[0]) + shift_ref[0]
    hb = h.astype(BF)
    cos_t = cos_ref[...]
    sin_t = sin_ref[...]
    gm = gm_ref[...]

    def proj(a, width):
        return jnp.dot(hb, w_ref[:, a:a + width], preferred_element_type=F32)

    a = 0
    fo_ref[...] = proj(a, _W_FO).astype(BF)
    a += _W_FO
    rq_ref[...] = _rope(proj(a, _W_RQ) * (HEAD_DIM ** -0.5), cos_t, sin_t).astype(BF)
    a += _W_RQ
    rk_ref[...] = _rope(proj(a, _W_RK), cos_t, sin_t).astype(BF)
    a += _W_RK
    rv_ref[...] = proj(a, _W_RV).astype(BF)
    a += _W_RV
    rg = proj(a, _W_RG)
    sg_ref[...] = (rg * _sigmoid(rg)).astype(BF)
    a += _W_RG
    dq = _group_rms(proj(a, _W_DQ), gm, qg_ref[...])
    dq_ref[...] = (_rope(dq, cos_t, sin_t) * (HEAD_DIM ** -0.5 * LOG2E)).astype(BF)
    a += _W_DQ
    dk = _group_rms(proj(a, _W_DK), gm, kg_ref[...])
    dk_ref[...] = _rope(dk, cos_t, sin_t).astype(BF)
    a += _W_DK
    dv = proj(a, _W_DV).astype(BF)
    ones = jnp.ones((dv.shape[0], HEAD_V), BF)
    for hh in range(N_HEADS):
        dv_ref[:, 2 * hh * HEAD_V:(2 * hh + 1) * HEAD_V] = dv[:, hh * HEAD_V:(hh + 1) * HEAD_V]
        dv_ref[:, (2 * hh + 1) * HEAD_V:(2 * hh + 2) * HEAD_V] = ones
    a += _W_DV
    for j in range(3):
        gl = proj(a + j * d, d)
        gt_ref[:, j * d:(j + 1) * d] = _sigmoid(gl).astype(BF)


def _in_projection(x2d, shift, scale, g, w_in_bf, cos_t, sin_t, qg, kg, gm, n_seq, tm):
    rows, d = x2d.shape
    tiles_per_b = n_seq // tm
    nb = shift.shape[0]
    win = w_in_bf.shape[1]
    widths = (_W_FO, _W_RQ, _W_RK, _W_RV, _W_RG, _W_DQ, _W_DK, 2 * _W_DV, 3 * d)

    def mod_map(i):
        return ((i // tiles_per_b) if nb > 1 else 0, 0, 0)

    def pos_map(i):
        return (i % tiles_per_b, 0)

    const2 = lambda i: (0, 0)
    return pl.pallas_call(
        _inproj_kernel,
        grid=(rows // tm,),
        in_specs=[pl.BlockSpec((tm, d), lambda i: (i, 0)),
                  pl.BlockSpec((1, 1, d), mod_map),
                  pl.BlockSpec((1, 1, d), mod_map),
                  pl.BlockSpec((1, d), const2),
                  pl.BlockSpec((d, win), const2, pipeline_mode=pl.Buffered(1)),
                  pl.BlockSpec((tm, LANES), pos_map),
                  pl.BlockSpec((tm, LANES), pos_map),
                  pl.BlockSpec((1, _W_DQ), const2),
                  pl.BlockSpec((1, _W_DK), const2),
                  pl.BlockSpec((_W_DQ, _W_DQ), const2)],
        out_specs=[pl.BlockSpec((tm, w), lambda i: (i, 0)) for w in widths],
        out_shape=[jax.ShapeDtypeStruct((rows, w), BF) for w in widths],
        compiler_params=_cparams(("arbitrary",)),
        name="in_projection",
    )(x2d, shift, scale, g, w_in_bf, cos_t, sin_t, qg, kg, gm)


def _fourier_kernel(x_ref, cs_ref, bc_ref, bs_ref, jsh_ref, o_ref, z_ref, eo_ref):
    n, w = x_ref.shape[1], x_ref.shape[2]
    half = n // 2
    gw = bc_ref.shape[0]
    fb = jsh_ref.shape[0]
    tr = o_ref.shape[1]

    @pl.when(pl.program_id(1) == 0)
    def _():
        x = x_ref[0]
        for g in range(w // gw):
            xs = x[:, g * gw:(g + 1) * gw]
            z_ref[0, :, g * gw:(g + 1) * gw] = jnp.dot(xs, bc_ref[...], preferred_element_type=F32).astype(BF)
            z_ref[1, :, g * gw:(g + 1) * gw] = jnp.dot(xs, bs_ref[...], preferred_element_type=F32).astype(BF)
        jsh = jsh_ref[...]
        n_blocks = half // fb
        for blk in range(n_blocks):
            hi_blk = 2 * n_blocks - 1 - blk
            for t in range(2):
                top = z_ref[t, hi_blk * fb:(hi_blk + 1) * fb, :]
                nxt = jnp.zeros_like(top) if blk == 0 else z_ref[t, (hi_blk + 1) * fb:(hi_blk + 2) * fb, :]
                partner = jnp.dot(jsh, jnp.concatenate([top, nxt], axis=0), preferred_element_type=F32)
                own = z_ref[t, blk * fb:(blk + 1) * fb, :].astype(F32)
                folded = own + partner if t == 0 else own - partner
                eo_ref[t * half + blk * fb:t * half + (blk + 1) * fb, :] = folded.astype(BF)

    y = jnp.dot(cs_ref[...], eo_ref[...], preferred_element_type=F32)
    parity = lax.broadcasted_iota(I32, (tr, 1), 0) & 1
    sign = (1.0 - 2.0 * parity.astype(F32)) * (1.0 / math.sqrt(n))
    o_ref[0] = (y + sign * z_ref[0, half:half + 1, :].astype(F32)).astype(BF)


def _fourier_mix(fo, cs, bdc, bds, jsh, tr):
    b, n, w = fo.shape
    assert n % (2 * jsh.shape[0]) == 0 and tr % 2 == 0
    return pl.pallas_call(
        _fourier_kernel,
        grid=(b, n // tr),
        in_specs=[pl.BlockSpec((1, n, w), lambda i, j: (i, 0, 0)),
                  pl.BlockSpec((tr, n), lambda i, j: (j, 0)),
                  pl.BlockSpec(bdc.shape, lambda i, j: (0, 0)),
                  pl.BlockSpec(bds.shape, lambda i, j: (0, 0)),
                  pl.BlockSpec(jsh.shape, lambda i, j: (0, 0))],
        out_specs=pl.BlockSpec((1, tr, w), lambda i, j: (i, j, 0)),
        out_shape=jax.ShapeDtypeStruct((b, n, w), BF),
        scratch_shapes=[pltpu.VMEM((2, n, w), BF), pltpu.VMEM((n, w), BF)],
        compiler_params=_cparams(("arbitrary", "arbitrary")),
        name="fourier_mix",
    )(fo, cs, bdc, bds, jsh)


def _ret_kernel(lg_ref, q_ref, k_ref, v_ref, sg_ref, sf_ref, sb_ref, o_ref, sfo_ref, sbo_ref,
                acc_ref, tmp_ref, st_ref, dm_ref, qd_ref, qm_ref, kd_ref, cd_ref, *, chunk, n_chunks, sub):
    b = pl.program_id(0)
    p = pl.program_id(1)
    c = pl.program_id(2)
    sf = float(sub)
    n_sub = chunk // sub

    @pl.when((b == 0) & (p == 0) & (c == 0))
    def _tables():
        ii = lax.broadcasted_iota(I32, (sub, sub), 0).astype(F32)
        jj = lax.broadcasted_iota(I32, (sub, sub), 1).astype(F32)
        pos = lax.broadcasted_iota(I32, (sub, LANES), 0).astype(F32)
        lane = lax.broadcasted_iota(I32, (sub, LANES), 1)
        for dr in range(2):
            if dr == 0:
                rel, keep = ii - jj, ii >= jj
                qe, ke = pos + 1.0, (sf - 1.0) - pos
            else:
                rel, keep = jj - ii, jj > ii
                qe, ke = sf - pos, pos
            for hp in range(N_HEADS // 2):
                lg0 = jnp.full((sub, LANES), lg_ref[dr, 2 * hp], F32)
                lg1 = jnp.full((sub, LANES), lg_ref[dr, 2 * hp + 1], F32)
                lgp = jnp.where(lane < HEAD_DIM, lg0, lg1)
                kd_ref[dr, hp] = jnp.exp(lgp * ke)
            for h in range(N_HEADS):
                lgs = lg_ref[dr, h]
                in_head = (lane >= (h % 2) * HEAD_DIM) & (lane < (h % 2 + 1) * HEAD_DIM)
                dm_ref[dr, h] = jnp.where(keep, jnp.exp(jnp.full((sub, sub), lgs, F32) * jnp.maximum(rel, 0.0)), 0.0)
                qd_ref[dr, h] = jnp.where(in_head, jnp.exp(jnp.full((sub, LANES), lgs, F32) * qe), 0.0)
                cd_ref[dr, h] = jnp.exp(jnp.full((LANES, LANES), lgs, F32) * sf)
        for h in range(N_HEADS):
            in_head = (lane >= (h % 2) * HEAD_DIM) & (lane < (h % 2 + 1) * HEAD_DIM)
            qm_ref[h] = jnp.where(in_head, 1.0, 0.0)

    def load_state(src_ref):
        z = jnp.zeros((HEAD_DIM, HEAD_V), F32)
        for h in range(N_HEADS):
            s = src_ref[0, h]
            st_ref[h] = jnp.concatenate([s, z], axis=0) if h % 2 == 0 else jnp.concatenate([z, s], axis=0)

    @pl.when((c == 0) & (p == 0))
    def _():
        load_state(sf_ref)

    @pl.when((c == 0) & (p == 1))
    def _():
        load_state(sb_ref)

    sts = [st_ref[h] for h in range(N_HEADS)]
    for u in range(n_sub):
        r0 = pl.multiple_of(jnp.where(p == 0, u, n_sub - 1 - u) * sub, sub)
        rows = pl.ds(r0, sub)
        q = q_ref[0, rows, :]
        k = k_ref[0, rows, :]
        v = v_ref[0, rows, :]
        outs = []
        for hp in range(N_HEADS // 2):
            qp = q[:, hp * LANES:(hp + 1) * LANES].astype(F32)
            kp = k[:, hp * LANES:(hp + 1) * LANES]
            kdp = (kp.astype(F32) * kd_ref[p, hp]).astype(BF)
            for h in (2 * hp, 2 * hp + 1):
                vh = v[:, h * HEAD_V:(h + 1) * HEAD_V]
                qm = (qp * qm_ref[h]).astype(BF)
                qdq = (qp * qd_ref[p, h]).astype(BF)
                s = _nt_dot(qm, kp) * dm_ref[p, h]
                o = (jnp.dot(s.astype(BF), vh, preferred_element_type=F32)
                     + jnp.dot(qdq, sts[h].astype(BF), preferred_element_type=F32))
                sts[h] = sts[h] * cd_ref[p, h] + _tn_dot(kdp, vh)
                outs.append(o)
        tmp_ref[rows, :] = jnp.concatenate(outs, axis=1)
    for h in range(N_HEADS):
        st_ref[h] = sts[h]
    idx = jnp.where(p == 0, c, n_chunks - 1 - c)

    @pl.when(p == 0)
    def _():
        acc_ref[idx] = tmp_ref[...]

    @pl.when(p == 1)
    def _():
        tot = acc_ref[idx] + tmp_ref[...]
        sg = sg_ref[0].astype(F32)
        ys = []
        for h in range(N_HEADS):
            oh = tot[:, h * HEAD_V:(h + 1) * HEAD_V]
            mu = jnp.mean(oh, axis=-1, keepdims=True)
            dlt = oh - mu
            var = jnp.mean(dlt * dlt, axis=-1, keepdims=True)
            ys.append(dlt * lax.rsqrt(var + EPS))
        o_ref[0] = (jnp.concatenate(ys, axis=1) * sg).astype(BF)

    def store_state(dst_ref):
        for h in range(N_HEADS):
            r0 = (h % 2) * HEAD_DIM
            dst_ref[0, h] = st_ref[h, r0:r0 + HEAD_DIM, :]

    @pl.when((c == n_chunks - 1) & (p == 0))
    def _():
        store_state(sfo_ref)

    @pl.when((c == n_chunks - 1) & (p == 1))
    def _():
        store_state(sbo_ref)


def _retention(rq, rk, rv, sg, decays, s_f, s_b, chunk):
    b, n, _ = rq.shape
    n_chunks = n // chunk
    wv = rv.shape[2]

    def seq_map(i, p, c):
        return (i, jnp.where(p == 0, c, n_chunks - 1 - c), 0)

    def out_map(i, p, c):
        return (i, jnp.where(p == 0, n_chunks - 1, n_chunks - 1 - c), 0)

    st_spec = pl.BlockSpec((1, N_HEADS, HEAD_DIM, HEAD_V), lambda i, p, c: (i, 0, 0, 0))
    st_shape = jax.ShapeDtypeStruct((b, N_HEADS, HEAD_DIM, HEAD_V), F32)
    sub = min(RET_SUB, chunk)
    kern = functools.partial(_ret_kernel, chunk=chunk, n_chunks=n_chunks, sub=sub)
    return pl.pallas_call(
        kern,
        grid=(b, 2, n_chunks),
        in_specs=[pl.BlockSpec(memory_space=pltpu.SMEM),
                  pl.BlockSpec((1, chunk, rq.shape[2]), seq_map),
                  pl.BlockSpec((1, chunk, rk.shape[2]), seq_map),
                  pl.BlockSpec((1, chunk, wv), seq_map),
                  pl.BlockSpec((1, chunk, wv), seq_map),
                  st_spec, st_spec],
        out_specs=[pl.BlockSpec((1, chunk, wv), out_map), st_spec, st_spec],
        out_shape=[jax.ShapeDtypeStruct((b, n, wv), BF), st_shape, st_shape],
        scratch_shapes=[pltpu.VMEM((n_chunks, chunk, wv), F32),
                        pltpu.VMEM((chunk, wv), F32),
                        pltpu.VMEM((N_HEADS, LANES, HEAD_V), F32),
                        pltpu.VMEM((2, N_HEADS, sub, sub), F32),
                        pltpu.VMEM((2, N_HEADS, sub, LANES), F32),
                        pltpu.VMEM((N_HEADS, sub, LANES), F32),
                        pltpu.VMEM((2, N_HEADS // 2, sub, LANES), F32),
                        pltpu.VMEM((2, N_HEADS, LANES, HEAD_V), F32)],
        compiler_params=_cparams(("arbitrary", "arbitrary", "arbitrary")),
        name="retention",
    )(decays, rq, rk, rv, sg, s_f, s_b)


def _dattn_kernel(dl_ref, q_ref, *refs, n_src, lam_init):
    k_refs = refs[:n_src]
    v_refs = refs[n_src:2 * n_src]
    o_ref = refs[2 * n_src]
    dl = dl_ref[...]
    lam = (jnp.exp(jnp.sum(dl[0:1] * dl[1:2], axis=-1, keepdims=True))
           - jnp.exp(jnp.sum(dl[2:3] * dl[3:4], axis=-1, keepdims=True)) + lam_init)
    q = q_ref[0].astype(F32)
    tq = q.shape[0]
    lane = lax.broadcasted_iota(I32, (tq, LANES), 1)
    ys = []
    for h in range(N_HEADS):
        qh = q[:, h * LANES:(h + 1) * LANES]
        qs = jnp.concatenate([jnp.where(lane < HEAD_DIM, qh, 0.0), jnp.where(lane >= HEAD_DIM, qh, 0.0)],
                             axis=0).astype(BF)
        ss = [_nt_dot(qs, kr[0, :, h * LANES:(h + 1) * LANES]) for kr in k_refs]
        m = ss[0].max(axis=-1, keepdims=True)
        for s in ss[1:]:
            m = jnp.maximum(m, s.max(axis=-1, keepdims=True))
        acc = jnp.zeros((2 * tq, 2 * HEAD_V), F32)
        for s, vr in zip(ss, v_refs):
            pexp = jnp.exp2(s - m).astype(BF)
            acc = acc + jnp.dot(pexp, vr[0, :, 2 * h * HEAD_V:(2 * h + 2) * HEAD_V], preferred_element_type=F32)
        o = acc[:, :HEAD_V] / acc[:, HEAD_V:]
        oh = o[:tq] - lam * o[tq:]
        ms = jnp.mean(oh * oh, axis=-1, keepdims=True)
        ys.append(oh * lax.rsqrt(ms + EPS) * (1.0 - lam_init))
    o_ref[0] = jnp.concatenate(ys, axis=1).astype(BF)


def _diff_attention(dq, ks, vs, dlam, lam_init, tq):
    b, n, w = dq.shape
    n_src = len(ks)
    kern = functools.partial(_dattn_kernel, n_src=n_src, lam_init=lam_init)
    kv_specs = [pl.BlockSpec((1, a.shape[1], a.shape[2]), lambda i, j: (i, 0, 0)) for a in (*ks, *vs)]
    return pl.pallas_call(
        kern,
        grid=(b, n // tq),
        in_specs=[pl.BlockSpec(dlam.shape, lambda i, j: (0, 0)),
                  pl.BlockSpec((1, tq, w), lambda i, j: (i, j, 0))] + kv_specs,
        out_specs=pl.BlockSpec((1, tq, w), lambda i, j: (i, j, 0)),
        out_shape=jax.ShapeDtypeStruct((b, n, w), BF),
        compiler_params=_cparams(("arbitrary", "arbitrary")),
        name="diff_attention",
    )(dlam, dq, *ks, *vs)


def _merge_kernel(f_ref, ro_ref, do_ref, gt_ref, x_ref, m2_ref, m3_ref, m4_ref, g_ref,
                  wf_ref, wr_ref, wd_ref, wo_ref, rcat_ref, x1_ref, h2_ref, aff_ref):
    tm, d = x_ref.shape
    st = min(MERGE_SUB, tm)
    for r0 in range(0, tm, st):
        rows = slice(r0, r0 + st)
        t = (gt_ref[rows, 0:d].astype(F32) * jnp.dot(f_ref[rows, :], wf_ref[...], preferred_element_type=F32)
             + gt_ref[rows, d:2 * d].astype(F32) * jnp.dot(ro_ref[rows, :], wr_ref[...], preferred_element_type=F32)
             + gt_ref[rows, 2 * d:3 * d].astype(F32) * jnp.dot(do_ref[rows, :], wd_ref[...],
                                                                preferred_element_type=F32))
        mix = jnp.dot(t.astype(BF), wo_ref[...], preferred_element_type=F32)
        x1 = x_ref[rows, :] + m2_ref[0] * mix
        x1_ref[rows, :] = x1
        ms = jnp.mean(x1 * x1, axis=-1, keepdims=True)
        h2 = x1 * lax.rsqrt(ms + EPS) * g_ref[...]
        h2 = h2 * (1.0 + m4_ref[0]) + m3_ref[0]
        hi = h2.astype(BF)
        lo = (h2 - hi.astype(F32)).astype(BF)
        h2_ref[rows, :] = hi
        l1 = jnp.dot(hi, rcat_ref[...], preferred_element_type=F32)
        lt = l1[:, :LANES] + l1[:, LANES:] + jnp.dot(lo, rcat_ref[:, :LANES], preferred_element_type=F32)
        lane = lax.broadcasted_iota(I32, lt.shape, 1)
        lt = jnp.where(lane < N_EXPERTS, lt, -jnp.inf)
        ex = jnp.exp(lt - lt.max(axis=-1, keepdims=True))
        aff = ex / jnp.sum(ex, axis=-1, keepdims=True)
        aff_ref[0, :, rows] = aff.T[:N_EXPERTS, :]


def _merge(f, ro, do, gt, x2d, m2, m3, m4, g, wf, wr, wd, wo, rcat, n_seq, tm):
    rows, d = x2d.shape
    tiles_per_b = n_seq // tm
    nb = m2.shape[0]
    bsz = rows // n_seq
    ne = N_EXPERTS

    def mod_map(i):
        return ((i // tiles_per_b) if nb > 1 else 0, 0, 0)

    const2 = lambda i: (0, 0)
    row_spec = lambda w: pl.BlockSpec((tm, w), lambda i: (i, 0))
    return pl.pallas_call(
        _merge_kernel,
        grid=(rows // tm,),
        in_specs=[row_spec(f.shape[1]), row_spec(ro.shape[1]), row_spec(do.shape[1]), row_spec(gt.shape[1]),
                  row_spec(d),
                  pl.BlockSpec((1, 1, d), mod_map), pl.BlockSpec((1, 1, d), mod_map), pl.BlockSpec((1, 1, d), mod_map),
                  pl.BlockSpec((1, d), const2),
                  pl.BlockSpec(wf.shape, const2), pl.BlockSpec(wr.shape, const2), pl.BlockSpec(wd.shape, const2),
                  pl.BlockSpec(wo.shape, const2), pl.BlockSpec(rcat.shape, const2)],
        out_specs=[row_spec(d), row_spec(d),
                   pl.BlockSpec((1, ne, tm), lambda i: (i // tiles_per_b, 0, i % tiles_per_b))],
        out_shape=[jax.ShapeDtypeStruct((rows, d), F32), jax.ShapeDtypeStruct((rows, d), BF),
                   jax.ShapeDtypeStruct((bsz, ne, n_seq), F32)],
        compiler_params=_cparams(("arbitrary",)),
        name="merge_router",
    )(f, ro, do, gt, x2d, m2, m3, m4, g, wf, wr, wd, wo, rcat)


def _route_kernel(a_ref, slot_ref, *, cap, blk):
    a = a_ref[...]
    ne, n = a.shape
    capf = float(cap)

    def enough(t):
        return jnp.sum(jnp.where(a >= t, 1.0, 0.0), axis=-1, keepdims=True) >= capf

    tiny = jnp.full((ne, 1), 2.0 ** -126, F32)
    found = enough(tiny)
    cur = tiny
    for step in (64, 32, 16, 8, 4, 2, 1):
        cand = cur * (2.0 ** step)
        cur = jnp.where(enough(cand), cand, cur)
    base = cur

    def mantissa_bit(_, carry):
        cur, stepv = carry
        cand = cur + stepv
        return jnp.where(enough(cand), cand, cur), stepv * 0.5

    cur, ulp = lax.fori_loop(0, 23, mantissa_bit, (cur, base * 0.5))
    lo = jnp.where(found, cur, 0.0)
    hi = jnp.where(found, cur + ulp * 2.0, tiny)

    def refine(_, carry):
        lo, hi = carry
        mid = lo + (hi - lo) * 0.5
        ok = enough(mid)
        return jnp.where(ok, mid, lo), jnp.where(ok, hi, mid)

    lo, hi = lax.fori_loop(0, 24, refine, (lo, hi))
    ri = lax.broadcasted_iota(I32, (blk, blk), 0)
    ci = lax.broadcasted_iota(I32, (blk, blk), 1)
    upper = jnp.where(ri < ci, 1.0, 0.0).astype(BF)

    def excl_cumsum(m):
        carry = jnp.zeros((ne, 1), F32)
        outs = []
        for j in range(n // blk):
            mb = m[:, j * blk:(j + 1) * blk]
            outs.append(jnp.dot(mb.astype(BF), upper, preferred_element_type=F32) + carry)
            carry = carry + jnp.sum(mb, axis=-1, keepdims=True)
        return jnp.concatenate(outs, axis=1)

    gt = a >= hi
    tie = (a >= lo) & (a < hi)
    need = capf - jnp.sum(jnp.where(gt, 1.0, 0.0), axis=-1, keepdims=True)
    sel = gt | (tie & (excl_cumsum(jnp.where(tie, 1.0, 0.0)) < need))
    slot = excl_cumsum(jnp.where(sel, 1.0, 0.0))
    slot_ref[...] = jnp.where(sel, slot.astype(I32), -1)


def _route(aff_t, cap):
    b, ne, n = aff_t.shape
    kern = functools.partial(_route_kernel, cap=cap, blk=min(256, n))
    slot = pl.pallas_call(
        kern,
        grid=(1,),
        in_specs=[pl.BlockSpec((b * ne, n), lambda i: (0, 0))],
        out_specs=pl.BlockSpec((b * ne, n), lambda i: (0, 0)),
        out_shape=jax.ShapeDtypeStruct((b * ne, n), I32),
        compiler_params=_cparams(("arbitrary",)),
        name="route",
    )(aff_t.reshape(b * ne, n))
    return slot.reshape(b, ne, n)


def _gather_kernel(slot_ref, h_ref, xs_ref, *, cap):
    n = slot_ref.shape[3]
    sub = lax.broadcasted_iota(I32, (cap, n), 0)
    hb = h_ref[0]
    for g in range(slot_ref.shape[1]):
        onehot = jnp.where(sub == slot_ref[0, g], 1.0, 0.0).astype(BF)
        xs_ref[0, g] = jnp.dot(onehot, hb, preferred_element_type=F32).astype(BF)


def _gather(slot, h2, cap):
    b, ne, n = slot.shape
    d = h2.shape[2]
    eg = max(1, min(ne, 1024 // cap))
    kern = functools.partial(_gather_kernel, cap=cap)
    return pl.pallas_call(
        kern,
        grid=(b, ne // eg),
        in_specs=[pl.BlockSpec((1, eg, 1, n), lambda i, e: (i, e, 0, 0)),
                  pl.BlockSpec((1, n, d), lambda i, e: (i, 0, 0))],
        out_specs=pl.BlockSpec((1, eg, cap, d), lambda i, e: (i, e, 0, 0)),
        out_shape=jax.ShapeDtypeStruct((b, ne, cap, d), BF),
        compiler_params=_cparams(("arbitrary", "arbitrary")),
        name="moe_gather",
    )(slot.reshape(b, ne, 1, n), h2)


def _ffn_kernel(*refs, n_src, n_fc):
    xs_refs = refs[:n_src]
    wg_ref, wu_ref, wd_ref = refs[n_src:n_src + 3]
    ye_refs = refs[n_src + 3:2 * n_src + 3]
    x_ref, hid_ref = refs[2 * n_src + 3:]
    j = pl.program_id(1)
    fc = wg_ref.shape[3]
    row_spans = []
    r0 = 0
    for xr in xs_refs:
        rows = xr.shape[0] * xr.shape[2]
        row_spans.append((r0, rows))
        r0 += rows

    @pl.when(j == 0)
    def _():
        for xr, (s0, rows) in zip(xs_refs, row_spans):
            x_ref[s0:s0 + rows, :] = xr[:, 0].reshape(rows, xr.shape[3])

    @pl.when(j < n_fc)
    def _():
        x = x_ref[...]
        g = jnp.dot(x, wg_ref[0, 0].astype(BF), preferred_element_type=F32)
        u = jnp.dot(x, wu_ref[0, 0].astype(BF), preferred_element_type=F32)
        hid_ref[j] = (g * _sigmoid(g) * u).astype(BF)

    @pl.when(j >= n_fc)
    def _():
        y = jnp.dot(hid_ref[0], wd_ref[0, 0, 0:fc, :].astype(BF), preferred_element_type=F32)
        for k in range(1, n_fc):
            y = y + jnp.dot(hid_ref[k], wd_ref[0, 0, k * fc:(k + 1) * fc, :].astype(BF), preferred_element_type=F32)
        for yr, (s0, rows) in zip(ye_refs, row_spans):
            yr[:, 0] = y[s0:s0 + rows].reshape(yr.shape[0], yr.shape[2], yr.shape[3]).astype(BF)


def _expert_ffn(xs_list, w_gate, w_up, w_down, layer, fchunk, ochunk):
    ne, d = xs_list[0].shape[1], xs_list[0].shape[3]
    f = w_gate.shape[3]
    n_fc, n_oc = f // fchunk, d // ochunk
    n_src = len(xs_list)
    total_rows = sum(x.shape[0] * x.shape[2] for x in xs_list)
    kern = functools.partial(_ffn_kernel, n_src=n_src, n_fc=n_fc)
    up_map = lambda e, j: (layer, e, 0, jnp.minimum(j, n_fc - 1))
    down_map = lambda e, j: (layer, e, 0, jnp.maximum(j - n_fc, 0))
    out_map = lambda e, j: (0, e, 0, jnp.maximum(j - n_fc, 0))
    outs = pl.pallas_call(
        kern,
        grid=(ne, n_fc + n_oc),
        in_specs=[pl.BlockSpec((x.shape[0], 1, x.shape[2], d), lambda e, j: (0, e, 0, 0), pipeline_mode=pl.Buffered(1))
                  for x in xs_list]
        + [pl.BlockSpec((1, 1, d, fchunk), up_map),
           pl.BlockSpec((1, 1, d, fchunk), up_map),
           pl.BlockSpec((1, 1, f, ochunk), down_map)],
        out_specs=[pl.BlockSpec((x.shape[0], 1, x.shape[2], ochunk), out_map) for x in xs_list],
        out_shape=[jax.ShapeDtypeStruct(x.shape, BF) for x in xs_list],
        scratch_shapes=[pltpu.VMEM((total_rows, d), BF), pltpu.VMEM((n_fc, total_rows, fchunk), BF)],
        compiler_params=_cparams(("arbitrary", "arbitrary")),
        name="expert_ffn",
    )(*xs_list, w_gate, w_up, w_down)
    return list(outs)


def _combine_kernel(x1_ref, m5_ref, slot_ref, aff_ref, ye_ref, o_ref, *, cap):
    st = slot_ref[0]
    at = aff_ref[0]
    tn, ne = st.shape
    lane = lax.broadcasted_iota(I32, (tn, cap), 1)
    acc = jnp.zeros(x1_ref.shape[1:], F32)
    for e in range(ne):
        onehot = jnp.where(st[:, e:e + 1] == lane, 1.0, 0.0).astype(BF)
        acc = acc + jnp.dot(onehot, ye_ref[0, e], preferred_element_type=F32) * at[:, e:e + 1]
    o_ref[0] = x1_ref[0] + m5_ref[0] * acc


def _combine(x1, m5, slot_tok, aff_tok, ye, tn):
    b, n, d = x1.shape
    ne, cap = ye.shape[1], ye.shape[2]
    nb = m5.shape[0]
    kern = functools.partial(_combine_kernel, cap=cap)
    return pl.pallas_call(
        kern,
        grid=(b, n // tn),
        in_specs=[pl.BlockSpec((1, tn, d), lambda i, j: (i, j, 0)),
                  pl.BlockSpec((1, 1, d), lambda i, j: (i if nb > 1 else 0, 0, 0)),
                  pl.BlockSpec((1, tn, ne), lambda i, j: (i, j, 0)),
                  pl.BlockSpec((1, tn, ne), lambda i, j: (i, j, 0)),
                  pl.BlockSpec((1, ne, cap, d), lambda i, j: (i, 0, 0, 0))],
        out_specs=pl.BlockSpec((1, tn, d), lambda i, j: (i, j, 0)),
        out_shape=jax.ShapeDtypeStruct((b, n, d), F32),
        compiler_params=_cparams(("arbitrary", "arbitrary")),
        name="moe_combine",
    )(x1, m5, slot_tok, aff_tok, ye)


def _rope_tables(n, use_rope):
    if not use_rope:
        return jnp.ones((n, LANES), F32), jnp.zeros((n, LANES), F32)
    rows = n // GRID_W
    row = jnp.repeat(jnp.arange(rows, dtype=F32), GRID_W)
    col = jnp.tile(jnp.arange(GRID_W, dtype=F32), rows)
    inv = ROPE_BASE ** (-jnp.arange(ROPE_FREQS_PER_AXIS, dtype=F32) / ROPE_FREQS_PER_AXIS)
    ang = jnp.concatenate([row[:, None] * inv, col[:, None] * inv], axis=-1)
    cos, sin = jnp.cos(ang), jnp.sin(ang)
    cos_t = jnp.tile(cos, (1, LANES // (HEAD_DIM // 2)))
    sin_t = jnp.tile(jnp.concatenate([-sin, sin], axis=-1), (1, LANES // HEAD_DIM))
    return cos_t, sin_t


def _dft_tables(n):
    n0 = 64
    n1 = n // n0
    k = np.arange(n, dtype=np.int64)[:, None]
    ang1 = jnp.asarray(2.0 * np.pi * ((k * np.arange(n1 // 2)[None, :]) % n1) / n1, F32)
    ang0 = jnp.asarray(2.0 * np.pi * ((k * np.arange(n0)[None, :]) % n) / n, F32)
    c1, s1, c0, s0 = jnp.cos(ang1), jnp.sin(ang1), jnp.cos(ang0), jnp.sin(ang0)
    scale = 1.0 / math.sqrt(n)
    cos_n = (c1[:, :, None] * c0[:, None, :] - s1[:, :, None] * s0[:, None, :]).reshape(n, n // 2) * scale
    sin_n = (s1[:, :, None] * c0[:, None, :] + c1[:, :, None] * s0[:, None, :]).reshape(n, n // 2) * scale
    return jnp.concatenate([cos_n, -sin_n], axis=1).astype(BF)


def _fold_matrix(fb):
    i = np.arange(fb)[:, None]
    s = np.arange(2 * fb)[None, :]
    return jnp.asarray(np.where(s == fb - i, 1.0, 0.0), BF)


def _group_dft_tables(width):
    g = FNET_GROUP_DIM
    idx = np.arange(width)
    same = (idx[:, None] // g) == (idx[None, :] // g)
    ang = 2.0 * np.pi * (((idx[:, None] % g) * (idx[None, :] % g)) % g) / g
    scale = 1.0 / math.sqrt(g)
    bdc = np.where(same, np.cos(ang), 0.0) * scale
    bds = np.where(same, np.sin(ang), 0.0) * scale
    return jnp.asarray(bdc, F32).astype(BF), jnp.asarray(bds, F32).astype(BF)


def _group_mean_matrix(width):
    idx = np.arange(width)
    same = (idx[:, None] // HEAD_DIM) == (idx[None, :] // HEAD_DIM)
    return jnp.asarray(np.where(same, 1.0 / HEAD_DIM, 0.0), BF)


def _mixer_inputs(x3, mods, g_attn_l, w_in_bf, tables, qg, kg, gm):
    b, n, d = x3.shape
    tm = min(512, n)
    outs = _in_projection(x3.reshape(b * n, d), mods[0], mods[1], g_attn_l, w_in_bf, tables[0], tables[1],
                          qg, kg, gm, n, tm)
    return [o.reshape(b, n, o.shape[1]) for o in outs]


def _moe(sets, w_gate, w_up, w_down, layer):
    slots, xss = [], []
    for x1, h2, aff_t, _ in sets:
        cap = EC_FACTOR * x1.shape[1] // N_EXPERTS
        slot = _route(aff_t, cap)
        slots.append(slot)
        xss.append(_gather(slot, h2, cap))
    yes = _expert_ffn(xss, w_gate, w_up, w_down, layer, 512, 512)
    return [_combine(x1, m5, jnp.swapaxes(slot, 1, 2), jnp.swapaxes(aff_t, 1, 2), ye, min(512, x1.shape[1]))
            for (x1, _, aff_t, m5), slot, ye in zip(sets, slots, yes)]


def kernel(x, c, ctx, c_ctx, w_mod, b_mod, g_attn, g_ffn, w_in, ret_decay, diff_qn, diff_kn, diff_lambda,
           w_fnet_o, w_ret_o, w_diff_o, w_out, w_router, w_exp_gate, w_exp_up, w_exp_down):
    bsz, n, d = x.shape
    n_ctx = ctx.shape[1]
    depth = w_mod.shape[0]

    pad = (-(bsz + 1)) % 8
    cvecs = jnp.concatenate([c, c_ctx[None, :], jnp.zeros((pad, d), F32)], axis=0)
    mods = _modulation(cvecs, w_mod, b_mod)

    rope_lat = _rope_tables(n, True)
    rope_ctx = _rope_tables(n_ctx, False)
    cs_lat, cs_ctx = _dft_tables(n), _dft_tables(n_ctx)
    bdc, bds = _group_dft_tables(2 * LANES)
    jsh = _fold_matrix(LANES)
    gm = _group_mean_matrix(_W_DQ)
    zero_state = jnp.zeros((bsz, N_HEADS, HEAD_DIM, HEAD_V), F32)
    ret_chunk = 1024

    xc = ctx
    for layer in range(depth):
        last = layer == depth - 1
        lam_init = 0.8 - 0.6 * math.exp(-0.3 * layer)
        mx = [mods[layer, :bsz, j * d:(j + 1) * d].reshape(bsz, 1, d) for j in range(N_MOD)]
        mc = [mods[layer, bsz:bsz + 1, j * d:(j + 1) * d].reshape(1, 1, d) for j in range(N_MOD)]
        w_in_bf = w_in[layer].astype(BF)
        qg = jnp.tile(diff_qn[layer], _W_DQ // HEAD_DIM)[None, :]
        kg = jnp.tile(diff_kn[layer], _W_DK // HEAD_DIM)[None, :]
        g_a = g_attn[layer][None, :]
        g_f = g_ffn[layer][None, :]
        decays = ret_decay[layer]
        dlam = diff_lambda[layer]
        branch_w = (w_fnet_o[layer].astype(BF), w_ret_o[layer].astype(BF), w_diff_o[layer].astype(BF),
                    w_out[layer].astype(BF))
        wr32 = w_router[layer]
        wr_hi = wr32.astype(BF)
        wr_lo = (wr32 - wr_hi.astype(F32)).astype(BF)
        lane_pad = jnp.zeros((d, LANES - N_EXPERTS), BF)
        rcat = jnp.concatenate([wr_hi, lane_pad, wr_lo, lane_pad], axis=1)
        moe_sets = []

        fo_c, rq_c, rk_c, rv_c, sg_c, dq_c, dk_c, dv_c, gt_c = _mixer_inputs(xc, mc, g_a, w_in_bf, rope_ctx, qg, kg, gm)
        ro_c, s_f, s_b = _retention(rq_c, rk_c, rv_c, sg_c, decays, zero_state, zero_state, min(ret_chunk, n_ctx))
        if not last:
            f_c = _fourier_mix(fo_c, cs_ctx, bdc, bds, jsh, min(512, n_ctx))
            do_c = _diff_attention(dq_c, [dk_c], [dv_c], dlam, lam_init, min(256, n_ctx))
            rows_c = bsz * n_ctx
            x1_c, h2_c, aff_c = _merge(f_c.reshape(rows_c, -1), ro_c.reshape(rows_c, -1), do_c.reshape(rows_c, -1),
                                       gt_c.reshape(rows_c, -1), xc.reshape(rows_c, d), mc[2], mc[3], mc[4], g_f,
                                       *branch_w, rcat, n_ctx, min(512, n_ctx))
            moe_sets.append((x1_c.reshape(bsz, n_ctx, d), h2_c.reshape(bsz, n_ctx, d), aff_c, mc[5]))

        fo, rq, rk, rv, sg, dq, dk, dv, gt = _mixer_inputs(x, mx, g_a, w_in_bf, rope_lat, qg, kg, gm)
        ro, _, _ = _retention(rq, rk, rv, sg, decays, s_f, s_b, min(ret_chunk, n))
        f = _fourier_mix(fo, cs_lat, bdc, bds, jsh, min(512, n))
        do = _diff_attention(dq, [dk, dk_c], [dv, dv_c], dlam, lam_init, min(512, n))
        rows = bsz * n
        x1, h2, aff = _merge(f.reshape(rows, -1), ro.reshape(rows, -1), do.reshape(rows, -1), gt.reshape(rows, -1),
                             x.reshape(rows, d), mx[2], mx[3], mx[4], g_f, *branch_w, rcat, n, min(1024, n))
        moe_sets.append((x1.reshape(bsz, n, d), h2.reshape(bsz, n, d), aff, mx[5]))
        moe_out = _moe(moe_sets, w_exp_gate, w_exp_up, w_exp_down, layer)
        x = moe_out[-1]
        if not last:
            xc = moe_out[0]
    return x
```

```python
import functools
import math

import jax
import jax.numpy as jnp
import numpy as np
from jax import lax
from jax.experimental import pallas as pl
from jax.experimental.pallas import tpu as pltpu

F32 = jnp.float32
BF = jnp.bfloat16
I32 = jnp.int32

GRID_W = 64
HEAD_DIM = 64
ROPE_FREQS_PER_AXIS = HEAD_DIM // 4
ROPE_BASE = 10000.0
EPS = 1e-6
LOG2E = 1.4426950408889634
FNET_GROUP_DIM = 64
N_HEADS = 4
HEAD_V = 128
N_EXPERTS = 16
EC_FACTOR = 2
N_MOD = 6
RET_SUB = 256
MERGE_SUB = 1024
TOKEN_BLOCK = 256
SLOT_WINDOW = 64
LANES = 128
VMEM_LIMIT = 56 * 1024 * 1024

_W_FO, _W_RQ, _W_RK, _W_RV, _W_RG, _W_DQ, _W_DK, _W_DV = 512, 256, 256, 512, 512, 512, 512, 512


def _cparams(sem):
    return pltpu.CompilerParams(dimension_semantics=sem, vmem_limit_bytes=VMEM_LIMIT)


def _sigmoid(v):
    return 1.0 / (1.0 + jnp.exp(-v))


def _nt_dot(a, b):
    return lax.dot_general(a, b, (((1,), (1,)), ((), ())), preferred_element_type=F32)


def _tn_dot(a, b):
    return lax.dot_general(a, b, (((0,), (0,)), ((), ())), preferred_element_type=F32)


def _mod_kernel(c_ref, w_ref, b_ref, o_ref):
    cv = c_ref[...]
    s = cv * _sigmoid(cv)
    o_ref[0] = jnp.dot(s.astype(BF), w_ref[0].astype(BF), preferred_element_type=F32) + b_ref[0]


def _modulation(cvecs, w_mod, b_mod):
    depth, d, wd = w_mod.shape
    rows = cvecs.shape[0]
    tn = 1536
    return pl.pallas_call(
        _mod_kernel,
        grid=(depth, wd // tn),
        in_specs=[pl.BlockSpec((rows, d), lambda l, j: (0, 0)),
                  pl.BlockSpec((1, d, tn), lambda l, j: (l, 0, j)),
                  pl.BlockSpec((1, 1, tn), lambda l, j: (l, 0, j))],
        out_specs=pl.BlockSpec((1, rows, tn), lambda l, j: (l, 0, j)),
        out_shape=jax.ShapeDtypeStruct((depth, rows, wd), F32),
        compiler_params=_cparams(("arbitrary", "arbitrary")),
        name="modulation",
    )(cvecs, w_mod, b_mod.reshape(depth, 1, wd))


def _rope(x, cos_t, sin_t):
    lane = lax.broadcasted_iota(I32, (x.shape[0], LANES), 1)
    first = (lane & 63) < 32
    outs = []
    for j in range(x.shape[1] // LANES):
        xc = x[:, j * LANES:(j + 1) * LANES]
        sw = jnp.where(first, pltpu.roll(xc, LANES - 32, 1), pltpu.roll(xc, 32, 1))
        outs.append(xc * cos_t + sw * sin_t)
    return jnp.concatenate(outs, axis=1)


def _group_rms(x, gm, gain):
    ms = jnp.dot((x * x).astype(BF), gm, preferred_element_type=F32)
    return x * lax.rsqrt(ms + EPS) * gain


def _inproj_kernel(x_ref, shift_ref, scale_ref, g_ref, w_ref, cos_ref, sin_ref, qg_ref, kg_ref, gm_ref,
                   fo_ref, rq_ref, rk_ref, rv_ref, sg_ref, dq_ref, dk_ref, dv_ref, gt_ref):
    x = x_ref[...]
    d = x.shape[1]
    ms = jnp.mean(x * x, axis=-1, keepdims=True)
    h = x * lax.rsqrt(ms + EPS) * g_ref[...]
    h = h * (1.0 + scale_ref[0]) + shift_ref[0]
    hb = h.astype(BF)
    cos_t = cos_ref[...]
    sin_t = sin_ref[...]
    gm = gm_ref[...]

    def proj(a, width):
        return jnp.dot(hb, w_ref[:, a:a + width], preferred_element_type=F32)

    a = 0
    fo_ref[...] = proj(a, _W_FO).astype(BF)
    a += _W_FO
    rq_ref[...] = _rope(proj(a, _W_RQ) * (HEAD_DIM ** -0.5), cos_t, sin_t).astype(BF)
    a += _W_RQ
    rk_ref[...] = _rope(proj(a, _W_RK), cos_t, sin_t).astype(BF)
    a += _W_RK
    rv_ref[...] = proj(a, _W_RV).astype(BF)
    a += _W_RV
    rg = proj(a, _W_RG)
    sg_ref[...] = (rg * _sigmoid(rg)).astype(BF)
    a += _W_RG
    dq = _group_rms(proj(a, _W_DQ), gm, qg_ref[...])
    dq_ref[...] = (_rope(dq, cos_t, sin_t) * (HEAD_DIM ** -0.5 * LOG2E)).astype(BF)
    a += _W_DQ
    dk = _group_rms(proj(a, _W_DK), gm, kg_ref[...])
    dk_ref[...] = _rope(dk, cos_t, sin_t).astype(BF)
    a += _W_DK
    dv = proj(a, _W_DV).astype(BF)
    ones = jnp.ones((dv.shape[0], HEAD_V), BF)
    for hh in range(N_HEADS):
        dv_ref[:, 2 * hh * HEAD_V:(2 * hh + 1) * HEAD_V] = dv[:, hh * HEAD_V:(hh + 1) * HEAD_V]
        dv_ref[:, (2 * hh + 1) * HEAD_V:(2 * hh + 2) * HEAD_V] = ones
    a += _W_DV
    for j in range(3):
        gl = proj(a + j * d, d)
        gt_ref[:, j * d:(j + 1) * d] = _sigmoid(gl).astype(BF)


def _in_projection(x2d, shift, scale, g, w_in_bf, cos_t, sin_t, qg, kg, gm, n_seq, tm):
    rows, d = x2d.shape
    tiles_per_b = n_seq // tm
    nb = shift.shape[0]
    win = w_in_bf.shape[1]
    widths = (_W_FO, _W_RQ, _W_RK, _W_RV, _W_RG, _W_DQ, _W_DK, 2 * _W_DV, 3 * d)

    def mod_map(i):
        return ((i // tiles_per_b) if nb > 1 else 0, 0, 0)

    def pos_map(i):
        return (i % tiles_per_b, 0)

    const2 = lambda i: (0, 0)
    return pl.pallas_call(
        _inproj_kernel,
        grid=(rows // tm,),
        in_specs=[pl.BlockSpec((tm, d), lambda i: (i, 0)),
                  pl.BlockSpec((1, 1, d), mod_map),
                  pl.BlockSpec((1, 1, d), mod_map),
                  pl.BlockSpec((1, d), const2),
                  pl.BlockSpec((d, win), const2, pipeline_mode=pl.Buffered(1)),
                  pl.BlockSpec((tm, LANES), pos_map),
                  pl.BlockSpec((tm, LANES), pos_map),
                  pl.BlockSpec((1, _W_DQ), const2),
                  pl.BlockSpec((1, _W_DK), const2),
                  pl.BlockSpec((_W_DQ, _W_DQ), const2)],
        out_specs=[pl.BlockSpec((tm, w), lambda i: (i, 0)) for w in widths],
        out_shape=[jax.ShapeDtypeStruct((rows, w), BF) for w in widths],
        compiler_params=_cparams(("arbitrary",)),
        name="in_projection",
    )(x2d, shift, scale, g, w_in_bf, cos_t, sin_t, qg, kg, gm)


def _fourier_kernel(x_ref, cs_ref, bc_ref, bs_ref, jsh_ref, o_ref, z_ref, eo_ref):
    n, w = x_ref.shape[1], x_ref.shape[2]
    half = n // 2
    gw = bc_ref.shape[0]
    fb = jsh_ref.shape[0]
    tr = o_ref.shape[1]

    @pl.when(pl.program_id(1) == 0)
    def _():
        x = x_ref[0]
        for g in range(w // gw):
            xs = x[:, g * gw:(g + 1) * gw]
            z_ref[0, :, g * gw:(g + 1) * gw] = jnp.dot(xs, bc_ref[...], preferred_element_type=F32).astype(BF)
            z_ref[1, :, g * gw:(g + 1) * gw] = jnp.dot(xs, bs_ref[...], preferred_element_type=F32).astype(BF)
        jsh = jsh_ref[...]
        n_blocks = half // fb
        for blk in range(n_blocks):
            hi_blk = 2 * n_blocks - 1 - blk
            for t in range(2):
                top = z_ref[t, hi_blk * fb:(hi_blk + 1) * fb, :]
                nxt = jnp.zeros_like(top) if blk == 0 else z_ref[t, (hi_blk + 1) * fb:(hi_blk + 2) * fb, :]
                partner = jnp.dot(jsh, jnp.concatenate([top, nxt], axis=0), preferred_element_type=F32)
                own = z_ref[t, blk * fb:(blk + 1) * fb, :].astype(F32)
                folded = own + partner if t == 0 else own - partner
                eo_ref[t * half + blk * fb:t * half + (blk + 1) * fb, :] = folded.astype(BF)

    y = jnp.dot(cs_ref[...], eo_ref[...], preferred_element_type=F32)
    parity = lax.broadcasted_iota(I32, (tr, 1), 0) & 1
    sign = (1.0 - 2.0 * parity.astype(F32)) * (1.0 / math.sqrt(n))
    o_ref[0] = (y + sign * z_ref[0, half:half + 1, :].astype(F32)).astype(BF)


def _fourier_mix(fo, cs, bdc, bds, jsh, tr):
    b, n, w = fo.shape
    assert n % (2 * jsh.shape[0]) == 0 and tr % 2 == 0
    return pl.pallas_call(
        _fourier_kernel,
        grid=(b, n // tr),
        in_specs=[pl.BlockSpec((1, n, w), lambda i, j: (i, 0, 0)),
                  pl.BlockSpec((tr, n), lambda i, j: (j, 0)),
                  pl.BlockSpec(bdc.shape, lambda i, j: (0, 0)),
                  pl.BlockSpec(bds.shape, lambda i, j: (0, 0)),
                  pl.BlockSpec(jsh.shape, lambda i, j: (0, 0))],
        out_specs=pl.BlockSpec((1, tr, w), lambda i, j: (i, j, 0)),
        out_shape=jax.ShapeDtypeStruct((b, n, w), BF),
        scratch_shapes=[pltpu.VMEM((2, n, w), BF), pltpu.VMEM((n, w), BF)],
        compiler_params=_cparams(("arbitrary", "arbitrary")),
        name="fourier_mix",
    )(fo, cs, bdc, bds, jsh)


def _ret_kernel(lg_ref, q_ref, k_ref, v_ref, sg_ref, sf_ref, sb_ref, o_ref, sfo_ref, sbo_ref,
                acc_ref, tmp_ref, st_ref, dm_ref, qd_ref, qm_ref, kd_ref, cd_ref, *, chunk, n_chunks, sub):
    b = pl.program_id(0)
    p = pl.program_id(1)
    c = pl.program_id(2)
    sf = float(sub)
    n_sub = chunk // sub

    @pl.when((b == 0) & (p == 0) & (c == 0))
    def _tables():
        ii = lax.broadcasted_iota(I32, (sub, sub), 0).astype(F32)
        jj = lax.broadcasted_iota(I32, (sub, sub), 1).astype(F32)
        pos = lax.broadcasted_iota(I32, (sub, LANES), 0).astype(F32)
        lane = lax.broadcasted_iota(I32, (sub, LANES), 1)
        for dr in range(2):
            if dr == 0:
                rel, keep = ii - jj, ii >= jj
                qe, ke = pos + 1.0, (sf - 1.0) - pos
            else:
                rel, keep = jj - ii, jj > ii
                qe, ke = sf - pos, pos
            for hp in range(N_HEADS // 2):
                lg0 = jnp.full((sub, LANES), lg_ref[dr, 2 * hp], F32)
                lg1 = jnp.full((sub, LANES), lg_ref[dr, 2 * hp + 1], F32)
                lgp = jnp.where(lane < HEAD_DIM, lg0, lg1)
                kd_ref[dr, hp] = jnp.exp(lgp * ke)
            for h in range(N_HEADS):
                lgs = lg_ref[dr, h]
                in_head = (lane >= (h % 2) * HEAD_DIM) & (lane < (h % 2 + 1) * HEAD_DIM)
                dm_ref[dr, h] = jnp.where(keep, jnp.exp(jnp.full((sub, sub), lgs, F32) * jnp.maximum(rel, 0.0)), 0.0)
                qd_ref[dr, h] = jnp.where(in_head, jnp.exp(jnp.full((sub, LANES), lgs, F32) * qe), 0.0)
                cd_ref[dr, h] = jnp.exp(jnp.full((LANES, LANES), lgs, F32) * sf)
        for h in range(N_HEADS):
            in_head = (lane >= (h % 2) * HEAD_DIM) & (lane < (h % 2 + 1) * HEAD_DIM)
            qm_ref[h] = jnp.where(in_head, 1.0, 0.0)

    def load_state(src_ref):
        z = jnp.zeros((HEAD_DIM, HEAD_V), F32)
        for h in range(N_HEADS):
            s = src_ref[0, h]
            st_ref[h] = jnp.concatenate([s, z], axis=0) if h % 2 == 0 else jnp.concatenate([z, s], axis=0)

    @pl.when((c == 0) & (p == 0))
    def _():
        load_state(sf_ref)

    @pl.when((c == 0) & (p == 1))
    def _():
        load_state(sb_ref)

    sts = [st_ref[h] for h in range(N_HEADS)]
    for u in range(n_sub):
        r0 = pl.multiple_of(jnp.where(p == 0, u, n_sub - 1 - u) * sub, sub)
        rows = pl.ds(r0, sub)
        q = q_ref[0, rows, :]
        k = k_ref[0, rows, :]
        v = v_ref[0, rows, :]
        outs = []
        for hp in range(N_HEADS // 2):
            qp = q[:, hp * LANES:(hp + 1) * LANES].astype(F32)
            kp = k[:, hp * LANES:(hp + 1) * LANES]
            kdp = (kp.astype(F32) * kd_ref[p, hp]).astype(BF)
            for h in (2 * hp, 2 * hp + 1):
                vh = v[:, h * HEAD_V:(h + 1) * HEAD_V]
                qm = (qp * qm_ref[h]).astype(BF)
                qdq = (qp * qd_ref[p, h]).astype(BF)
                s = _nt_dot(qm, kp) * dm_ref[p, h]
                o = (jnp.dot(s.astype(BF), vh, preferred_element_type=F32)
                     + jnp.dot(qdq, sts[h].astype(BF), preferred_element_type=F32))
                sts[h] = sts[h] * cd_ref[p, h] + _tn_dot(kdp, vh)
                outs.append(o)
        tmp_ref[rows, :] = jnp.concatenate(outs, axis=1)
    for h in range(N_HEADS):
        st_ref[h] = sts[h]
    idx = jnp.where(p == 0, c, n_chunks - 1 - c)

    @pl.when(p == 0)
    def _():
        acc_ref[idx] = tmp_ref[...]

    @pl.when(p == 1)
    def _():
        tot = acc_ref[idx] + tmp_ref[...]
        sg = sg_ref[0].astype(F32)
        ys = []
        for h in range(N_HEADS):
            oh = tot[:, h * HEAD_V:(h + 1) * HEAD_V]
            mu = jnp.mean(oh, axis=-1, keepdims=True)
            dlt = oh - mu
            var = jnp.mean(dlt * dlt, axis=-1, keepdims=True)
            ys.append(dlt * lax.rsqrt(var + EPS))
        o_ref[0] = (jnp.concatenate(ys, axis=1) * sg).astype(BF)

    def store_state(dst_ref):
        for h in range(N_HEADS):
            r0 = (h % 2) * HEAD_DIM
            dst_ref[0, h] = st_ref[h, r0:r0 + HEAD_DIM, :]

    @pl.when((c == n_chunks - 1) & (p == 0))
    def _():
        store_state(sfo_ref)

    @pl.when((c == n_chunks - 1) & (p == 1))
    def _():
        store_state(sbo_ref)


def _retention(rq, rk, rv, sg, decays, s_f, s_b, chunk):
    b, n, _ = rq.shape
    n_chunks = n // chunk
    wv = rv.shape[2]

    def seq_map(i, p, c):
        return (i, jnp.where(p == 0, c, n_chunks - 1 - c), 0)

    def out_map(i, p, c):
        return (i, jnp.where(p == 0, n_chunks - 1, n_chunks - 1 - c), 0)

    st_spec = pl.BlockSpec((1, N_HEADS, HEAD_DIM, HEAD_V), lambda i, p, c: (i, 0, 0, 0))
    st_shape = jax.ShapeDtypeStruct((b, N_HEADS, HEAD_DIM, HEAD_V), F32)
    sub = min(RET_SUB, chunk)
    kern = functools.partial(_ret_kernel, chunk=chunk, n_chunks=n_chunks, sub=sub)
    return pl.pallas_call(
        kern,
        grid=(b, 2, n_chunks),
        in_specs=[pl.BlockSpec(memory_space=pltpu.SMEM),
                  pl.BlockSpec((1, chunk, rq.shape[2]), seq_map),
                  pl.BlockSpec((1, chunk, rk.shape[2]), seq_map),
                  pl.BlockSpec((1, chunk, wv), seq_map),
                  pl.BlockSpec((1, chunk, wv), seq_map),
                  st_spec, st_spec],
        out_specs=[pl.BlockSpec((1, chunk, wv), out_map), st_spec, st_spec],
        out_shape=[jax.ShapeDtypeStruct((b, n, wv), BF), st_shape, st_shape],
        scratch_shapes=[pltpu.VMEM((n_chunks, chunk, wv), F32),
                        pltpu.VMEM((chunk, wv), F32),
                        pltpu.VMEM((N_HEADS, LANES, HEAD_V), F32),
                        pltpu.VMEM((2, N_HEADS, sub, sub), F32),
                        pltpu.VMEM((2, N_HEADS, sub, LANES), F32),
                        pltpu.VMEM((N_HEADS, sub, LANES), F32),
                        pltpu.VMEM((2, N_HEADS // 2, sub, LANES), F32),
                        pltpu.VMEM((2, N_HEADS, LANES, HEAD_V), F32)],
        compiler_params=_cparams(("arbitrary", "arbitrary", "arbitrary")),
        name="retention",
    )(decays, rq, rk, rv, sg, s_f, s_b)


def _dattn_kernel(dl_ref, q_ref, *refs, n_src, lam_init):
    k_refs = refs[:n_src]
    v_refs = refs[n_src:2 * n_src]
    o_ref = refs[2 * n_src]
    dl = dl_ref[...]
    lam = (jnp.exp(jnp.sum(dl[0:1] * dl[1:2], axis=-1, keepdims=True))
           - jnp.exp(jnp.sum(dl[2:3] * dl[3:4], axis=-1, keepdims=True)) + lam_init)
    q = q_ref[0].astype(F32)
    tq = q.shape[0]
    lane = lax.broadcasted_iota(I32, (tq, LANES), 1)
    ys = []
    for h in range(N_HEADS):
        qh = q[:, h * LANES:(h + 1) * LANES]
        qs = jnp.concatenate([jnp.where(lane < HEAD_DIM, qh, 0.0), jnp.where(lane >= HEAD_DIM, qh, 0.0)],
                             axis=0).astype(BF)
        ss = [_nt_dot(qs, kr[0, :, h * LANES:(h + 1) * LANES]) for kr in k_refs]
        m = ss[0].max(axis=-1, keepdims=True)
        for s in ss[1:]:
            m = jnp.maximum(m, s.max(axis=-1, keepdims=True))
        acc = jnp.zeros((2 * tq, 2 * HEAD_V), F32)
        for s, vr in zip(ss, v_refs):
            pexp = jnp.exp2(s - m).astype(BF)
            acc = acc + jnp.dot(pexp, vr[0, :, 2 * h * HEAD_V:(2 * h + 2) * HEAD_V], preferred_element_type=F32)
        o = acc[:, :HEAD_V] / acc[:, HEAD_V:]
        oh = o[:tq] - lam * o[tq:]
        ms = jnp.mean(oh * oh, axis=-1, keepdims=True)
        ys.append(oh * lax.rsqrt(ms + EPS) * (1.0 - lam_init))
    o_ref[0] = jnp.concatenate(ys, axis=1).astype(BF)


def _diff_attention(dq, ks, vs, dlam, lam_init, tq):
    b, n, w = dq.shape
    n_src = len(ks)
    kern = functools.partial(_dattn_kernel, n_src=n_src, lam_init=lam_init)
    kv_specs = [pl.BlockSpec((1, a.shape[1], a.shape[2]), lambda i, j: (i, 0, 0)) for a in (*ks, *vs)]
    return pl.pallas_call(
        kern,
        grid=(b, n // tq),
        in_specs=[pl.BlockSpec(dlam.shape, lambda i, j: (0, 0)),
                  pl.BlockSpec((1, tq, w), lambda i, j: (i, j, 0))] + kv_specs,
        out_specs=pl.BlockSpec((1, tq, w), lambda i, j: (i, j, 0)),
        out_shape=jax.ShapeDtypeStruct((b, n, w), BF),
        compiler_params=_cparams(("arbitrary", "arbitrary")),
        name="diff_attention",
    )(dlam, dq, *ks, *vs)


def _merge_kernel(f_ref, ro_ref, do_ref, gt_ref, x_ref, m2_ref, m3_ref, m4_ref, g_ref,
                  wf_ref, wr_ref, wd_ref, wo_ref, rcat_ref, x1_ref, h2_ref, aff_ref):
    tm, d = x_ref.shape
    st = min(MERGE_SUB, tm)
    for r0 in range(0, tm, st):
        rows = slice(r0, r0 + st)
        t = (gt_ref[rows, 0:d].astype(F32) * jnp.dot(f_ref[rows, :], wf_ref[...], preferred_element_type=F32)
             + gt_ref[rows, d:2 * d].astype(F32) * jnp.dot(ro_ref[rows, :], wr_ref[...], preferred_element_type=F32)
             + gt_ref[rows, 2 * d:3 * d].astype(F32) * jnp.dot(do_ref[rows, :], wd_ref[...],
                                                                preferred_element_type=F32))
        mix = jnp.dot(t.astype(BF), wo_ref[...], preferred_element_type=F32)
        x1 = x_ref[rows, :] + m2_ref[0] * mix
        x1_ref[rows, :] = x1
        ms = jnp.mean(x1 * x1, axis=-1, keepdims=True)
        h2 = x1 * lax.rsqrt(ms + EPS) * g_ref[...]
        h2 = h2 * (1.0 + m4_ref[0]) + m3_ref[0]
        hi = h2.astype(BF)
        lo = (h2 - hi.astype(F32)).astype(BF)
        h2_ref[rows, :] = hi
        l1 = jnp.dot(hi, rcat_ref[...], preferred_element_type=F32)
        lt = l1[:, :LANES] + l1[:, LANES:] + jnp.dot(lo, rcat_ref[:, :LANES], preferred_element_type=F32)
        lane = lax.broadcasted_iota(I32, lt.shape, 1)
        lt = jnp.where(lane < N_EXPERTS, lt, -jnp.inf)
        ex = jnp.exp(lt - lt.max(axis=-1, keepdims=True))
        aff = ex / jnp.sum(ex, axis=-1, keepdims=True)
        aff_ref[0, :, rows] = aff.T[:N_EXPERTS, :]


def _merge(f, ro, do, gt, x2d, m2, m3, m4, g, wf, wr, wd, wo, rcat, n_seq, tm):
    rows, d = x2d.shape
    tiles_per_b = n_seq // tm
    nb = m2.shape[0]
    bsz = rows // n_seq
    ne = N_EXPERTS

    def mod_map(i):
        return ((i // tiles_per_b) if nb > 1 else 0, 0, 0)

    const2 = lambda i: (0, 0)
    row_spec = lambda w: pl.BlockSpec((tm, w), lambda i: (i, 0))
    return pl.pallas_call(
        _merge_kernel,
        grid=(rows // tm,),
        in_specs=[row_spec(f.shape[1]), row_spec(ro.shape[1]), row_spec(do.shape[1]), row_spec(gt.shape[1]),
                  row_spec(d),
                  pl.BlockSpec((1, 1, d), mod_map), pl.BlockSpec((1, 1, d), mod_map), pl.BlockSpec((1, 1, d), mod_map),
                  pl.BlockSpec((1, d), const2),
                  pl.BlockSpec(wf.shape, const2), pl.BlockSpec(wr.shape, const2), pl.BlockSpec(wd.shape, const2),
                  pl.BlockSpec(wo.shape, const2), pl.BlockSpec(rcat.shape, const2)],
        out_specs=[row_spec(d), row_spec(d),
                   pl.BlockSpec((1, ne, tm), lambda i: (i // tiles_per_b, 0, i % tiles_per_b))],
        out_shape=[jax.ShapeDtypeStruct((rows, d), F32), jax.ShapeDtypeStruct((rows, d), BF),
                   jax.ShapeDtypeStruct((bsz, ne, n_seq), F32)],
        compiler_params=_cparams(("arbitrary",)),
        name="merge_router",
    )(f, ro, do, gt, x2d, m2, m3, m4, g, wf, wr, wd, wo, rcat)


def _route_kernel(a_ref, slot_ref, offs_ref, *, cap, blk):
    a = a_ref[...]
    ne, n = a.shape
    capf = float(cap)

    def enough(t):
        return jnp.sum(jnp.where(a >= t, 1.0, 0.0), axis=-1, keepdims=True) >= capf

    tiny = jnp.full((ne, 1), 2.0 ** -126, F32)
    found = enough(tiny)
    cur = tiny
    for step in (64, 32, 16, 8, 4, 2, 1):
        cand = cur * (2.0 ** step)
        cur = jnp.where(enough(cand), cand, cur)
    base = cur

    def mantissa_bit(_, carry):
        cur, stepv = carry
        cand = cur + stepv
        return jnp.where(enough(cand), cand, cur), stepv * 0.5

    cur, ulp = lax.fori_loop(0, 23, mantissa_bit, (cur, base * 0.5))
    lo = jnp.where(found, cur, 0.0)
    hi = jnp.where(found, cur + ulp * 2.0, tiny)

    def refine(_, carry):
        lo, hi = carry
        mid = lo + (hi - lo) * 0.5
        ok = enough(mid)
        return jnp.where(ok, mid, lo), jnp.where(ok, hi, mid)

    lo, hi = lax.fori_loop(0, 24, refine, (lo, hi))
    ri = lax.broadcasted_iota(I32, (blk, blk), 0)
    ci = lax.broadcasted_iota(I32, (blk, blk), 1)
    upper = jnp.where(ri < ci, 1.0, 0.0).astype(BF)

    def excl_cumsum(m):
        carry = jnp.zeros((ne, 1), F32)
        outs = []
        for j in range(n // blk):
            mb = m[:, j * blk:(j + 1) * blk]
            outs.append(jnp.dot(mb.astype(BF), upper, preferred_element_type=F32) + carry)
            carry = carry + jnp.sum(mb, axis=-1, keepdims=True)
        return jnp.concatenate(outs, axis=1)

    gt = a >= hi
    tie = (a >= lo) & (a < hi)
    need = capf - jnp.sum(jnp.where(gt, 1.0, 0.0), axis=-1, keepdims=True)
    sel = gt | (tie & (excl_cumsum(jnp.where(tie, 1.0, 0.0)) < need))
    selm = jnp.where(sel, 1.0, 0.0)
    slot = excl_cumsum(selm)
    slot_ref[...] = jnp.where(sel, slot.astype(I32), -1)
    tok = lax.broadcasted_iota(I32, (n, LANES), 0)
    blk_start = lax.broadcasted_iota(I32, (n, LANES), 1) * blk
    before = jnp.where((tok < blk_start) & (blk_start <= n), 1.0, 0.0).astype(BF)
    offs_ref[...] = jnp.dot(selm.astype(BF), before, preferred_element_type=F32).astype(I32)


def _route(aff_t, cap):
    b, ne, n = aff_t.shape
    kern = functools.partial(_route_kernel, cap=cap, blk=min(TOKEN_BLOCK, n))
    slot, offs = pl.pallas_call(
        kern,
        grid=(1,),
        in_specs=[pl.BlockSpec((b * ne, n), lambda i: (0, 0))],
        out_specs=[pl.BlockSpec((b * ne, n), lambda i: (0, 0)), pl.BlockSpec((b * ne, LANES), lambda i: (0, 0))],
        out_shape=[jax.ShapeDtypeStruct((b * ne, n), I32), jax.ShapeDtypeStruct((b * ne, LANES), I32)],
        compiler_params=_cparams(("arbitrary",)),
        name="route",
    )(aff_t.reshape(b * ne, n))
    return slot.reshape(b, ne, n), offs


def _window_start(off, cap, win):
    return jnp.minimum(lax.shift_left(lax.shift_right_logical(off, 4), 4), cap - win)


def _windows_fit(offs_ref, row0, ne, blk0, nblk, cap, win):
    bad = jnp.int32(0)
    for e in range(ne):
        for jj in range(nblk):
            j = blk0 + jj
            off = offs_ref[row0 + e, j]
            end = offs_ref[row0 + e, j + 1]
            bad = bad | (end - _window_start(off, cap, win) > win).astype(I32)
    return bad == 0


def _gather_kernel(offs_ref, slot_ref, h_ref, xs_ref, *, cap, win, tb):
    ne, n = slot_ref.shape[1], slot_ref.shape[3]
    nblk = n // tb
    row0 = pl.program_id(0) * ne
    fits = _windows_fit(offs_ref, row0, ne, 0, nblk, cap, win)

    @pl.when(fits)
    def _windowed():
        xs_ref[...] = jnp.zeros(xs_ref.shape, BF)
        rows = lax.broadcasted_iota(I32, (win, tb), 0)
        for j in range(nblk):
            starts, lhs = [], []
            for e in range(ne):
                w0 = pl.multiple_of(_window_start(offs_ref[row0 + e, j], cap, win), 16)
                starts.append(w0)
                lhs.append(jnp.where(slot_ref[0, e, :, j * tb:(j + 1) * tb] == rows + w0, 1.0, 0.0).astype(BF))
            got = jnp.dot(jnp.concatenate(lhs, axis=0), h_ref[0, j * tb:(j + 1) * tb, :],
                          preferred_element_type=F32)
            for e in range(ne):
                dst = xs_ref.at[0, e, pl.ds(starts[e], win), :]
                dst[...] = (dst[...].astype(F32) + got[e * win:(e + 1) * win]).astype(BF)

    @pl.when(jnp.logical_not(fits))
    def _dense():
        sub = lax.broadcasted_iota(I32, (cap, n), 0)
        hb = h_ref[0]
        for e in range(ne):
            onehot = jnp.where(sub == slot_ref[0, e], 1.0, 0.0).astype(BF)
            xs_ref[0, e] = jnp.dot(onehot, hb, preferred_element_type=F32).astype(BF)


def _gather(slot, offs, h2, cap):
    b, ne, n = slot.shape
    d = h2.shape[2]
    tb = min(TOKEN_BLOCK, n)
    kern = functools.partial(_gather_kernel, cap=cap, win=min(SLOT_WINDOW, cap), tb=tb)
    grid_spec = pltpu.PrefetchScalarGridSpec(
        num_scalar_prefetch=1,
        grid=(b,),
        in_specs=[pl.BlockSpec((1, ne, 1, n), lambda i, offs: (i, 0, 0, 0)),
                  pl.BlockSpec((1, n, d), lambda i, offs: (i, 0, 0))],
        out_specs=pl.BlockSpec((1, ne, cap, d), lambda i, offs: (i, 0, 0, 0)))
    return pl.pallas_call(
        kern,
        grid_spec=grid_spec,
        out_shape=jax.ShapeDtypeStruct((b, ne, cap, d), BF),
        compiler_params=_cparams(("arbitrary",)),
        name="moe_gather",
    )(offs, slot.reshape(b, ne, 1, n), h2)


def _ffn_kernel(*refs, n_src, n_fc):
    xs_refs = refs[:n_src]
    wg_ref, wu_ref, wd_ref = refs[n_src:n_src + 3]
    ye_refs = refs[n_src + 3:2 * n_src + 3]
    x_ref, hid_ref = refs[2 * n_src + 3:]
    j = pl.program_id(1)
    fc = wg_ref.shape[3]
    row_spans = []
    r0 = 0
    for xr in xs_refs:
        rows = xr.shape[0] * xr.shape[2]
        row_spans.append((r0, rows))
        r0 += rows

    @pl.when(j == 0)
    def _():
        for xr, (s0, rows) in zip(xs_refs, row_spans):
            x_ref[s0:s0 + rows, :] = xr[:, 0].reshape(rows, xr.shape[3])

    @pl.when(j < n_fc)
    def _():
        x = x_ref[...]
        g = jnp.dot(x, wg_ref[0, 0].astype(BF), preferred_element_type=F32)
        u = jnp.dot(x, wu_ref[0, 0].astype(BF), preferred_element_type=F32)
        hid_ref[j] = (g * _sigmoid(g) * u).astype(BF)

    @pl.when(j >= n_fc)
    def _():
        y = jnp.dot(hid_ref[0], wd_ref[0, 0, 0:fc, :].astype(BF), preferred_element_type=F32)
        for k in range(1, n_fc):
            y = y + jnp.dot(hid_ref[k], wd_ref[0, 0, k * fc:(k + 1) * fc, :].astype(BF), preferred_element_type=F32)
        for yr, (s0, rows) in zip(ye_refs, row_spans):
            yr[:, 0] = y[s0:s0 + rows].reshape(yr.shape[0], yr.shape[2], yr.shape[3]).astype(BF)


def _expert_ffn(xs_list, w_gate, w_up, w_down, layer, fchunk, ochunk):
    ne, d = xs_list[0].shape[1], xs_list[0].shape[3]
    f = w_gate.shape[3]
    n_fc, n_oc = f // fchunk, d // ochunk
    n_src = len(xs_list)
    total_rows = sum(x.shape[0] * x.shape[2] for x in xs_list)
    kern = functools.partial(_ffn_kernel, n_src=n_src, n_fc=n_fc)
    up_map = lambda e, j: (layer, e, 0, jnp.minimum(j, n_fc - 1))
    down_map = lambda e, j: (layer, e, 0, jnp.maximum(j - n_fc, 0))
    out_map = lambda e, j: (0, e, 0, jnp.maximum(j - n_fc, 0))
    outs = pl.pallas_call(
        kern,
        grid=(ne, n_fc + n_oc),
        in_specs=[pl.BlockSpec((x.shape[0], 1, x.shape[2], d), lambda e, j: (0, e, 0, 0), pipeline_mode=pl.Buffered(1))
                  for x in xs_list]
        + [pl.BlockSpec((1, 1, d, fchunk), up_map),
           pl.BlockSpec((1, 1, d, fchunk), up_map),
           pl.BlockSpec((1, 1, f, ochunk), down_map)],
        out_specs=[pl.BlockSpec((x.shape[0], 1, x.shape[2], ochunk), out_map) for x in xs_list],
        out_shape=[jax.ShapeDtypeStruct(x.shape, BF) for x in xs_list],
        scratch_shapes=[pltpu.VMEM((total_rows, d), BF), pltpu.VMEM((n_fc, total_rows, fchunk), BF)],
        compiler_params=_cparams(("arbitrary", "arbitrary")),
        name="expert_ffn",
    )(*xs_list, w_gate, w_up, w_down)
    return list(outs)


def _combine_kernel(offs_ref, x1_ref, m5_ref, slot_ref, aff_ref, ye_ref, o_ref, ywin_ref, *, cap, win, tb):
    tn, ne = slot_ref.shape[1], slot_ref.shape[2]
    row0 = pl.program_id(0) * ne
    blk0 = pl.program_id(1) * (tn // tb)
    fits = _windows_fit(offs_ref, row0, ne, blk0, tn // tb, cap, win)

    @pl.when(fits)
    def _windowed():
        per_piece = LANES // win
        lane = lax.broadcasted_iota(I32, (tb, LANES), 1)
        for jj in range(tn // tb):
            tok = slice(jj * tb, (jj + 1) * tb)
            st = slot_ref[0, tok, :]
            at = aff_ref[0, tok, :]
            pieces = []
            for p0 in range(0, ne, per_piece):
                sv = tg = wv = None
                for g in range(per_piece):
                    e = p0 + g
                    w0 = pl.multiple_of(_window_start(offs_ref[row0 + e, blk0 + jj], cap, win), 16)
                    ywin_ref[e * win:(e + 1) * win, :] = ye_ref[0, e, pl.ds(w0, win), :]
                    s_e, a_e, t_e = st[:, e:e + 1], at[:, e:e + 1], lane + (w0 - g * win)
                    if g == 0:
                        sv = jnp.broadcast_to(s_e, lane.shape)
                        wv = jnp.broadcast_to(a_e, lane.shape)
                        tg = t_e
                    else:
                        here = lane >= g * win
                        sv, wv, tg = jnp.where(here, s_e, sv), jnp.where(here, a_e, wv), jnp.where(here, t_e, tg)
                pieces.append(jnp.where(sv == tg, wv, 0.0).astype(BF))
            acc = jnp.dot(jnp.concatenate(pieces, axis=1), ywin_ref[...], preferred_element_type=F32)
            o_ref[0, tok, :] = x1_ref[0, tok, :] + m5_ref[0] * acc

    @pl.when(jnp.logical_not(fits))
    def _dense():
        st = slot_ref[0]
        at = aff_ref[0]
        lane = lax.broadcasted_iota(I32, (tn, cap), 1)
        acc = jnp.zeros(x1_ref.shape[1:], F32)
        for e in range(ne):
            onehot = jnp.where(st[:, e:e + 1] == lane, 1.0, 0.0).astype(BF)
            acc = acc + jnp.dot(onehot, ye_ref[0, e], preferred_element_type=F32) * at[:, e:e + 1]
        o_ref[0] = x1_ref[0] + m5_ref[0] * acc


def _combine(x1, m5, slot_tok, aff_tok, offs, ye, tn):
    b, n, d = x1.shape
    ne, cap = ye.shape[1], ye.shape[2]
    nb = m5.shape[0]
    tb = min(TOKEN_BLOCK, n)
    win = min(SLOT_WINDOW, cap)
    kern = functools.partial(_combine_kernel, cap=cap, win=win, tb=tb)
    grid_spec = pltpu.PrefetchScalarGridSpec(
        num_scalar_prefetch=1,
        grid=(b, n // tn),
        in_specs=[pl.BlockSpec((1, tn, d), lambda i, j, offs: (i, j, 0)),
                  pl.BlockSpec((1, 1, d), lambda i, j, offs: (i if nb > 1 else 0, 0, 0)),
                  pl.BlockSpec((1, tn, ne), lambda i, j, offs: (i, j, 0)),
                  pl.BlockSpec((1, tn, ne), lambda i, j, offs: (i, j, 0)),
                  pl.BlockSpec((1, ne, cap, d), lambda i, j, offs: (i, 0, 0, 0))],
        out_specs=pl.BlockSpec((1, tn, d), lambda i, j, offs: (i, j, 0)),
        scratch_shapes=[pltpu.VMEM((ne * win, d), BF)])
    return pl.pallas_call(
        kern,
        grid_spec=grid_spec,
        out_shape=jax.ShapeDtypeStruct((b, n, d), F32),
        compiler_params=_cparams(("arbitrary", "arbitrary")),
        name="moe_combine",
    )(offs, x1, m5, slot_tok, aff_tok, ye)


def _rope_tables(n, use_rope):
    if not use_rope:
        return jnp.ones((n, LANES), F32), jnp.zeros((n, LANES), F32)
    rows = n // GRID_W
    row = jnp.repeat(jnp.arange(rows, dtype=F32), GRID_W)
    col = jnp.tile(jnp.arange(GRID_W, dtype=F32), rows)
    inv = ROPE_BASE ** (-jnp.arange(ROPE_FREQS_PER_AXIS, dtype=F32) / ROPE_FREQS_PER_AXIS)
    ang = jnp.concatenate([row[:, None] * inv, col[:, None] * inv], axis=-1)
    cos, sin = jnp.cos(ang), jnp.sin(ang)
    cos_t = jnp.tile(cos, (1, LANES // (HEAD_DIM // 2)))
    sin_t = jnp.tile(jnp.concatenate([-sin, sin], axis=-1), (1, LANES // HEAD_DIM))
    return cos_t, sin_t


def _dft_tables(n):
    n0 = 64
    n1 = n // n0
    k = np.arange(n, dtype=np.int64)[:, None]
    ang1 = jnp.asarray(2.0 * np.pi * ((k * np.arange(n1 // 2)[None, :]) % n1) / n1, F32)
    ang0 = jnp.asarray(2.0 * np.pi * ((k * np.arange(n0)[None, :]) % n) / n, F32)
    c1, s1, c0, s0 = jnp.cos(ang1), jnp.sin(ang1), jnp.cos(ang0), jnp.sin(ang0)
    scale = 1.0 / math.sqrt(n)
    cos_n = (c1[:, :, None] * c0[:, None, :] - s1[:, :, None] * s0[:, None, :]).reshape(n, n // 2) * scale
    sin_n = (s1[:, :, None] * c0[:, None, :] + c1[:, :, None] * s0[:, None, :]).reshape(n, n // 2) * scale
    return jnp.concatenate([cos_n, -sin_n], axis=1).astype(BF)


def _fold_matrix(fb):
    i = np.arange(fb)[:, None]
    s = np.arange(2 * fb)[None, :]
    return jnp.asarray(np.where(s == fb - i, 1.0, 0.0), BF)


def _group_dft_tables(width):
    g = FNET_GROUP_DIM
    idx = np.arange(width)
    same = (idx[:, None] // g) == (idx[None, :] // g)
    ang = 2.0 * np.pi * (((idx[:, None] % g) * (idx[None, :] % g)) % g) / g
    scale = 1.0 / math.sqrt(g)
    bdc = np.where(same, np.cos(ang), 0.0) * scale
    bds = np.where(same, np.sin(ang), 0.0) * scale
    return jnp.asarray(bdc, F32).astype(BF), jnp.asarray(bds, F32).astype(BF)


def _group_mean_matrix(width):
    idx = np.arange(width)
    same = (idx[:, None] // HEAD_DIM) == (idx[None, :] // HEAD_DIM)
    return jnp.asarray(np.where(same, 1.0 / HEAD_DIM, 0.0), BF)


def _mixer_inputs(x3, mods, g_attn_l, w_in_bf, tables, qg, kg, gm):
    b, n, d = x3.shape
    tm = min(512, n)
    outs = _in_projection(x3.reshape(b * n, d), mods[0], mods[1], g_attn_l, w_in_bf, tables[0], tables[1],
                          qg, kg, gm, n, tm)
    return [o.reshape(b, n, o.shape[1]) for o in outs]


def _moe(sets, w_gate, w_up, w_down, layer):
    slots, offss, xss = [], [], []
    for x1, h2, aff_t, _ in sets:
        cap = EC_FACTOR * x1.shape[1] // N_EXPERTS
        slot, offs = _route(aff_t, cap)
        offs = offs[:, :16]
        slots.append(slot)
        offss.append(offs)
        xss.append(_gather(slot, offs, h2, cap))
    yes = _expert_ffn(xss, w_gate, w_up, w_down, layer, 512, 512)
    return [_combine(x1, m5, jnp.swapaxes(slot, 1, 2), jnp.swapaxes(aff_t, 1, 2), offs, ye, min(512, x1.shape[1]))
            for (x1, _, aff_t, m5), slot, offs, ye in zip(sets, slots, offss, yes)]


def kernel(x, c, ctx, c_ctx, w_mod, b_mod, g_attn, g_ffn, w_in, ret_decay, diff_qn, diff_kn, diff_lambda,
           w_fnet_o, w_ret_o, w_diff_o, w_out, w_router, w_exp_gate, w_exp_up, w_exp_down):
    bsz, n, d = x.shape
    n_ctx = ctx.shape[1]
    depth = w_mod.shape[0]

    pad = (-(bsz + 1)) % 8
    cvecs = jnp.concatenate([c, c_ctx[None, :], jnp.zeros((pad, d), F32)], axis=0)
    mods = _modulation(cvecs, w_mod, b_mod)

    rope_lat = _rope_tables(n, True)
    rope_ctx = _rope_tables(n_ctx, False)
    cs_lat, cs_ctx = _dft_tables(n), _dft_tables(n_ctx)
    bdc, bds = _group_dft_tables(2 * LANES)
    jsh = _fold_matrix(LANES)
    gm = _group_mean_matrix(_W_DQ)
    zero_state = jnp.zeros((bsz, N_HEADS, HEAD_DIM, HEAD_V), F32)
    ret_chunk = 1024

    xc = ctx
    for layer in range(depth):
        last = layer == depth - 1
        lam_init = 0.8 - 0.6 * math.exp(-0.3 * layer)
        mx = [mods[layer, :bsz, j * d:(j + 1) * d].reshape(bsz, 1, d) for j in range(N_MOD)]
        mc = [mods[layer, bsz:bsz + 1, j * d:(j + 1) * d].reshape(1, 1, d) for j in range(N_MOD)]
        w_in_bf = w_in[layer].astype(BF)
        qg = jnp.tile(diff_qn[layer], _W_DQ // HEAD_DIM)[None, :]
        kg = jnp.tile(diff_kn[layer], _W_DK // HEAD_DIM)[None, :]
        g_a = g_attn[layer][None, :]
        g_f = g_ffn[layer][None, :]
        decays = ret_decay[layer]
        dlam = diff_lambda[layer]
        branch_w = (w_fnet_o[layer].astype(BF), w_ret_o[layer].astype(BF), w_diff_o[layer].astype(BF),
                    w_out[layer].astype(BF))
        wr32 = w_router[layer]
        wr_hi = wr32.astype(BF)
        wr_lo = (wr32 - wr_hi.astype(F32)).astype(BF)
        lane_pad = jnp.zeros((d, LANES - N_EXPERTS), BF)
        rcat = jnp.concatenate([wr_hi, lane_pad, wr_lo, lane_pad], axis=1)
        moe_sets = []

        fo_c, rq_c, rk_c, rv_c, sg_c, dq_c, dk_c, dv_c, gt_c = _mixer_inputs(xc, mc, g_a, w_in_bf, rope_ctx, qg, kg, gm)
        ro_c, s_f, s_b = _retention(rq_c, rk_c, rv_c, sg_c, decays, zero_state, zero_state, min(ret_chunk, n_ctx))
        if not last:
            f_c = _fourier_mix(fo_c, cs_ctx, bdc, bds, jsh, min(512, n_ctx))
            do_c = _diff_attention(dq_c, [dk_c], [dv_c], dlam, lam_init, min(256, n_ctx))
            rows_c = bsz * n_ctx
            x1_c, h2_c, aff_c = _merge(f_c.reshape(rows_c, -1), ro_c.reshape(rows_c, -1), do_c.reshape(rows_c, -1),
                                       gt_c.reshape(rows_c, -1), xc.reshape(rows_c, d), mc[2], mc[3], mc[4], g_f,
                                       *branch_w, rcat, n_ctx, min(512, n_ctx))
            moe_sets.append((x1_c.reshape(bsz, n_ctx, d), h2_c.reshape(bsz, n_ctx, d), aff_c, mc[5]))

        fo, rq, rk, rv, sg, dq, dk, dv, gt = _mixer_inputs(x, mx, g_a, w_in_bf, rope_lat, qg, kg, gm)
        ro, _, _ = _retention(rq, rk, rv, sg, decays, s_f, s_b, min(ret_chunk, n))
        f = _fourier_mix(fo, cs_lat, bdc, bds, jsh, min(512, n))
        do = _diff_attention(dq, [dk, dk_c], [dv, dv_c], dlam, lam_init, min(512, n))
        rows = bsz * n
        x1, h2, aff = _merge(f.reshape(rows, -1), ro.reshape(rows, -1), do.reshape(rows, -1), gt.reshape(rows, -1),
                             x.reshape(rows, d), mx[2], mx[3], mx[4], g_f, *branch_w, rcat, n, min(1024, n))
        moe_sets.append((x1.reshape(bsz, n, d), h2.reshape(bsz, n, d), aff, mx[5]))
        moe_out = _moe(moe_sets, w_exp_gate, w_exp_up, w_exp_down, layer)
        x = moe_out[-1]
        if not last:
            xc = moe_out[0]
    return x
```

```python
import functools
import math

import jax
import jax.numpy as jnp
import numpy as np
from jax import lax
from jax.experimental import pallas as pl
from jax.experimental.pallas import tpu as pltpu

F32 = jnp.float32
BF = jnp.bfloat16
I32 = jnp.int32

GRID_W = 64
HEAD_DIM = 64
ROPE_FREQS_PER_AXIS = HEAD_DIM // 4
ROPE_BASE = 10000.0
EPS = 1e-6
LOG2E = 1.4426950408889634
FNET_GROUP_DIM = 64
N_HEADS = 4
HEAD_V = 128
N_EXPERTS = 16
EC_FACTOR = 2
N_MOD = 6
RET_SUB = 256
MERGE_SUB = 1024
TOKEN_BLOCK = 256
SLOT_WINDOW = 64
LANES = 128
VMEM_LIMIT = 56 * 1024 * 1024

_W_FO, _W_RQ, _W_RK, _W_RV, _W_RG, _W_DQ, _W_DK, _W_DV = 512, 256, 256, 512, 512, 512, 512, 512


def _cparams(sem):
    return pltpu.CompilerParams(dimension_semantics=sem, vmem_limit_bytes=VMEM_LIMIT)


def _sigmoid(v):
    return 1.0 / (1.0 + jnp.exp(-v))


def _nt_dot(a, b):
    return lax.dot_general(a, b, (((1,), (1,)), ((), ())), preferred_element_type=F32)


def _tn_dot(a, b):
    return lax.dot_general(a, b, (((0,), (0,)), ((), ())), preferred_element_type=F32)


def _mod_kernel(c_ref, w_ref, b_ref, o_ref):
    cv = c_ref[...]
    s = cv * _sigmoid(cv)
    o_ref[0] = jnp.dot(s.astype(BF), w_ref[0].astype(BF), preferred_element_type=F32) + b_ref[0]


def _modulation(cvecs, w_mod, b_mod):
    depth, d, wd = w_mod.shape
    rows = cvecs.shape[0]
    tn = 1536
    return pl.pallas_call(
        _mod_kernel,
        grid=(depth, wd // tn),
        in_specs=[pl.BlockSpec((rows, d), lambda l, j: (0, 0)),
                  pl.BlockSpec((1, d, tn), lambda l, j: (l, 0, j)),
                  pl.BlockSpec((1, 1, tn), lambda l, j: (l, 0, j))],
        out_specs=pl.BlockSpec((1, rows, tn), lambda l, j: (l, 0, j)),
        out_shape=jax.ShapeDtypeStruct((depth, rows, wd), F32),
        compiler_params=_cparams(("arbitrary", "arbitrary")),
        name="modulation",
    )(cvecs, w_mod, b_mod.reshape(depth, 1, wd))


def _rope(x, cos_t, sin_t):
    lane = lax.broadcasted_iota(I32, (x.shape[0], LANES), 1)
    first = (lane & 63) < 32
    outs = []
    for j in range(x.shape[1] // LANES):
        xc = x[:, j * LANES:(j + 1) * LANES]
        sw = jnp.where(first, pltpu.roll(xc, LANES - 32, 1), pltpu.roll(xc, 32, 1))
        outs.append(xc * cos_t + sw * sin_t)
    return jnp.concatenate(outs, axis=1)


def _group_rms(x, gm, gain):
    ms = jnp.dot((x * x).astype(BF), gm, preferred_element_type=F32)
    return x * lax.rsqrt(ms + EPS) * gain


def _inproj_kernel(x_ref, shift_ref, scale_ref, g_ref, w_ref, cos_ref, sin_ref, qg_ref, kg_ref, gm_ref,
                   fo_ref, rq_ref, rk_ref, rv_ref, sg_ref, dq_ref, dk_ref, dv_ref, gt_ref):
    x = x_ref[...]
    d = x.shape[1]
    ms = jnp.mean(x * x, axis=-1, keepdims=True)
    h = x * lax.rsqrt(ms + EPS) * g_ref[...]
    h = h * (1.0 + scale_ref[0]) + shift_ref[0]
    hb = h.astype(BF)
    cos_t = cos_ref[...]
    sin_t = sin_ref[...]
    gm = gm_ref[...]

    def proj(a, width):
        return jnp.dot(hb, w_ref[:, a:a + width], preferred_element_type=F32)

    a = 0
    fo_ref[...] = proj(a, _W_FO).astype(BF)
    a += _W_FO
    rq_ref[...] = _rope(proj(a, _W_RQ) * (HEAD_DIM ** -0.5), cos_t, sin_t).astype(BF)
    a += _W_RQ
    rk_ref[...] = _rope(proj(a, _W_RK), cos_t, sin_t).astype(BF)
    a += _W_RK
    rv_ref[...] = proj(a, _W_RV).astype(BF)
    a += _W_RV
    rg = proj(a, _W_RG)
    sg_ref[...] = (rg * _sigmoid(rg)).astype(BF)
    a += _W_RG
    dq = _group_rms(proj(a, _W_DQ), gm, qg_ref[...])
    dq_ref[...] = (_rope(dq, cos_t, sin_t) * (HEAD_DIM ** -0.5 * LOG2E)).astype(BF)
    a += _W_DQ
    dk = _group_rms(proj(a, _W_DK), gm, kg_ref[...])
    dk_ref[...] = _rope(dk, cos_t, sin_t).astype(BF)
    a += _W_DK
    dv = proj(a, _W_DV).astype(BF)
    ones = jnp.ones((dv.shape[0], HEAD_V), BF)
    for hh in range(N_HEADS):
        dv_ref[:, 2 * hh * HEAD_V:(2 * hh + 1) * HEAD_V] = dv[:, hh * HEAD_V:(hh + 1) * HEAD_V]
        dv_ref[:, (2 * hh + 1) * HEAD_V:(2 * hh + 2) * HEAD_V] = ones
    a += _W_DV
    for j in range(3):
        gl = proj(a + j * d, d)
        gt_ref[:, j * d:(j + 1) * d] = _sigmoid(gl).astype(BF)


def _in_projection(x2d, shift, scale, g, w_in_bf, cos_t, sin_t, qg, kg, gm, n_seq, tm):
    rows, d = x2d.shape
    tiles_per_b = n_seq // tm
    nb = shift.shape[0]
    win = w_in_bf.shape[1]
    widths = (_W_FO, _W_RQ, _W_RK, _W_RV, _W_RG, _W_DQ, _W_DK, 2 * _W_DV, 3 * d)

    def mod_map(i):
        return ((i // tiles_per_b) if nb > 1 else 0, 0, 0)

    def pos_map(i):
        return (i % tiles_per_b, 0)

    const2 = lambda i: (0, 0)
    return pl.pallas_call(
        _inproj_kernel,
        grid=(rows // tm,),
        in_specs=[pl.BlockSpec((tm, d), lambda i: (i, 0)),
                  pl.BlockSpec((1, 1, d), mod_map),
                  pl.BlockSpec((1, 1, d), mod_map),
                  pl.BlockSpec((1, d), const2),
                  pl.BlockSpec((d, win), const2, pipeline_mode=pl.Buffered(1)),
                  pl.BlockSpec((tm, LANES), pos_map),
                  pl.BlockSpec((tm, LANES), pos_map),
                  pl.BlockSpec((1, _W_DQ), const2),
                  pl.BlockSpec((1, _W_DK), const2),
                  pl.BlockSpec((_W_DQ, _W_DQ), const2)],
        out_specs=[pl.BlockSpec((tm, w), lambda i: (i, 0)) for w in widths],
        out_shape=[jax.ShapeDtypeStruct((rows, w), BF) for w in widths],
        compiler_params=_cparams(("arbitrary",)),
        name="in_projection",
    )(x2d, shift, scale, g, w_in_bf, cos_t, sin_t, qg, kg, gm)


def _fourier_kernel(x_ref, cs_ref, bc_ref, bs_ref, jsh_ref, o_ref, z_ref, eo_ref):
    n, w = x_ref.shape[1], x_ref.shape[2]
    half = n // 2
    gw = bc_ref.shape[0]
    fb = jsh_ref.shape[0]
    tr = o_ref.shape[1]

    @pl.when(pl.program_id(1) == 0)
    def _():
        x = x_ref[0]
        for g in range(w // gw):
            xs = x[:, g * gw:(g + 1) * gw]
            z_ref[0, :, g * gw:(g + 1) * gw] = jnp.dot(xs, bc_ref[...], preferred_element_type=F32).astype(BF)
            z_ref[1, :, g * gw:(g + 1) * gw] = jnp.dot(xs, bs_ref[...], preferred_element_type=F32).astype(BF)
        jsh = jsh_ref[...]
        n_blocks = half // fb
        for blk in range(n_blocks):
            hi_blk = 2 * n_blocks - 1 - blk
            for t in range(2):
                top = z_ref[t, hi_blk * fb:(hi_blk + 1) * fb, :]
                nxt = jnp.zeros_like(top) if blk == 0 else z_ref[t, (hi_blk + 1) * fb:(hi_blk + 2) * fb, :]
                partner = jnp.dot(jsh, jnp.concatenate([top, nxt], axis=0), preferred_element_type=F32)
                own = z_ref[t, blk * fb:(blk + 1) * fb, :].astype(F32)
                folded = own + partner if t == 0 else own - partner
                eo_ref[t * half + blk * fb:t * half + (blk + 1) * fb, :] = folded.astype(BF)

    y = jnp.dot(cs_ref[...], eo_ref[...], preferred_element_type=F32)
    parity = lax.broadcasted_iota(I32, (tr, 1), 0) & 1
    sign = (1.0 - 2.0 * parity.astype(F32)) * (1.0 / math.sqrt(n))
    o_ref[0] = (y + sign * z_ref[0, half:half + 1, :].astype(F32)).astype(BF)


def _fourier_mix(fo, cs, bdc, bds, jsh, tr):
    b, n, w = fo.shape
    assert n % (2 * jsh.shape[0]) == 0 and tr % 2 == 0
    return pl.pallas_call(
        _fourier_kernel,
        grid=(b, n // tr),
        in_specs=[pl.BlockSpec((1, n, w), lambda i, j: (i, 0, 0)),
                  pl.BlockSpec((tr, n), lambda i, j: (j, 0)),
                  pl.BlockSpec(bdc.shape, lambda i, j: (0, 0)),
                  pl.BlockSpec(bds.shape, lambda i, j: (0, 0)),
                  pl.BlockSpec(jsh.shape, lambda i, j: (0, 0))],
        out_specs=pl.BlockSpec((1, tr, w), lambda i, j: (i, j, 0)),
        out_shape=jax.ShapeDtypeStruct((b, n, w), BF),
        scratch_shapes=[pltpu.VMEM((2, n, w), BF), pltpu.VMEM((n, w), BF)],
        compiler_params=_cparams(("arbitrary", "arbitrary")),
        name="fourier_mix",
    )(fo, cs, bdc, bds, jsh)


def _ret_kernel(lg_ref, q_ref, k_ref, v_ref, sg_ref, sf_ref, sb_ref, o_ref, sfo_ref, sbo_ref,
                acc_ref, tmp_ref, st_ref, dm_ref, qd_ref, qm_ref, kd_ref, cd_ref, *, chunk, n_chunks, sub):
    b = pl.program_id(0)
    p = pl.program_id(1)
    c = pl.program_id(2)
    sf = float(sub)
    n_sub = chunk // sub

    @pl.when((b == 0) & (p == 0) & (c == 0))
    def _tables():
        ii = lax.broadcasted_iota(I32, (sub, sub), 0).astype(F32)
        jj = lax.broadcasted_iota(I32, (sub, sub), 1).astype(F32)
        pos = lax.broadcasted_iota(I32, (sub, LANES), 0).astype(F32)
        lane = lax.broadcasted_iota(I32, (sub, LANES), 1)
        for dr in range(2):
            if dr == 0:
                rel, keep = ii - jj, ii >= jj
                qe, ke = pos + 1.0, (sf - 1.0) - pos
            else:
                rel, keep = jj - ii, jj > ii
                qe, ke = sf - pos, pos
            for hp in range(N_HEADS // 2):
                lg0 = jnp.full((sub, LANES), lg_ref[dr, 2 * hp], F32)
                lg1 = jnp.full((sub, LANES), lg_ref[dr, 2 * hp + 1], F32)
                lgp = jnp.where(lane < HEAD_DIM, lg0, lg1)
                kd_ref[dr, hp] = jnp.exp(lgp * ke)
            for h in range(N_HEADS):
                lgs = lg_ref[dr, h]
                in_head = (lane >= (h % 2) * HEAD_DIM) & (lane < (h % 2 + 1) * HEAD_DIM)
                dm_ref[dr, h] = jnp.where(keep, jnp.exp(jnp.full((sub, sub), lgs, F32) * jnp.maximum(rel, 0.0)), 0.0)
                qd_ref[dr, h] = jnp.where(in_head, jnp.exp(jnp.full((sub, LANES), lgs, F32) * qe), 0.0)
                cd_ref[dr, h] = jnp.exp(jnp.full((LANES, LANES), lgs, F32) * sf)
        for h in range(N_HEADS):
            in_head = (lane >= (h % 2) * HEAD_DIM) & (lane < (h % 2 + 1) * HEAD_DIM)
            qm_ref[h] = jnp.where(in_head, 1.0, 0.0)

    def load_state(src_ref):
        z = jnp.zeros((HEAD_DIM, HEAD_V), F32)
        for h in range(N_HEADS):
            s = src_ref[0, h]
            st_ref[h] = jnp.concatenate([s, z], axis=0) if h % 2 == 0 else jnp.concatenate([z, s], axis=0)

    @pl.when((c == 0) & (p == 0))
    def _():
        load_state(sf_ref)

    @pl.when((c == 0) & (p == 1))
    def _():
        load_state(sb_ref)

    sts = [st_ref[h] for h in range(N_HEADS)]
    for u in range(n_sub):
        r0 = pl.multiple_of(jnp.where(p == 0, u, n_sub - 1 - u) * sub, sub)
        rows = pl.ds(r0, sub)
        q = q_ref[0, rows, :]
        k = k_ref[0, rows, :]
        v = v_ref[0, rows, :]
        outs = []
        for hp in range(N_HEADS // 2):
            qp = q[:, hp * LANES:(hp + 1) * LANES].astype(F32)
            kp = k[:, hp * LANES:(hp + 1) * LANES]
            kdp = (kp.astype(F32) * kd_ref[p, hp]).astype(BF)
            for h in (2 * hp, 2 * hp + 1):
                vh = v[:, h * HEAD_V:(h + 1) * HEAD_V]
                qm = (qp * qm_ref[h]).astype(BF)
                qdq = (qp * qd_ref[p, h]).astype(BF)
                s = _nt_dot(qm, kp) * dm_ref[p, h]
                o = (jnp.dot(s.astype(BF), vh, preferred_element_type=F32)
                     + jnp.dot(qdq, sts[h].astype(BF), preferred_element_type=F32))
                sts[h] = sts[h] * cd_ref[p, h] + _tn_dot(kdp, vh)
                outs.append(o)
        tmp_ref[rows, :] = jnp.concatenate(outs, axis=1)
    for h in range(N_HEADS):
        st_ref[h] = sts[h]
    idx = jnp.where(p == 0, c, n_chunks - 1 - c)

    @pl.when(p == 0)
    def _():
        acc_ref[idx] = tmp_ref[...]

    @pl.when(p == 1)
    def _():
        tot = acc_ref[idx] + tmp_ref[...]
        sg = sg_ref[0].astype(F32)
        ys = []
        for h in range(N_HEADS):
            oh = tot[:, h * HEAD_V:(h + 1) * HEAD_V]
            mu = jnp.mean(oh, axis=-1, keepdims=True)
            dlt = oh - mu
            var = jnp.mean(dlt * dlt, axis=-1, keepdims=True)
            ys.append(dlt * lax.rsqrt(var + EPS))
        o_ref[0] = (jnp.concatenate(ys, axis=1) * sg).astype(BF)

    def store_state(dst_ref):
        for h in range(N_HEADS):
            r0 = (h % 2) * HEAD_DIM
            dst_ref[0, h] = st_ref[h, r0:r0 + HEAD_DIM, :]

    @pl.when((c == n_chunks - 1) & (p == 0))
    def _():
        store_state(sfo_ref)

    @pl.when((c == n_chunks - 1) & (p == 1))
    def _():
        store_state(sbo_ref)


def _retention(rq, rk, rv, sg, decays, s_f, s_b, chunk):
    b, n, _ = rq.shape
    n_chunks = n // chunk
    wv = rv.shape[2]

    def seq_map(i, p, c):
        return (i, jnp.where(p == 0, c, n_chunks - 1 - c), 0)

    def out_map(i, p, c):
        return (i, jnp.where(p == 0, n_chunks - 1, n_chunks - 1 - c), 0)

    st_spec = pl.BlockSpec((1, N_HEADS, HEAD_DIM, HEAD_V), lambda i, p, c: (i, 0, 0, 0))
    st_shape = jax.ShapeDtypeStruct((b, N_HEADS, HEAD_DIM, HEAD_V), F32)
    sub = min(RET_SUB, chunk)
    kern = functools.partial(_ret_kernel, chunk=chunk, n_chunks=n_chunks, sub=sub)
    return pl.pallas_call(
        kern,
        grid=(b, 2, n_chunks),
        in_specs=[pl.BlockSpec(memory_space=pltpu.SMEM),
                  pl.BlockSpec((1, chunk, rq.shape[2]), seq_map),
                  pl.BlockSpec((1, chunk, rk.shape[2]), seq_map),
                  pl.BlockSpec((1, chunk, wv), seq_map),
                  pl.BlockSpec((1, chunk, wv), seq_map),
                  st_spec, st_spec],
        out_specs=[pl.BlockSpec((1, chunk, wv), out_map), st_spec, st_spec],
        out_shape=[jax.ShapeDtypeStruct((b, n, wv), BF), st_shape, st_shape],
        scratch_shapes=[pltpu.VMEM((n_chunks, chunk, wv), F32),
                        pltpu.VMEM((chunk, wv), F32),
                        pltpu.VMEM((N_HEADS, LANES, HEAD_V), F32),
                        pltpu.VMEM((2, N_HEADS, sub, sub), F32),
                        pltpu.VMEM((2, N_HEADS, sub, LANES), F32),
                        pltpu.VMEM((N_HEADS, sub, LANES), F32),
                        pltpu.VMEM((2, N_HEADS // 2, sub, LANES), F32),
                        pltpu.VMEM((2, N_HEADS, LANES, HEAD_V), F32)],
        compiler_params=_cparams(("arbitrary", "arbitrary", "arbitrary")),
        name="retention",
    )(decays, rq, rk, rv, sg, s_f, s_b)


def _dattn_kernel(dl_ref, q_ref, *refs, n_src, lam_init):
    k_refs = refs[:n_src]
    v_refs = refs[n_src:2 * n_src]
    o_ref = refs[2 * n_src]
    dl = dl_ref[...]
    lam = (jnp.exp(jnp.sum(dl[0:1] * dl[1:2], axis=-1, keepdims=True))
           - jnp.exp(jnp.sum(dl[2:3] * dl[3:4], axis=-1, keepdims=True)) + lam_init)
    q = q_ref[0].astype(F32)
    tq = q.shape[0]
    lane = lax.broadcasted_iota(I32, (tq, LANES), 1)
    ys = []
    for h in range(N_HEADS):
        qh = q[:, h * LANES:(h + 1) * LANES]
        qs = jnp.concatenate([jnp.where(lane < HEAD_DIM, qh, 0.0), jnp.where(lane >= HEAD_DIM, qh, 0.0)],
                             axis=0).astype(BF)
        ss = [_nt_dot(qs, kr[0, :, h * LANES:(h + 1) * LANES]) for kr in k_refs]
        m = ss[0].max(axis=-1, keepdims=True)
        for s in ss[1:]:
            m = jnp.maximum(m, s.max(axis=-1, keepdims=True))
        acc = jnp.zeros((2 * tq, 2 * HEAD_V), F32)
        for s, vr in zip(ss, v_refs):
            pexp = jnp.exp2(s - m).astype(BF)
            acc = acc + jnp.dot(pexp, vr[0, :, 2 * h * HEAD_V:(2 * h + 2) * HEAD_V], preferred_element_type=F32)
        o = acc[:, :HEAD_V] / acc[:, HEAD_V:]
        oh = o[:tq] - lam * o[tq:]
        ms = jnp.mean(oh * oh, axis=-1, keepdims=True)
        ys.append(oh * lax.rsqrt(ms + EPS) * (1.0 - lam_init))
    o_ref[0] = jnp.concatenate(ys, axis=1).astype(BF)


def _diff_attention(dq, ks, vs, dlam, lam_init, tq):
    b, n, w = dq.shape
    n_src = len(ks)
    kern = functools.partial(_dattn_kernel, n_src=n_src, lam_init=lam_init)
    kv_specs = [pl.BlockSpec((1, a.shape[1], a.shape[2]), lambda i, j: (i, 0, 0)) for a in (*ks, *vs)]
    return pl.pallas_call(
        kern,
        grid=(b, n // tq),
        in_specs=[pl.BlockSpec(dlam.shape, lambda i, j: (0, 0)),
                  pl.BlockSpec((1, tq, w), lambda i, j: (i, j, 0))] + kv_specs,
        out_specs=pl.BlockSpec((1, tq, w), lambda i, j: (i, j, 0)),
        out_shape=jax.ShapeDtypeStruct((b, n, w), BF),
        compiler_params=_cparams(("arbitrary", "arbitrary")),
        name="diff_attention",
    )(dlam, dq, *ks, *vs)


def _merge_kernel(f_ref, ro_ref, do_ref, gt_ref, x_ref, m2_ref, m3_ref, m4_ref, g_ref,
                  wf_ref, wr_ref, wd_ref, wo_ref, rcat_ref, x1_ref, h2_ref, aff_ref):
    tm, d = x_ref.shape
    st = min(MERGE_SUB, tm)
    for r0 in range(0, tm, st):
        rows = slice(r0, r0 + st)
        t = (gt_ref[rows, 0:d].astype(F32) * jnp.dot(f_ref[rows, :], wf_ref[...], preferred_element_type=F32)
             + gt_ref[rows, d:2 * d].astype(F32) * jnp.dot(ro_ref[rows, :], wr_ref[...], preferred_element_type=F32)
             + gt_ref[rows, 2 * d:3 * d].astype(F32) * jnp.dot(do_ref[rows, :], wd_ref[...],
                                                                preferred_element_type=F32))
        mix = jnp.dot(t.astype(BF), wo_ref[...], preferred_element_type=F32)
        x1 = x_ref[rows, :] + m2_ref[0] * mix
        x1_ref[rows, :] = x1
        ms = jnp.mean(x1 * x1, axis=-1, keepdims=True)
        h2 = x1 * lax.rsqrt(ms + EPS) * g_ref[...]
        h2 = h2 * (1.0 + m4_ref[0]) + m3_ref[0]
        hi = h2.astype(BF)
        lo = (h2 - hi.astype(F32)).astype(BF)
        h2_ref[rows, :] = hi
        l1 = jnp.dot(hi, rcat_ref[...], preferred_element_type=F32)
        lt = l1[:, :LANES] + l1[:, LANES:] + jnp.dot(lo, rcat_ref[:, :LANES], preferred_element_type=F32)
        lane = lax.broadcasted_iota(I32, lt.shape, 1)
        lt = jnp.where(lane < N_EXPERTS, lt, -jnp.inf)
        ex = jnp.exp(lt - lt.max(axis=-1, keepdims=True))
        aff = ex / jnp.sum(ex, axis=-1, keepdims=True)
        aff_ref[0, :, rows] = aff.T[:N_EXPERTS, :]


def _merge(f, ro, do, gt, x2d, m2, m3, m4, g, wf, wr, wd, wo, rcat, n_seq, tm):
    rows, d = x2d.shape
    tiles_per_b = n_seq // tm
    nb = m2.shape[0]
    bsz = rows // n_seq
    ne = N_EXPERTS

    def mod_map(i):
        return ((i // tiles_per_b) if nb > 1 else 0, 0, 0)

    const2 = lambda i: (0, 0)
    row_spec = lambda w: pl.BlockSpec((tm, w), lambda i: (i, 0))
    return pl.pallas_call(
        _merge_kernel,
        grid=(rows // tm,),
        in_specs=[row_spec(f.shape[1]), row_spec(ro.shape[1]), row_spec(do.shape[1]), row_spec(gt.shape[1]),
                  row_spec(d),
                  pl.BlockSpec((1, 1, d), mod_map), pl.BlockSpec((1, 1, d), mod_map), pl.BlockSpec((1, 1, d), mod_map),
                  pl.BlockSpec((1, d), const2),
                  pl.BlockSpec(wf.shape, const2), pl.BlockSpec(wr.shape, const2), pl.BlockSpec(wd.shape, const2),
                  pl.BlockSpec(wo.shape, const2), pl.BlockSpec(rcat.shape, const2)],
        out_specs=[row_spec(d), row_spec(d),
                   pl.BlockSpec((1, ne, tm), lambda i: (i // tiles_per_b, 0, i % tiles_per_b))],
        out_shape=[jax.ShapeDtypeStruct((rows, d), F32), jax.ShapeDtypeStruct((rows, d), BF),
                   jax.ShapeDtypeStruct((bsz, ne, n_seq), F32)],
        compiler_params=_cparams(("arbitrary",)),
        name="merge_router",
    )(f, ro, do, gt, x2d, m2, m3, m4, g, wf, wr, wd, wo, rcat)


def _route_kernel(a_ref, slot_ref, offs_ref, *, cap, blk):
    a = a_ref[...]
    ne, n = a.shape
    capf = float(cap)

    def enough(t):
        return jnp.sum(jnp.where(a >= t, 1.0, 0.0), axis=-1, keepdims=True) >= capf

    tiny = jnp.full((ne, 1), 2.0 ** -126, F32)
    found = enough(tiny)
    cur = tiny
    for step in (64, 32, 16, 8, 4, 2, 1):
        cand = cur * (2.0 ** step)
        cur = jnp.where(enough(cand), cand, cur)
    base = cur

    def mantissa_bit(_, carry):
        cur, stepv = carry
        cand = cur + stepv
        return jnp.where(enough(cand), cand, cur), stepv * 0.5

    cur, ulp = lax.fori_loop(0, 23, mantissa_bit, (cur, base * 0.5))
    lo = jnp.where(found, cur, 0.0)
    hi = jnp.where(found, cur + ulp * 2.0, tiny)

    def refine(_, carry):
        lo, hi = carry
        mid = lo + (hi - lo) * 0.5
        ok = enough(mid)
        return jnp.where(ok, mid, lo), jnp.where(ok, hi, mid)

    lo, hi = lax.fori_loop(0, 24, refine, (lo, hi))
    ri = lax.broadcasted_iota(I32, (blk, blk), 0)
    ci = lax.broadcasted_iota(I32, (blk, blk), 1)
    upper = jnp.where(ri < ci, 1.0, 0.0).astype(BF)

    def excl_cumsum(m):
        carry = jnp.zeros((ne, 1), F32)
        outs = []
        for j in range(n // blk):
            mb = m[:, j * blk:(j + 1) * blk]
            outs.append(jnp.dot(mb.astype(BF), upper, preferred_element_type=F32) + carry)
            carry = carry + jnp.sum(mb, axis=-1, keepdims=True)
        return jnp.concatenate(outs, axis=1)

    gt = a >= hi
    tie = (a >= lo) & (a < hi)
    need = capf - jnp.sum(jnp.where(gt, 1.0, 0.0), axis=-1, keepdims=True)
    sel = gt | (tie & (excl_cumsum(jnp.where(tie, 1.0, 0.0)) < need))
    selm = jnp.where(sel, 1.0, 0.0)
    slot = excl_cumsum(selm)
    slot_ref[...] = jnp.where(sel, slot.astype(I32), -1)
    tok = lax.broadcasted_iota(I32, (n, LANES), 0)
    blk_start = lax.broadcasted_iota(I32, (n, LANES), 1) * blk
    before = jnp.where((tok < blk_start) & (blk_start <= n), 1.0, 0.0).astype(BF)
    offs_ref[...] = jnp.dot(selm.astype(BF), before, preferred_element_type=F32).astype(I32)


def _route(aff_t, cap):
    b, ne, n = aff_t.shape
    kern = functools.partial(_route_kernel, cap=cap, blk=min(TOKEN_BLOCK, n))
    slot, offs = pl.pallas_call(
        kern,
        grid=(1,),
        in_specs=[pl.BlockSpec((b * ne, n), lambda i: (0, 0))],
        out_specs=[pl.BlockSpec((b * ne, n), lambda i: (0, 0)), pl.BlockSpec((b * ne, LANES), lambda i: (0, 0))],
        out_shape=[jax.ShapeDtypeStruct((b * ne, n), I32), jax.ShapeDtypeStruct((b * ne, LANES), I32)],
        compiler_params=_cparams(("arbitrary",)),
        name="route",
    )(aff_t.reshape(b * ne, n))
    return slot.reshape(b, ne, n), offs


def _window_start(off, cap, win):
    return jnp.minimum(lax.shift_left(lax.shift_right_logical(off, 4), 4), cap - win)


def _windows_fit(offs_ref, row0, ne, blk0, nblk, cap, win):
    bad = jnp.int32(0)
    for e in range(ne):
        for jj in range(nblk):
            j = blk0 + jj
            off = offs_ref[row0 + e, j]
            end = offs_ref[row0 + e, j + 1]
            bad = bad | (end - _window_start(off, cap, win) > win).astype(I32)
    return bad == 0


def _gather_kernel(offs_ref, slot_ref, h_ref, xs_ref, *, cap, win, tb):
    ne, n = slot_ref.shape[1], slot_ref.shape[3]
    nblk = n // tb
    row0 = pl.program_id(0) * ne
    fits = _windows_fit(offs_ref, row0, ne, 0, nblk, cap, win)

    @pl.when(fits)
    def _windowed():
        xs_ref[...] = jnp.zeros(xs_ref.shape, BF)
        rows = lax.broadcasted_iota(I32, (win, tb), 0)
        for j in range(nblk):
            starts, lhs = [], []
            for e in range(ne):
                w0 = pl.multiple_of(_window_start(offs_ref[row0 + e, j], cap, win), 16)
                starts.append(w0)
                lhs.append(jnp.where(slot_ref[0, e, :, j * tb:(j + 1) * tb] == rows + w0, 1.0, 0.0).astype(BF))
            got = jnp.dot(jnp.concatenate(lhs, axis=0), h_ref[0, j * tb:(j + 1) * tb, :],
                          preferred_element_type=F32)
            for e in range(ne):
                dst = xs_ref.at[0, e, pl.ds(starts[e], win), :]
                dst[...] = (dst[...].astype(F32) + got[e * win:(e + 1) * win]).astype(BF)

    @pl.when(jnp.logical_not(fits))
    def _dense():
        sub = lax.broadcasted_iota(I32, (cap, n), 0)
        hb = h_ref[0]
        for e in range(ne):
            onehot = jnp.where(sub == slot_ref[0, e], 1.0, 0.0).astype(BF)
            xs_ref[0, e] = jnp.dot(onehot, hb, preferred_element_type=F32).astype(BF)


def _gather(slot, offs, h2, cap):
    b, ne, n = slot.shape
    d = h2.shape[2]
    tb = min(TOKEN_BLOCK, n)
    kern = functools.partial(_gather_kernel, cap=cap, win=min(SLOT_WINDOW, cap), tb=tb)
    grid_spec = pltpu.PrefetchScalarGridSpec(
        num_scalar_prefetch=1,
        grid=(b,),
        in_specs=[pl.BlockSpec((1, ne, 1, n), lambda i, offs: (i, 0, 0, 0)),
                  pl.BlockSpec((1, n, d), lambda i, offs: (i, 0, 0))],
        out_specs=pl.BlockSpec((1, ne, cap, d), lambda i, offs: (i, 0, 0, 0)))
    return pl.pallas_call(
        kern,
        grid_spec=grid_spec,
        out_shape=jax.ShapeDtypeStruct((b, ne, cap, d), BF),
        compiler_params=_cparams(("arbitrary",)),
        name="moe_gather",
    )(offs, slot.reshape(b, ne, 1, n), h2)


def _ffn_kernel(*refs, n_src, n_fc):
    xs_refs = refs[:n_src]
    wg_ref, wu_ref, wd_ref = refs[n_src:n_src + 3]
    ye_refs = refs[n_src + 3:2 * n_src + 3]
    x_ref, hid_ref = refs[2 * n_src + 3:]
    j = pl.program_id(1)
    fc = wg_ref.shape[3]
    row_spans = []
    r0 = 0
    for xr in xs_refs:
        rows = xr.shape[0] * xr.shape[2]
        row_spans.append((r0, rows))
        r0 += rows

    @pl.when(j == 0)
    def _():
        for xr, (s0, rows) in zip(xs_refs, row_spans):
            x_ref[s0:s0 + rows, :] = xr[:, 0].reshape(rows, xr.shape[3])

    @pl.when(j < n_fc)
    def _():
        x = x_ref[...]
        g = jnp.dot(x, wg_ref[0, 0].astype(BF), preferred_element_type=F32)
        u = jnp.dot(x, wu_ref[0, 0].astype(BF), preferred_element_type=F32)
        hid_ref[j] = (g * _sigmoid(g) * u).astype(BF)

    @pl.when(j >= n_fc)
    def _():
        y = jnp.dot(hid_ref[0], wd_ref[0, 0, 0:fc, :].astype(BF), preferred_element_type=F32)
        for k in range(1, n_fc):
            y = y + jnp.dot(hid_ref[k], wd_ref[0, 0, k * fc:(k + 1) * fc, :].astype(BF), preferred_element_type=F32)
        for yr, (s0, rows) in zip(ye_refs, row_spans):
            yr[:, 0] = y[s0:s0 + rows].reshape(yr.shape[0], yr.shape[2], yr.shape[3]).astype(BF)


def _expert_ffn(xs_list, w_gate, w_up, w_down, layer, fchunk, ochunk):
    ne, d = xs_list[0].shape[1], xs_list[0].shape[3]
    f = w_gate.shape[3]
    n_fc, n_oc = f // fchunk, d // ochunk
    n_src = len(xs_list)
    total_rows = sum(x.shape[0] * x.shape[2] for x in xs_list)
    kern = functools.partial(_ffn_kernel, n_src=n_src, n_fc=n_fc)
    up_map = lambda e, j: (layer, e, 0, jnp.minimum(j, n_fc - 1))
    down_map = lambda e, j: (layer, e, 0, jnp.maximum(j - n_fc, 0))
    out_map = lambda e, j: (0, e, 0, jnp.maximum(j - n_fc, 0))
    outs = pl.pallas_call(
        kern,
        grid=(ne, n_fc + n_oc),
        in_specs=[pl.BlockSpec((x.shape[0], 1, x.shape[2], d), lambda e, j: (0, e, 0, 0), pipeline_mode=pl.Buffered(1))
                  for x in xs_list]
        + [pl.BlockSpec((1, 1, d, fchunk), up_map),
           pl.BlockSpec((1, 1, d, fchunk), up_map),
           pl.BlockSpec((1, 1, f, ochunk), down_map)],
        out_specs=[pl.BlockSpec((x.shape[0], 1, x.shape[2], ochunk), out_map) for x in xs_list],
        out_shape=[jax.ShapeDtypeStruct(x.shape, BF) for x in xs_list],
        scratch_shapes=[pltpu.VMEM((total_rows, d), BF), pltpu.VMEM((n_fc, total_rows, fchunk), BF)],
        compiler_params=_cparams(("arbitrary", "arbitrary")),
        name="expert_ffn",
    )(*xs_list, w_gate, w_up, w_down)
    return list(outs)


def _combine_kernel(offs_ref, x1_ref, m5_ref, slot_ref, aff_ref, ye_ref, o_ref, ywin_ref, *, cap, win, tb):
    ne, tn = slot_ref.shape[1], slot_ref.shape[3]
    row0 = pl.program_id(0) * ne
    blk0 = pl.program_id(1) * (tn // tb)
    fits = _windows_fit(offs_ref, row0, ne, blk0, tn // tb, cap, win)

    @pl.when(fits)
    def _windowed():
        rows = lax.broadcasted_iota(I32, (win, tb), 0)
        for jj in range(tn // tb):
            tok = slice(jj * tb, (jj + 1) * tb)
            pieces = []
            for e in range(ne):
                w0 = pl.multiple_of(_window_start(offs_ref[row0 + e, blk0 + jj], cap, win), 16)
                ywin_ref[e * win:(e + 1) * win, :] = ye_ref[0, e, pl.ds(w0, win), :]
                pieces.append(jnp.where(slot_ref[0, e, :, tok] == rows + w0, aff_ref[0, e, :, tok], 0.0).astype(BF))
            acc = _tn_dot(jnp.concatenate(pieces, axis=0), ywin_ref[...])
            o_ref[0, tok, :] = x1_ref[0, tok, :] + m5_ref[0] * acc

    @pl.when(jnp.logical_not(fits))
    def _dense():
        rows = lax.broadcasted_iota(I32, (cap, tn), 0)
        acc = jnp.zeros(x1_ref.shape[1:], F32)
        for e in range(ne):
            weighted = jnp.where(slot_ref[0, e] == rows, aff_ref[0, e], 0.0).astype(BF)
            acc = acc + _tn_dot(weighted, ye_ref[0, e])
        o_ref[0] = x1_ref[0] + m5_ref[0] * acc


def _combine(x1, m5, slot, aff_t, offs, ye, tn):
    b, n, d = x1.shape
    ne, cap = ye.shape[1], ye.shape[2]
    nb = m5.shape[0]
    tb = min(TOKEN_BLOCK, n)
    win = min(SLOT_WINDOW, cap)
    kern = functools.partial(_combine_kernel, cap=cap, win=win, tb=tb)
    grid_spec = pltpu.PrefetchScalarGridSpec(
        num_scalar_prefetch=1,
        grid=(b, n // tn),
        in_specs=[pl.BlockSpec((1, tn, d), lambda i, j, offs: (i, j, 0)),
                  pl.BlockSpec((1, 1, d), lambda i, j, offs: (i if nb > 1 else 0, 0, 0)),
                  pl.BlockSpec((1, ne, 1, tn), lambda i, j, offs: (i, 0, 0, j)),
                  pl.BlockSpec((1, ne, 1, tn), lambda i, j, offs: (i, 0, 0, j)),
                  pl.BlockSpec((1, ne, cap, d), lambda i, j, offs: (i, 0, 0, 0))],
        out_specs=pl.BlockSpec((1, tn, d), lambda i, j, offs: (i, j, 0)),
        scratch_shapes=[pltpu.VMEM((ne * win, d), BF)])
    return pl.pallas_call(
        kern,
        grid_spec=grid_spec,
        out_shape=jax.ShapeDtypeStruct((b, n, d), F32),
        compiler_params=_cparams(("arbitrary", "arbitrary")),
        name="moe_combine",
    )(offs, x1, m5, slot.reshape(b, ne, 1, n), aff_t.reshape(b, ne, 1, n), ye)


def _rope_tables(n, use_rope):
    if not use_rope:
        return jnp.ones((n, LANES), F32), jnp.zeros((n, LANES), F32)
    rows = n // GRID_W
    row = jnp.repeat(jnp.arange(rows, dtype=F32), GRID_W)
    col = jnp.tile(jnp.arange(GRID_W, dtype=F32), rows)
    inv = ROPE_BASE ** (-jnp.arange(ROPE_FREQS_PER_AXIS, dtype=F32) / ROPE_FREQS_PER_AXIS)
    ang = jnp.concatenate([row[:, None] * inv, col[:, None] * inv], axis=-1)
    cos, sin = jnp.cos(ang), jnp.sin(ang)
    cos_t = jnp.tile(cos, (1, LANES // (HEAD_DIM // 2)))
    sin_t = jnp.tile(jnp.concatenate([-sin, sin], axis=-1), (1, LANES // HEAD_DIM))
    return cos_t, sin_t


def _dft_tables(n):
    n0 = 64
    n1 = n // n0
    k = np.arange(n, dtype=np.int64)[:, None]
    ang1 = jnp.asarray(2.0 * np.pi * ((k * np.arange(n1 // 2)[None, :]) % n1) / n1, F32)
    ang0 = jnp.asarray(2.0 * np.pi * ((k * np.arange(n0)[None, :]) % n) / n, F32)
    c1, s1, c0, s0 = jnp.cos(ang1), jnp.sin(ang1), jnp.cos(ang0), jnp.sin(ang0)
    scale = 1.0 / math.sqrt(n)
    cos_n = (c1[:, :, None] * c0[:, None, :] - s1[:, :, None] * s0[:, None, :]).reshape(n, n // 2) * scale
    sin_n = (s1[:, :, None] * c0[:, None, :] + c1[:, :, None] * s0[:, None, :]).reshape(n, n // 2) * scale
    return jnp.concatenate([cos_n, -sin_n], axis=1).astype(BF)


def _fold_matrix(fb):
    i = np.arange(fb)[:, None]
    s = np.arange(2 * fb)[None, :]
    return jnp.asarray(np.where(s == fb - i, 1.0, 0.0), BF)


def _group_dft_tables(width):
    g = FNET_GROUP_DIM
    idx = np.arange(width)
    same = (idx[:, None] // g) == (idx[None, :] // g)
    ang = 2.0 * np.pi * (((idx[:, None] % g) * (idx[None, :] % g)) % g) / g
    scale = 1.0 / math.sqrt(g)
    bdc = np.where(same, np.cos(ang), 0.0) * scale
    bds = np.where(same, np.sin(ang), 0.0) * scale
    return jnp.asarray(bdc, F32).astype(BF), jnp.asarray(bds, F32).astype(BF)


def _group_mean_matrix(width):
    idx = np.arange(width)
    same = (idx[:, None] // HEAD_DIM) == (idx[None, :] // HEAD_DIM)
    return jnp.asarray(np.where(same, 1.0 / HEAD_DIM, 0.0), BF)


def _mixer_inputs(x3, mods, g_attn_l, w_in_bf, tables, qg, kg, gm):
    b, n, d = x3.shape
    tm = min(512, n)
    outs = _in_projection(x3.reshape(b * n, d), mods[0], mods[1], g_attn_l, w_in_bf, tables[0], tables[1],
                          qg, kg, gm, n, tm)
    return [o.reshape(b, n, o.shape[1]) for o in outs]


def _moe(sets, w_gate, w_up, w_down, layer):
    slots, offss, xss = [], [], []
    for x1, h2, aff_t, _ in sets:
        cap = EC_FACTOR * x1.shape[1] // N_EXPERTS
        slot, offs = _route(aff_t, cap)
        offs = offs[:, :16]
        slots.append(slot)
        offss.append(offs)
        xss.append(_gather(slot, offs, h2, cap))
    yes = _expert_ffn(xss, w_gate, w_up, w_down, layer, 512, 512)
    return [_combine(x1, m5, slot, aff_t, offs, ye, min(512, x1.shape[1]))
            for (x1, _, aff_t, m5), slot, offs, ye in zip(sets, slots, offss, yes)]


def kernel(x, c, ctx, c_ctx, w_mod, b_mod, g_attn, g_ffn, w_in, ret_decay, diff_qn, diff_kn, diff_lambda,
           w_fnet_o, w_ret_o, w_diff_o, w_out, w_router, w_exp_gate, w_exp_up, w_exp_down):
    bsz, n, d = x.shape
    n_ctx = ctx.shape[1]
    depth = w_mod.shape[0]

    pad = (-(bsz + 1)) % 8
    cvecs = jnp.concatenate([c, c_ctx[None, :], jnp.zeros((pad, d), F32)], axis=0)
    mods = _modulation(cvecs, w_mod, b_mod)

    rope_lat = _rope_tables(n, True)
    rope_ctx = _rope_tables(n_ctx, False)
    cs_lat, cs_ctx = _dft_tables(n), _dft_tables(n_ctx)
    bdc, bds = _group_dft_tables(2 * LANES)
    jsh = _fold_matrix(LANES)
    gm = _group_mean_matrix(_W_DQ)
    zero_state = jnp.zeros((bsz, N_HEADS, HEAD_DIM, HEAD_V), F32)
    ret_chunk = 1024

    xc = ctx
    for layer in range(depth):
        last = layer == depth - 1
        lam_init = 0.8 - 0.6 * math.exp(-0.3 * layer)
        mx = [mods[layer, :bsz, j * d:(j + 1) * d].reshape(bsz, 1, d) for j in range(N_MOD)]
        mc = [mods[layer, bsz:bsz + 1, j * d:(j + 1) * d].reshape(1, 1, d) for j in range(N_MOD)]
        w_in_bf = w_in[layer].astype(BF)
        qg = jnp.tile(diff_qn[layer], _W_DQ // HEAD_DIM)[None, :]
        kg = jnp.tile(diff_kn[layer], _W_DK // HEAD_DIM)[None, :]
        g_a = g_attn[layer][None, :]
        g_f = g_ffn[layer][None, :]
        decays = ret_decay[layer]
        dlam = diff_lambda[layer]
        branch_w = (w_fnet_o[layer].astype(BF), w_ret_o[layer].astype(BF), w_diff_o[layer].astype(BF),
                    w_out[layer].astype(BF))
        wr32 = w_router[layer]
        wr_hi = wr32.astype(BF)
        wr_lo = (wr32 - wr_hi.astype(F32)).astype(BF)
        lane_pad = jnp.zeros((d, LANES - N_EXPERTS), BF)
        rcat = jnp.concatenate([wr_hi, lane_pad, wr_lo, lane_pad], axis=1)
        moe_sets = []

        fo_c, rq_c, rk_c, rv_c, sg_c, dq_c, dk_c, dv_c, gt_c = _mixer_inputs(xc, mc, g_a, w_in_bf, rope_ctx, qg, kg, gm)
        ro_c, s_f, s_b = _retention(rq_c, rk_c, rv_c, sg_c, decays, zero_state, zero_state, min(ret_chunk, n_ctx))
        if not last:
            f_c = _fourier_mix(fo_c, cs_ctx, bdc, bds, jsh, min(512, n_ctx))
            do_c = _diff_attention(dq_c, [dk_c], [dv_c], dlam, lam_init, min(256, n_ctx))
            rows_c = bsz * n_ctx
            x1_c, h2_c, aff_c = _merge(f_c.reshape(rows_c, -1), ro_c.reshape(rows_c, -1), do_c.reshape(rows_c, -1),
                                       gt_c.reshape(rows_c, -1), xc.reshape(rows_c, d), mc[2], mc[3], mc[4], g_f,
                                       *branch_w, rcat, n_ctx, min(512, n_ctx))
            moe_sets.append((x1_c.reshape(bsz, n_ctx, d), h2_c.reshape(bsz, n_ctx, d), aff_c, mc[5]))

        fo, rq, rk, rv, sg, dq, dk, dv, gt = _mixer_inputs(x, mx, g_a, w_in_bf, rope_lat, qg, kg, gm)
        ro, _, _ = _retention(rq, rk, rv, sg, decays, s_f, s_b, min(ret_chunk, n))
        f = _fourier_mix(fo, cs_lat, bdc, bds, jsh, min(512, n))
        do = _diff_attention(dq, [dk, dk_c], [dv, dv_c], dlam, lam_init, min(512, n))
        rows = bsz * n
        x1, h2, aff = _merge(f.reshape(rows, -1), ro.reshape(rows, -1), do.reshape(rows, -1), gt.reshape(rows, -1),
                             x.reshape(rows, d), mx[2], mx[3], mx[4], g_f, *branch_w, rcat, n, min(1024, n))
        moe_sets.append((x1.reshape(bsz, n, d), h2.reshape(bsz, n, d), aff, mx[5]))
        moe_out = _moe(moe_sets, w_exp_gate, w_exp_up, w_exp_down, layer)
        x = moe_out[-1]
        if not last:
            xc = moe_out[0]
    return x
```

```python
import functools
import math

import jax
import jax.numpy as jnp
import numpy as np
from jax import lax
from jax.experimental import pallas as pl
from jax.experimental.pallas import tpu as pltpu

F32 = jnp.float32
BF = jnp.bfloat16
I32 = jnp.int32

GRID_W = 64
HEAD_DIM = 64
ROPE_FREQS_PER_AXIS = HEAD_DIM // 4
ROPE_BASE = 10000.0
EPS = 1e-6
LOG2E = 1.4426950408889634
FNET_GROUP_DIM = 64
N_HEADS = 4
HEAD_V = 128
N_EXPERTS = 16
EC_FACTOR = 2
N_MOD = 6
RET_SUB = 256
MERGE_SUB = 1024
TOKEN_BLOCK = 256
SLOT_WINDOW = 64
LANES = 128
VMEM_LIMIT = 56 * 1024 * 1024

_W_FO, _W_RQ, _W_RK, _W_RV, _W_RG, _W_DQ, _W_DK, _W_DV = 512, 256, 256, 512, 512, 512, 512, 512


def _cparams(sem):
    return pltpu.CompilerParams(dimension_semantics=sem, vmem_limit_bytes=VMEM_LIMIT)


def _sigmoid(v):
    return 1.0 / (1.0 + jnp.exp(-v))


def _nt_dot(a, b):
    return lax.dot_general(a, b, (((1,), (1,)), ((), ())), preferred_element_type=F32)


def _tn_dot(a, b):
    return lax.dot_general(a, b, (((0,), (0,)), ((), ())), preferred_element_type=F32)


def _mod_kernel(c_ref, w_ref, b_ref, o_ref):
    cv = c_ref[...]
    s = cv * _sigmoid(cv)
    o_ref[0] = jnp.dot(s.astype(BF), w_ref[0].astype(BF), preferred_element_type=F32) + b_ref[0]


def _modulation(cvecs, w_mod, b_mod):
    depth, d, wd = w_mod.shape
    rows = cvecs.shape[0]
    tn = 1536
    return pl.pallas_call(
        _mod_kernel,
        grid=(depth, wd // tn),
        in_specs=[pl.BlockSpec((rows, d), lambda l, j: (0, 0)),
                  pl.BlockSpec((1, d, tn), lambda l, j: (l, 0, j)),
                  pl.BlockSpec((1, 1, tn), lambda l, j: (l, 0, j))],
        out_specs=pl.BlockSpec((1, rows, tn), lambda l, j: (l, 0, j)),
        out_shape=jax.ShapeDtypeStruct((depth, rows, wd), F32),
        compiler_params=_cparams(("arbitrary", "arbitrary")),
        name="modulation",
    )(cvecs, w_mod, b_mod.reshape(depth, 1, wd))


def _rope(x, cos_t, sin_t):
    lane = lax.broadcasted_iota(I32, (x.shape[0], LANES), 1)
    first = (lane & 63) < 32
    outs = []
    for j in range(x.shape[1] // LANES):
        xc = x[:, j * LANES:(j + 1) * LANES]
        sw = jnp.where(first, pltpu.roll(xc, LANES - 32, 1), pltpu.roll(xc, 32, 1))
        outs.append(xc * cos_t + sw * sin_t)
    return jnp.concatenate(outs, axis=1)


def _group_rms(x, gm, gain):
    gw = gm.shape[0]
    sq = (x * x).astype(BF)
    ms = jnp.concatenate([jnp.dot(sq[:, c:c + gw], gm, preferred_element_type=F32)
                          for c in range(0, x.shape[1], gw)], axis=1)
    return x * lax.rsqrt(ms + EPS) * gain


def _inproj_kernel(x_ref, shift_ref, scale_ref, g_ref, w_ref, cos_ref, sin_ref, qg_ref, kg_ref, gm_ref,
                   *out_refs, kv_only):
    if kv_only:
        rk_ref, rv_ref, dk_ref, dv_ref = out_refs
    else:
        fo_ref, rq_ref, rk_ref, rv_ref, sg_ref, dq_ref, dk_ref, dv_ref, gt_ref = out_refs
    x = x_ref[...]
    d = x.shape[1]
    ms = jnp.mean(x * x, axis=-1, keepdims=True)
    h = x * lax.rsqrt(ms + EPS) * g_ref[...]
    h = h * (1.0 + scale_ref[0]) + shift_ref[0]
    hb = h.astype(BF)
    cos_t = cos_ref[...]
    sin_t = sin_ref[...]
    gm = gm_ref[...]

    def proj(a, width):
        return jnp.dot(hb, w_ref[:, a:a + width], preferred_element_type=F32)

    a = 0
    if not kv_only:
        fo_ref[...] = proj(a, _W_FO).astype(BF)
    a += _W_FO
    if not kv_only:
        rq_ref[...] = _rope(proj(a, _W_RQ) * (HEAD_DIM ** -0.5), cos_t, sin_t).astype(BF)
    a += _W_RQ
    rk_ref[...] = _rope(proj(a, _W_RK), cos_t, sin_t).astype(BF)
    a += _W_RK
    rv_ref[...] = proj(a, _W_RV).astype(BF)
    a += _W_RV
    if not kv_only:
        rg = proj(a, _W_RG)
        sg_ref[...] = (rg * _sigmoid(rg)).astype(BF)
    a += _W_RG
    if not kv_only:
        dq = _group_rms(proj(a, _W_DQ), gm, qg_ref[...])
        dq_ref[...] = (_rope(dq, cos_t, sin_t) * (HEAD_DIM ** -0.5 * LOG2E)).astype(BF)
    a += _W_DQ
    dk = _group_rms(proj(a, _W_DK), gm, kg_ref[...])
    dk_ref[...] = _rope(dk, cos_t, sin_t).astype(BF)
    a += _W_DK
    dv = proj(a, _W_DV).astype(BF)
    ones = jnp.ones((dv.shape[0], HEAD_V), BF)
    for hh in range(N_HEADS):
        dv_ref[:, 2 * hh * HEAD_V:(2 * hh + 1) * HEAD_V] = dv[:, hh * HEAD_V:(hh + 1) * HEAD_V]
        dv_ref[:, (2 * hh + 1) * HEAD_V:(2 * hh + 2) * HEAD_V] = ones
    a += _W_DV
    if not kv_only:
        for j in range(3):
            gl = proj(a + j * d, d)
            gt_ref[:, j * d:(j + 1) * d] = _sigmoid(gl).astype(BF)


def _in_projection(x2d, shift, scale, g, w_in_bf, cos_t, sin_t, qg, kg, gm, n_seq, tm, kv_only=False):
    rows, d = x2d.shape
    tiles_per_b = n_seq // tm
    nb = shift.shape[0]
    win = w_in_bf.shape[1]
    if kv_only:
        widths = (_W_RK, _W_RV, _W_DK, 2 * _W_DV)
    else:
        widths = (_W_FO, _W_RQ, _W_RK, _W_RV, _W_RG, _W_DQ, _W_DK, 2 * _W_DV, 3 * d)

    def mod_map(i):
        return ((i // tiles_per_b) if nb > 1 else 0, 0, 0)

    def pos_map(i):
        return (i % tiles_per_b, 0)

    const2 = lambda i: (0, 0)
    return pl.pallas_call(
        functools.partial(_inproj_kernel, kv_only=kv_only),
        grid=(rows // tm,),
        in_specs=[pl.BlockSpec((tm, d), lambda i: (i, 0)),
                  pl.BlockSpec((1, 1, d), mod_map),
                  pl.BlockSpec((1, 1, d), mod_map),
                  pl.BlockSpec((1, d), const2),
                  pl.BlockSpec((d, win), const2, pipeline_mode=pl.Buffered(1)),
                  pl.BlockSpec((tm, LANES), pos_map),
                  pl.BlockSpec((tm, LANES), pos_map),
                  pl.BlockSpec((1, _W_DQ), const2),
                  pl.BlockSpec((1, _W_DK), const2),
                  pl.BlockSpec(gm.shape, const2)],
        out_specs=[pl.BlockSpec((tm, w), lambda i: (i, 0)) for w in widths],
        out_shape=[jax.ShapeDtypeStruct((rows, w), BF) for w in widths],
        compiler_params=_cparams(("arbitrary",)),
        name="in_projection",
    )(x2d, shift, scale, g, w_in_bf, cos_t, sin_t, qg, kg, gm)


def _fourier_kernel(*refs):
    _fourier_fold(pl.program_id(1), *refs)
    _fourier_rows(*refs)


def _fourier_fold(step, x_ref, cs_ref, bc_ref, bs_ref, jsh_ref, o_ref, z_ref, eo_ref):
    n, w = x_ref.shape[1], x_ref.shape[2]
    half = n // 2
    gw = bc_ref.shape[0]
    fb = jsh_ref.shape[0]

    @pl.when(step == 0)
    def _():
        x = x_ref[0]
        for g in range(w // gw):
            xs = x[:, g * gw:(g + 1) * gw]
            z_ref[0, :, g * gw:(g + 1) * gw] = jnp.dot(xs, bc_ref[...], preferred_element_type=F32).astype(BF)
            z_ref[1, :, g * gw:(g + 1) * gw] = jnp.dot(xs, bs_ref[...], preferred_element_type=F32).astype(BF)
        jsh = jsh_ref[...]
        n_blocks = half // fb
        for blk in range(n_blocks):
            hi_blk = 2 * n_blocks - 1 - blk
            for t in range(2):
                top = z_ref[t, hi_blk * fb:(hi_blk + 1) * fb, :]
                nxt = jnp.zeros_like(top) if blk == 0 else z_ref[t, (hi_blk + 1) * fb:(hi_blk + 2) * fb, :]
                partner = jnp.dot(jsh, jnp.concatenate([top, nxt], axis=0), preferred_element_type=F32)
                own = z_ref[t, blk * fb:(blk + 1) * fb, :].astype(F32)
                folded = own + partner if t == 0 else own - partner
                eo_ref[t * half + blk * fb:t * half + (blk + 1) * fb, :] = folded.astype(BF)


def _fourier_rows(x_ref, cs_ref, bc_ref, bs_ref, jsh_ref, o_ref, z_ref, eo_ref):
    n = x_ref.shape[1]
    half = n // 2
    tr = o_ref.shape[1]
    y = jnp.dot(cs_ref[...], eo_ref[...], preferred_element_type=F32)
    parity = lax.broadcasted_iota(I32, (tr, 1), 0) & 1
    sign = (1.0 - 2.0 * parity.astype(F32)) * (1.0 / math.sqrt(n))
    o_ref[0] = (y + sign * z_ref[0, half:half + 1, :].astype(F32)).astype(BF)


def _fourier_mix(fo, cs, bdc, bds, jsh, tr):
    b, n, w = fo.shape
    assert n % (2 * jsh.shape[0]) == 0 and tr % 2 == 0
    return pl.pallas_call(
        _fourier_kernel,
        grid=(b, n // tr),
        in_specs=[pl.BlockSpec((1, n, w), lambda i, j: (i, 0, 0)),
                  pl.BlockSpec((tr, n), lambda i, j: (j, 0)),
                  pl.BlockSpec(bdc.shape, lambda i, j: (0, 0)),
                  pl.BlockSpec(bds.shape, lambda i, j: (0, 0)),
                  pl.BlockSpec(jsh.shape, lambda i, j: (0, 0))],
        out_specs=pl.BlockSpec((1, tr, w), lambda i, j: (i, j, 0)),
        out_shape=jax.ShapeDtypeStruct((b, n, w), BF),
        scratch_shapes=[pltpu.VMEM((2, n, w), BF), pltpu.VMEM((n, w), BF)],
        compiler_params=_cparams(("arbitrary", "arbitrary")),
        name="fourier_mix",
    )(fo, cs, bdc, bds, jsh)


def _ret_kernel(*refs, chunk, n_chunks, sub):
    _ret_body(pl.program_id(0), pl.program_id(1), pl.program_id(2), *refs, chunk=chunk, n_chunks=n_chunks, sub=sub)


def _ret_body(b, p, c, lg_ref, q_ref, k_ref, v_ref, sg_ref, sf_ref, sb_ref, o_ref, sfo_ref, sbo_ref,
              acc_ref, tmp_ref, st_ref, dm_ref, qd_ref, qm_ref, kd_ref, cd_ref, *, chunk, n_chunks, sub):
    sf = float(sub)
    n_sub = chunk // sub

    @pl.when((b == 0) & (p == 0) & (c == 0))
    def _tables():
        ii = lax.broadcasted_iota(I32, (sub, sub), 0).astype(F32)
        jj = lax.broadcasted_iota(I32, (sub, sub), 1).astype(F32)
        pos = lax.broadcasted_iota(I32, (sub, LANES), 0).astype(F32)
        lane = lax.broadcasted_iota(I32, (sub, LANES), 1)
        for dr in range(2):
            if dr == 0:
                rel, keep = ii - jj, ii >= jj
                qe, ke = pos + 1.0, (sf - 1.0) - pos
            else:
                rel, keep = jj - ii, jj > ii
                qe, ke = sf - pos, pos
            for hp in range(N_HEADS // 2):
                lg0 = jnp.full((sub, LANES), lg_ref[dr, 2 * hp], F32)
                lg1 = jnp.full((sub, LANES), lg_ref[dr, 2 * hp + 1], F32)
                lgp = jnp.where(lane < HEAD_DIM, lg0, lg1)
                kd_ref[dr, hp] = jnp.exp(lgp * ke)
            for h in range(N_HEADS):
                lgs = lg_ref[dr, h]
                in_head = (lane >= (h % 2) * HEAD_DIM) & (lane < (h % 2 + 1) * HEAD_DIM)
                dm_ref[dr, h] = jnp.where(keep, jnp.exp(jnp.full((sub, sub), lgs, F32) * jnp.maximum(rel, 0.0)), 0.0)
                qd_ref[dr, h] = jnp.where(in_head, jnp.exp(jnp.full((sub, LANES), lgs, F32) * qe), 0.0)
                cd_ref[dr, h] = jnp.exp(jnp.full((LANES, LANES), lgs, F32) * sf)
        for h in range(N_HEADS):
            in_head = (lane >= (h % 2) * HEAD_DIM) & (lane < (h % 2 + 1) * HEAD_DIM)
            qm_ref[h] = jnp.where(in_head, 1.0, 0.0)

    def load_state(src_ref):
        z = jnp.zeros((HEAD_DIM, HEAD_V), F32)
        for h in range(N_HEADS):
            s = src_ref[0, h]
            st_ref[h] = jnp.concatenate([s, z], axis=0) if h % 2 == 0 else jnp.concatenate([z, s], axis=0)

    @pl.when((c == 0) & (p == 0))
    def _():
        load_state(sf_ref)

    @pl.when((c == 0) & (p == 1))
    def _():
        load_state(sb_ref)

    sts = [st_ref[h] for h in range(N_HEADS)]
    for u in range(n_sub):
        r0 = pl.multiple_of(jnp.where(p == 0, u, n_sub - 1 - u) * sub, sub)
        rows = pl.ds(r0, sub)
        q = q_ref[0, rows, :]
        k = k_ref[0, rows, :]
        v = v_ref[0, rows, :]
        outs = []
        for hp in range(N_HEADS // 2):
            qp = q[:, hp * LANES:(hp + 1) * LANES].astype(F32)
            kp = k[:, hp * LANES:(hp + 1) * LANES]
            kdp = (kp.astype(F32) * kd_ref[p, hp]).astype(BF)
            for h in (2 * hp, 2 * hp + 1):
                vh = v[:, h * HEAD_V:(h + 1) * HEAD_V]
                qm = (qp * qm_ref[h]).astype(BF)
                qdq = (qp * qd_ref[p, h]).astype(BF)
                s = _nt_dot(qm, kp) * dm_ref[p, h]
                o = (jnp.dot(s.astype(BF), vh, preferred_element_type=F32)
                     + jnp.dot(qdq, sts[h].astype(BF), preferred_element_type=F32))
                sts[h] = sts[h] * cd_ref[p, h] + _tn_dot(kdp, vh)
                outs.append(o)
        tmp_ref[rows, :] = jnp.concatenate(outs, axis=1)
    for h in range(N_HEADS):
        st_ref[h] = sts[h]
    idx = jnp.where(p == 0, c, n_chunks - 1 - c)

    @pl.when(p == 0)
    def _():
        acc_ref[idx] = tmp_ref[...]

    @pl.when(p == 1)
    def _():
        tot = acc_ref[idx] + tmp_ref[...]
        sg = sg_ref[0].astype(F32)
        ys = []
        for h in range(N_HEADS):
            oh = tot[:, h * HEAD_V:(h + 1) * HEAD_V]
            mu = jnp.mean(oh, axis=-1, keepdims=True)
            dlt = oh - mu
            var = jnp.mean(dlt * dlt, axis=-1, keepdims=True)
            ys.append(dlt * lax.rsqrt(var + EPS))
        o_ref[0] = (jnp.concatenate(ys, axis=1) * sg).astype(BF)

    def store_state(dst_ref):
        for h in range(N_HEADS):
            r0 = (h % 2) * HEAD_DIM
            dst_ref[0, h] = st_ref[h, r0:r0 + HEAD_DIM, :]

    @pl.when((c == n_chunks - 1) & (p == 0))
    def _():
        store_state(sfo_ref)

    @pl.when((c == n_chunks - 1) & (p == 1))
    def _():
        store_state(sbo_ref)


def _retention(rq, rk, rv, sg, decays, s_f, s_b, chunk):
    b, n, _ = rq.shape
    n_chunks = n // chunk
    wv = rv.shape[2]

    def seq_map(i, p, c):
        return (i, jnp.where(p == 0, c, n_chunks - 1 - c), 0)

    def out_map(i, p, c):
        return (i, jnp.where(p == 0, n_chunks - 1, n_chunks - 1 - c), 0)

    st_spec = pl.BlockSpec((1, N_HEADS, HEAD_DIM, HEAD_V), lambda i, p, c: (i, 0, 0, 0))
    st_shape = jax.ShapeDtypeStruct((b, N_HEADS, HEAD_DIM, HEAD_V), F32)
    sub = min(RET_SUB, chunk)
    kern = functools.partial(_ret_kernel, chunk=chunk, n_chunks=n_chunks, sub=sub)
    return pl.pallas_call(
        kern,
        grid=(b, 2, n_chunks),
        in_specs=[pl.BlockSpec(memory_space=pltpu.SMEM),
                  pl.BlockSpec((1, chunk, rq.shape[2]), seq_map),
                  pl.BlockSpec((1, chunk, rk.shape[2]), seq_map),
                  pl.BlockSpec((1, chunk, wv), seq_map),
                  pl.BlockSpec((1, chunk, wv), seq_map),
                  st_spec, st_spec],
        out_specs=[pl.BlockSpec((1, chunk, wv), out_map), st_spec, st_spec],
        out_shape=[jax.ShapeDtypeStruct((b, n, wv), BF), st_shape, st_shape],
        scratch_shapes=[pltpu.VMEM((n_chunks, chunk, wv), F32),
                        pltpu.VMEM((chunk, wv), F32),
                        pltpu.VMEM((N_HEADS, LANES, HEAD_V), F32),
                        pltpu.VMEM((2, N_HEADS, sub, sub), F32),
                        pltpu.VMEM((2, N_HEADS, sub, LANES), F32),
                        pltpu.VMEM((N_HEADS, sub, LANES), F32),
                        pltpu.VMEM((2, N_HEADS // 2, sub, LANES), F32),
                        pltpu.VMEM((2, N_HEADS, LANES, HEAD_V), F32)],
        compiler_params=_cparams(("arbitrary", "arbitrary", "arbitrary")),
        name="retention",
    )(decays, rq, rk, rv, sg, s_f, s_b)


def _dattn_kernel(dl_ref, q_ref, *refs, n_src, lam_init):
    k_refs = refs[:n_src]
    v_refs = refs[n_src:2 * n_src]
    o_ref = refs[2 * n_src]
    dl = dl_ref[...]
    lam = (jnp.exp(jnp.sum(dl[0:1] * dl[1:2], axis=-1, keepdims=True))
           - jnp.exp(jnp.sum(dl[2:3] * dl[3:4], axis=-1, keepdims=True)) + lam_init)
    q = q_ref[0].astype(F32)
    tq = q.shape[0]
    lane = lax.broadcasted_iota(I32, (tq, LANES), 1)
    ys = []
    for h in range(N_HEADS):
        qh = q[:, h * LANES:(h + 1) * LANES]
        qs = jnp.concatenate([jnp.where(lane < HEAD_DIM, qh, 0.0), jnp.where(lane >= HEAD_DIM, qh, 0.0)],
                             axis=0).astype(BF)
        ss = [_nt_dot(qs, kr[0, :, h * LANES:(h + 1) * LANES]) for kr in k_refs]
        m = ss[0].max(axis=-1, keepdims=True)
        for s in ss[1:]:
            m = jnp.maximum(m, s.max(axis=-1, keepdims=True))
        acc = jnp.zeros((2 * tq, 2 * HEAD_V), F32)
        for s, vr in zip(ss, v_refs):
            pexp = jnp.exp2(s - m).astype(BF)
            acc = acc + jnp.dot(pexp, vr[0, :, 2 * h * HEAD_V:(2 * h + 2) * HEAD_V], preferred_element_type=F32)
        o = acc[:, :HEAD_V] / acc[:, HEAD_V:]
        oh = o[:tq] - lam * o[tq:]
        ms = jnp.mean(oh * oh, axis=-1, keepdims=True)
        ys.append(oh * lax.rsqrt(ms + EPS) * (1.0 - lam_init))
    o_ref[0] = jnp.concatenate(ys, axis=1).astype(BF)


def _diff_attention(dq, ks, vs, dlam, lam_init, tq):
    b, n, w = dq.shape
    n_src = len(ks)
    kern = functools.partial(_dattn_kernel, n_src=n_src, lam_init=lam_init)
    kv_specs = [pl.BlockSpec((1, a.shape[1], a.shape[2]), lambda i, j: (i, 0, 0)) for a in (*ks, *vs)]
    return pl.pallas_call(
        kern,
        grid=(b, n // tq),
        in_specs=[pl.BlockSpec(dlam.shape, lambda i, j: (0, 0)),
                  pl.BlockSpec((1, tq, w), lambda i, j: (i, j, 0))] + kv_specs,
        out_specs=pl.BlockSpec((1, tq, w), lambda i, j: (i, j, 0)),
        out_shape=jax.ShapeDtypeStruct((b, n, w), BF),
        compiler_params=_cparams(("arbitrary", "arbitrary")),
        name="diff_attention",
    )(dlam, dq, *ks, *vs)


def _merge_kernel(f_ref, ro_ref, do_ref, gt_ref, x_ref, m2_ref, m3_ref, m4_ref, g_ref,
                  wf_ref, wr_ref, wd_ref, wo_ref, rcat_ref, x1_ref, h2_ref, aff_ref):
    tm, d = x_ref.shape
    st = min(MERGE_SUB, tm)
    for r0 in range(0, tm, st):
        rows = slice(r0, r0 + st)
        t = (gt_ref[rows, 0:d].astype(F32) * jnp.dot(f_ref[rows, :], wf_ref[...], preferred_element_type=F32)
             + gt_ref[rows, d:2 * d].astype(F32) * jnp.dot(ro_ref[rows, :], wr_ref[...], preferred_element_type=F32)
             + gt_ref[rows, 2 * d:3 * d].astype(F32) * jnp.dot(do_ref[rows, :], wd_ref[...],
                                                                preferred_element_type=F32))
        mix = jnp.dot(t.astype(BF), wo_ref[...], preferred_element_type=F32)
        x1 = x_ref[rows, :] + m2_ref[0] * mix
        x1_ref[rows, :] = x1
        ms = jnp.mean(x1 * x1, axis=-1, keepdims=True)
        h2 = x1 * lax.rsqrt(ms + EPS) * g_ref[...]
        h2 = h2 * (1.0 + m4_ref[0]) + m3_ref[0]
        hi = h2.astype(BF)
        lo = (h2 - hi.astype(F32)).astype(BF)
        h2_ref[rows, :] = hi
        l1 = jnp.dot(hi, rcat_ref[...], preferred_element_type=F32)
        lt = l1[:, :LANES] + l1[:, LANES:] + jnp.dot(lo, rcat_ref[:, :LANES], preferred_element_type=F32)
        lane = lax.broadcasted_iota(I32, lt.shape, 1)
        lt = jnp.where(lane < N_EXPERTS, lt, -jnp.inf)
        ex = jnp.exp(lt - lt.max(axis=-1, keepdims=True))
        aff = ex / jnp.sum(ex, axis=-1, keepdims=True)
        aff_ref[0, :, rows] = aff.T[:N_EXPERTS, :]


def _merge(f, ro, do, gt, x2d, m2, m3, m4, g, wf, wr, wd, wo, rcat, n_seq, tm):
    rows, d = x2d.shape
    tiles_per_b = n_seq // tm
    nb = m2.shape[0]
    bsz = rows // n_seq
    ne = N_EXPERTS

    def mod_map(i):
        return ((i // tiles_per_b) if nb > 1 else 0, 0, 0)

    const2 = lambda i: (0, 0)
    row_spec = lambda w: pl.BlockSpec((tm, w), lambda i: (i, 0))
    return pl.pallas_call(
        _merge_kernel,
        grid=(rows // tm,),
        in_specs=[row_spec(f.shape[1]), row_spec(ro.shape[1]), row_spec(do.shape[1]), row_spec(gt.shape[1]),
                  row_spec(d),
                  pl.BlockSpec((1, 1, d), mod_map), pl.BlockSpec((1, 1, d), mod_map), pl.BlockSpec((1, 1, d), mod_map),
                  pl.BlockSpec((1, d), const2),
                  pl.BlockSpec(wf.shape, const2), pl.BlockSpec(wr.shape, const2), pl.BlockSpec(wd.shape, const2),
                  pl.BlockSpec(wo.shape, const2), pl.BlockSpec(rcat.shape, const2)],
        out_specs=[row_spec(d), row_spec(d),
                   pl.BlockSpec((1, ne, tm), lambda i: (i // tiles_per_b, 0, i % tiles_per_b))],
        out_shape=[jax.ShapeDtypeStruct((rows, d), F32), jax.ShapeDtypeStruct((rows, d), BF),
                   jax.ShapeDtypeStruct((bsz, ne, n_seq), F32)],
        compiler_params=_cparams(("arbitrary",)),
        name="merge_router",
    )(f, ro, do, gt, x2d, m2, m3, m4, g, wf, wr, wd, wo, rcat)


def _route_kernel(a_ref, slot_ref, offs_ref, *, cap, blk):
    a = a_ref[...]
    ne, n = a.shape
    capf = float(cap)

    def enough(t):
        return jnp.sum(jnp.where(a >= t, 1.0, 0.0), axis=-1, keepdims=True) >= capf

    tiny = jnp.full((ne, 1), 2.0 ** -126, F32)
    found = enough(tiny)
    cur = tiny
    for step in (64, 32, 16, 8, 4, 2, 1):
        cand = cur * (2.0 ** step)
        cur = jnp.where(enough(cand), cand, cur)
    base = cur

    def mantissa_bit(_, carry):
        cur, stepv = carry
        cand = cur + stepv
        return jnp.where(enough(cand), cand, cur), stepv * 0.5

    cur, ulp = lax.fori_loop(0, 23, mantissa_bit, (cur, base * 0.5))
    lo = jnp.where(found, cur, 0.0)
    hi = jnp.where(found, cur + ulp * 2.0, tiny)

    def refine(_, carry):
        lo, hi = carry
        mid = lo + (hi - lo) * 0.5
        ok = enough(mid)
        return jnp.where(ok, mid, lo), jnp.where(ok, hi, mid)

    lo, hi = lax.fori_loop(0, 24, refine, (lo, hi))
    ri = lax.broadcasted_iota(I32, (blk, blk), 0)
    ci = lax.broadcasted_iota(I32, (blk, blk), 1)
    upper = jnp.where(ri < ci, 1.0, 0.0).astype(BF)

    def excl_cumsum(m):
        carry = jnp.zeros((ne, 1), F32)
        outs = []
        for j in range(n // blk):
            mb = m[:, j * blk:(j + 1) * blk]
            outs.append(jnp.dot(mb.astype(BF), upper, preferred_element_type=F32) + carry)
            carry = carry + jnp.sum(mb, axis=-1, keepdims=True)
        return jnp.concatenate(outs, axis=1)

    gt = a >= hi
    tie = (a >= lo) & (a < hi)
    need = capf - jnp.sum(jnp.where(gt, 1.0, 0.0), axis=-1, keepdims=True)
    sel = gt | (tie & (excl_cumsum(jnp.where(tie, 1.0, 0.0)) < need))
    selm = jnp.where(sel, 1.0, 0.0)
    slot = excl_cumsum(selm)
    slot_ref[...] = jnp.where(sel, slot.astype(I32), -1)
    tok = lax.broadcasted_iota(I32, (n, LANES), 0)
    blk_start = lax.broadcasted_iota(I32, (n, LANES), 1) * blk
    before = jnp.where((tok < blk_start) & (blk_start <= n), 1.0, 0.0).astype(BF)
    offs_ref[...] = jnp.dot(selm.astype(BF), before, preferred_element_type=F32).astype(I32)


def _route(aff_t, cap):
    b, ne, n = aff_t.shape
    kern = functools.partial(_route_kernel, cap=cap, blk=min(TOKEN_BLOCK, n))
    slot, offs = pl.pallas_call(
        kern,
        grid=(1,),
        in_specs=[pl.BlockSpec((b * ne, n), lambda i: (0, 0))],
        out_specs=[pl.BlockSpec((b * ne, n), lambda i: (0, 0)), pl.BlockSpec((b * ne, LANES), lambda i: (0, 0))],
        out_shape=[jax.ShapeDtypeStruct((b * ne, n), I32), jax.ShapeDtypeStruct((b * ne, LANES), I32)],
        compiler_params=_cparams(("arbitrary",)),
        name="route",
    )(aff_t.reshape(b * ne, n))
    return slot.reshape(b, ne, n), offs


def _window_start(off, cap, win):
    return jnp.minimum(lax.shift_left(lax.shift_right_logical(off, 4), 4), cap - win)


def _windows_fit(offs_ref, row0, ne, blk0, nblk, cap, win):
    bad = jnp.int32(0)
    for e in range(ne):
        for jj in range(nblk):
            j = blk0 + jj
            off = offs_ref[row0 + e, j]
            end = offs_ref[row0 + e, j + 1]
            bad = bad | (end - _window_start(off, cap, win) > win).astype(I32)
    return bad == 0


def _gather_kernel(offs_ref, slot_ref, h_ref, xs_ref, *, cap, win, tb):
    ne, n = slot_ref.shape[1], slot_ref.shape[3]
    nblk = n // tb
    row0 = pl.program_id(0) * ne
    fits = _windows_fit(offs_ref, row0, ne, 0, nblk, cap, win)

    @pl.when(fits)
    def _windowed():
        xs_ref[...] = jnp.zeros(xs_ref.shape, BF)
        rows = lax.broadcasted_iota(I32, (win, tb), 0)
        for j in range(nblk):
            starts, lhs = [], []
            for e in range(ne):
                w0 = pl.multiple_of(_window_start(offs_ref[row0 + e, j], cap, win), 16)
                starts.append(w0)
                lhs.append(jnp.where(slot_ref[0, e, :, j * tb:(j + 1) * tb] == rows + w0, 1.0, 0.0).astype(BF))
            got = jnp.dot(jnp.concatenate(lhs, axis=0), h_ref[0, j * tb:(j + 1) * tb, :],
                          preferred_element_type=F32)
            for e in range(ne):
                dst = xs_ref.at[0, e, pl.ds(starts[e], win), :]
                dst[...] = (dst[...].astype(F32) + got[e * win:(e + 1) * win]).astype(BF)

    @pl.when(jnp.logical_not(fits))
    def _dense():
        sub = lax.broadcasted_iota(I32, (cap, n), 0)
        hb = h_ref[0]
        for e in range(ne):
            onehot = jnp.where(sub == slot_ref[0, e], 1.0, 0.0).astype(BF)
            xs_ref[0, e] = jnp.dot(onehot, hb, preferred_element_type=F32).astype(BF)


def _gather(slot, offs, h2, cap):
    b, ne, n = slot.shape
    d = h2.shape[2]
    tb = min(TOKEN_BLOCK, n)
    kern = functools.partial(_gather_kernel, cap=cap, win=min(SLOT_WINDOW, cap), tb=tb)
    grid_spec = pltpu.PrefetchScalarGridSpec(
        num_scalar_prefetch=1,
        grid=(b,),
        in_specs=[pl.BlockSpec((1, ne, 1, n), lambda i, offs: (i, 0, 0, 0)),
                  pl.BlockSpec((1, n, d), lambda i, offs: (i, 0, 0))],
        out_specs=pl.BlockSpec((1, ne, cap, d), lambda i, offs: (i, 0, 0, 0)))
    return pl.pallas_call(
        kern,
        grid_spec=grid_spec,
        out_shape=jax.ShapeDtypeStruct((b, ne, cap, d), BF),
        compiler_params=_cparams(("arbitrary",)),
        name="moe_gather",
    )(offs, slot.reshape(b, ne, 1, n), h2)


def _ffn_kernel(*refs, n_src, n_fc):
    xs_refs = refs[:n_src]
    wg_ref, wu_ref, wd_ref = refs[n_src:n_src + 3]
    ye_refs = refs[n_src + 3:2 * n_src + 3]
    x_ref, hid_ref = refs[2 * n_src + 3:]
    j = pl.program_id(1)
    fc = wg_ref.shape[3]
    row_spans = []
    r0 = 0
    for xr in xs_refs:
        rows = xr.shape[0] * xr.shape[2]
        row_spans.append((r0, rows))
        r0 += rows

    @pl.when(j == 0)
    def _():
        for xr, (s0, rows) in zip(xs_refs, row_spans):
            x_ref[s0:s0 + rows, :] = xr[:, 0].reshape(rows, xr.shape[3])

    @pl.when(j < n_fc)
    def _():
        x = x_ref[...]
        g = jnp.dot(x, wg_ref[0, 0].astype(BF), preferred_element_type=F32)
        u = jnp.dot(x, wu_ref[0, 0].astype(BF), preferred_element_type=F32)
        hid_ref[j] = (g * _sigmoid(g) * u).astype(BF)

    @pl.when(j >= n_fc)
    def _():
        y = jnp.dot(hid_ref[0], wd_ref[0, 0, 0:fc, :].astype(BF), preferred_element_type=F32)
        for k in range(1, n_fc):
            y = y + jnp.dot(hid_ref[k], wd_ref[0, 0, k * fc:(k + 1) * fc, :].astype(BF), preferred_element_type=F32)
        for yr, (s0, rows) in zip(ye_refs, row_spans):
            yr[:, 0] = y[s0:s0 + rows].reshape(yr.shape[0], yr.shape[2], yr.shape[3]).astype(BF)


def _expert_ffn(xs_list, w_gate, w_up, w_down, layer, fchunk, ochunk):
    ne, d = xs_list[0].shape[1], xs_list[0].shape[3]
    f = w_gate.shape[3]
    n_fc, n_oc = f // fchunk, d // ochunk
    n_src = len(xs_list)
    total_rows = sum(x.shape[0] * x.shape[2] for x in xs_list)
    kern = functools.partial(_ffn_kernel, n_src=n_src, n_fc=n_fc)
    up_map = lambda e, j: (layer, e, 0, jnp.minimum(j, n_fc - 1))
    down_map = lambda e, j: (layer, e, 0, jnp.maximum(j - n_fc, 0))
    out_map = lambda e, j: (0, e, 0, jnp.maximum(j - n_fc, 0))
    outs = pl.pallas_call(
        kern,
        grid=(ne, n_fc + n_oc),
        in_specs=[pl.BlockSpec((x.shape[0], 1, x.shape[2], d), lambda e, j: (0, e, 0, 0), pipeline_mode=pl.Buffered(1))
                  for x in xs_list]
        + [pl.BlockSpec((1, 1, d, fchunk), up_map),
           pl.BlockSpec((1, 1, d, fchunk), up_map),
           pl.BlockSpec((1, 1, f, ochunk), down_map)],
        out_specs=[pl.BlockSpec((x.shape[0], 1, x.shape[2], ochunk), out_map) for x in xs_list],
        out_shape=[jax.ShapeDtypeStruct(x.shape, BF) for x in xs_list],
        scratch_shapes=[pltpu.VMEM((total_rows, d), BF), pltpu.VMEM((n_fc, total_rows, fchunk), BF)],
        compiler_params=_cparams(("arbitrary", "arbitrary")),
        name="expert_ffn",
    )(*xs_list, w_gate, w_up, w_down)
    return list(outs)


def _combine_kernel(offs_ref, x1_ref, m5_ref, slot_ref, aff_ref, ye_ref, o_ref, ywin_ref, *, cap, win, tb):
    ne, tn = slot_ref.shape[1], slot_ref.shape[3]
    row0 = pl.program_id(0) * ne
    blk0 = pl.program_id(1) * (tn // tb)
    fits = _windows_fit(offs_ref, row0, ne, blk0, tn // tb, cap, win)

    @pl.when(fits)
    def _windowed():
        rows = lax.broadcasted_iota(I32, (win, tb), 0)
        for jj in range(tn // tb):
            tok = slice(jj * tb, (jj + 1) * tb)
            pieces = []
            for e in range(ne):
                w0 = pl.multiple_of(_window_start(offs_ref[row0 + e, blk0 + jj], cap, win), 16)
                ywin_ref[e * win:(e + 1) * win, :] = ye_ref[0, e, pl.ds(w0, win), :]
                pieces.append(jnp.where(slot_ref[0, e, :, tok] == rows + w0, aff_ref[0, e, :, tok], 0.0).astype(BF))
            acc = _tn_dot(jnp.concatenate(pieces, axis=0), ywin_ref[...])
            o_ref[0, tok, :] = x1_ref[0, tok, :] + m5_ref[0] * acc

    @pl.when(jnp.logical_not(fits))
    def _dense():
        rows = lax.broadcasted_iota(I32, (cap, tn), 0)
        acc = jnp.zeros(x1_ref.shape[1:], F32)
        for e in range(ne):
            weighted = jnp.where(slot_ref[0, e] == rows, aff_ref[0, e], 0.0).astype(BF)
            acc = acc + _tn_dot(weighted, ye_ref[0, e])
        o_ref[0] = x1_ref[0] + m5_ref[0] * acc


def _combine(x1, m5, slot, aff_t, offs, ye, tn):
    b, n, d = x1.shape
    ne, cap = ye.shape[1], ye.shape[2]
    nb = m5.shape[0]
    tb = min(TOKEN_BLOCK, n)
    win = min(SLOT_WINDOW, cap)
    kern = functools.partial(_combine_kernel, cap=cap, win=win, tb=tb)
    grid_spec = pltpu.PrefetchScalarGridSpec(
        num_scalar_prefetch=1,
        grid=(b, n // tn),
        in_specs=[pl.BlockSpec((1, tn, d), lambda i, j, offs: (i, j, 0)),
                  pl.BlockSpec((1, 1, d), lambda i, j, offs: (i if nb > 1 else 0, 0, 0)),
                  pl.BlockSpec((1, ne, 1, tn), lambda i, j, offs: (i, 0, 0, j)),
                  pl.BlockSpec((1, ne, 1, tn), lambda i, j, offs: (i, 0, 0, j)),
                  pl.BlockSpec((1, ne, cap, d), lambda i, j, offs: (i, 0, 0, 0))],
        out_specs=pl.BlockSpec((1, tn, d), lambda i, j, offs: (i, j, 0)),
        scratch_shapes=[pltpu.VMEM((ne * win, d), BF)])
    return pl.pallas_call(
        kern,
        grid_spec=grid_spec,
        out_shape=jax.ShapeDtypeStruct((b, n, d), F32),
        compiler_params=_cparams(("arbitrary", "arbitrary")),
        name="moe_combine",
    )(offs, x1, m5, slot.reshape(b, ne, 1, n), aff_t.reshape(b, ne, 1, n), ye)


def _rope_tables(n, use_rope):
    if not use_rope:
        return jnp.ones((n, LANES), F32), jnp.zeros((n, LANES), F32)
    rows = n // GRID_W
    row = jnp.repeat(jnp.arange(rows, dtype=F32), GRID_W)
    col = jnp.tile(jnp.arange(GRID_W, dtype=F32), rows)
    inv = ROPE_BASE ** (-jnp.arange(ROPE_FREQS_PER_AXIS, dtype=F32) / ROPE_FREQS_PER_AXIS)
    ang = jnp.concatenate([row[:, None] * inv, col[:, None] * inv], axis=-1)
    cos, sin = jnp.cos(ang), jnp.sin(ang)
    cos_t = jnp.tile(cos, (1, LANES // (HEAD_DIM // 2)))
    sin_t = jnp.tile(jnp.concatenate([-sin, sin], axis=-1), (1, LANES // HEAD_DIM))
    return cos_t, sin_t


def _dft_tables(n):
    n0 = 64
    n1 = n // n0
    k = np.arange(n, dtype=np.int64)[:, None]
    ang1 = jnp.asarray(2.0 * np.pi * ((k * np.arange(n1 // 2)[None, :]) % n1) / n1, F32)
    ang0 = jnp.asarray(2.0 * np.pi * ((k * np.arange(n0)[None, :]) % n) / n, F32)
    c1, s1, c0, s0 = jnp.cos(ang1), jnp.sin(ang1), jnp.cos(ang0), jnp.sin(ang0)
    scale = 1.0 / math.sqrt(n)
    cos_n = (c1[:, :, None] * c0[:, None, :] - s1[:, :, None] * s0[:, None, :]).reshape(n, n // 2) * scale
    sin_n = (s1[:, :, None] * c0[:, None, :] + c1[:, :, None] * s0[:, None, :]).reshape(n, n // 2) * scale
    return jnp.concatenate([cos_n, -sin_n], axis=1).astype(BF)


def _fold_matrix(fb):
    i = np.arange(fb)[:, None]
    s = np.arange(2 * fb)[None, :]
    return jnp.asarray(np.where(s == fb - i, 1.0, 0.0), BF)


def _group_dft_tables(width):
    g = FNET_GROUP_DIM
    idx = np.arange(width)
    same = (idx[:, None] // g) == (idx[None, :] // g)
    ang = 2.0 * np.pi * (((idx[:, None] % g) * (idx[None, :] % g)) % g) / g
    scale = 1.0 / math.sqrt(g)
    bdc = np.where(same, np.cos(ang), 0.0) * scale
    bds = np.where(same, np.sin(ang), 0.0) * scale
    return jnp.asarray(bdc, F32).astype(BF), jnp.asarray(bds, F32).astype(BF)


def _group_mean_matrix(width):
    idx = np.arange(width)
    same = (idx[:, None] // HEAD_DIM) == (idx[None, :] // HEAD_DIM)
    return jnp.asarray(np.where(same, 1.0 / HEAD_DIM, 0.0), BF)


def _mixer_inputs(x3, mods, g_attn_l, w_in_bf, tables, qg, kg, gm, kv_only=False):
    b, n, d = x3.shape
    tm = min(512, n)
    outs = _in_projection(x3.reshape(b * n, d), mods[0], mods[1], g_attn_l, w_in_bf, tables[0], tables[1],
                          qg, kg, gm, n, tm, kv_only)
    return [o.reshape(b, n, o.shape[1]) for o in outs]


def _moe(sets, w_gate, w_up, w_down, layer):
    slots, offss, xss = [], [], []
    for x1, h2, aff_t, _ in sets:
        cap = EC_FACTOR * x1.shape[1] // N_EXPERTS
        slot, offs = _route(aff_t, cap)
        offs = offs[:, :16]
        slots.append(slot)
        offss.append(offs)
        xss.append(_gather(slot, offs, h2, cap))
    yes = _expert_ffn(xss, w_gate, w_up, w_down, layer, 512, 512)
    return [_combine(x1, m5, slot, aff_t, offs, ye, min(512, x1.shape[1]))
            for (x1, _, aff_t, m5), slot, offs, ye in zip(sets, slots, offss, yes)]


def kernel(x, c, ctx, c_ctx, w_mod, b_mod, g_attn, g_ffn, w_in, ret_decay, diff_qn, diff_kn, diff_lambda,
           w_fnet_o, w_ret_o, w_diff_o, w_out, w_router, w_exp_gate, w_exp_up, w_exp_down):
    bsz, n, d = x.shape
    n_ctx = ctx.shape[1]
    depth = w_mod.shape[0]

    pad = (-(bsz + 1)) % 8
    cvecs = jnp.concatenate([c, c_ctx[None, :], jnp.zeros((pad, d), F32)], axis=0)
    mods = _modulation(cvecs, w_mod, b_mod)

    rope_lat = _rope_tables(n, True)
    rope_ctx = _rope_tables(n_ctx, False)
    cs_lat, cs_ctx = _dft_tables(n), _dft_tables(n_ctx)
    bdc, bds = _group_dft_tables(2 * LANES)
    jsh = _fold_matrix(LANES)
    gm = _group_mean_matrix(2 * LANES)
    zero_state = jnp.zeros((bsz, N_HEADS, HEAD_DIM, HEAD_V), F32)
    ret_chunk = 1024

    xc = ctx
    for layer in range(depth):
        last = layer == depth - 1
        lam_init = 0.8 - 0.6 * math.exp(-0.3 * layer)
        mx = [mods[layer, :bsz, j * d:(j + 1) * d].reshape(bsz, 1, d) for j in range(N_MOD)]
        mc = [mods[layer, bsz:bsz + 1, j * d:(j + 1) * d].reshape(1, 1, d) for j in range(N_MOD)]
        w_in_bf = w_in[layer].astype(BF)
        qg = jnp.tile(diff_qn[layer], _W_DQ // HEAD_DIM)[None, :]
        kg = jnp.tile(diff_kn[layer], _W_DK // HEAD_DIM)[None, :]
        g_a = g_attn[layer][None, :]
        g_f = g_ffn[layer][None, :]
        decays = ret_decay[layer]
        dlam = diff_lambda[layer]
        branch_w = (w_fnet_o[layer].astype(BF), w_ret_o[layer].astype(BF), w_diff_o[layer].astype(BF),
                    w_out[layer].astype(BF))
        wr32 = w_router[layer]
        wr_hi = wr32.astype(BF)
        wr_lo = (wr32 - wr_hi.astype(F32)).astype(BF)
        lane_pad = jnp.zeros((d, LANES - N_EXPERTS), BF)
        rcat = jnp.concatenate([wr_hi, lane_pad, wr_lo, lane_pad], axis=1)
        moe_sets = []

        if last:
            rk_c, rv_c, dk_c, dv_c = _mixer_inputs(xc, mc, g_a, w_in_bf, rope_ctx, qg, kg, gm, kv_only=True)
            rq_c, sg_c = rk_c, rv_c
        else:
            fo_c, rq_c, rk_c, rv_c, sg_c, dq_c, dk_c, dv_c, gt_c = _mixer_inputs(xc, mc, g_a, w_in_bf, rope_ctx,
                                                                               qg, kg, gm)
        ro_c, s_f, s_b = _retention(rq_c, rk_c, rv_c, sg_c, decays, zero_state, zero_state, min(ret_chunk, n_ctx))
        if not last:
            f_c = _fourier_mix(fo_c, cs_ctx, bdc, bds, jsh, min(512, n_ctx))
            do_c = _diff_attention(dq_c, [dk_c], [dv_c], dlam, lam_init, min(256, n_ctx))
            rows_c = bsz * n_ctx
            x1_c, h2_c, aff_c = _merge(f_c.reshape(rows_c, -1), ro_c.reshape(rows_c, -1), do_c.reshape(rows_c, -1),
                                       gt_c.reshape(rows_c, -1), xc.reshape(rows_c, d), mc[2], mc[3], mc[4], g_f,
                                       *branch_w, rcat, n_ctx, min(512, n_ctx))
            moe_sets.append((x1_c.reshape(bsz, n_ctx, d), h2_c.reshape(bsz, n_ctx, d), aff_c, mc[5]))

        fo, rq, rk, rv, sg, dq, dk, dv, gt = _mixer_inputs(x, mx, g_a, w_in_bf, rope_lat, qg, kg, gm)
        ro, _, _ = _retention(rq, rk, rv, sg, decays, s_f, s_b, min(ret_chunk, n))
        f = _fourier_mix(fo, cs_lat, bdc, bds, jsh, min(512, n))
        do = _diff_attention(dq, [dk, dk_c], [dv, dv_c], dlam, lam_init, min(512, n))
        rows = bsz * n
        x1, h2, aff = _merge(f.reshape(rows, -1), ro.reshape(rows, -1), do.reshape(rows, -1), gt.reshape(rows, -1),
                             x.reshape(rows, d), mx[2], mx[3], mx[4], g_f, *branch_w, rcat, n, min(1024, n))
        moe_sets.append((x1.reshape(bsz, n, d), h2.reshape(bsz, n, d), aff, mx[5]))
        moe_out = _moe(moe_sets, w_exp_gate, w_exp_up, w_exp_down, layer)
        x = moe_out[-1]
        if not last:
            xc = moe_out[0]
    return x
```

```python
import functools
import math

import jax
import jax.numpy as jnp
import numpy as np
from jax import lax
from jax.experimental import pallas as pl
from jax.experimental.pallas import tpu as pltpu

F32 = jnp.float32
BF = jnp.bfloat16
I32 = jnp.int32

GRID_W = 64
HEAD_DIM = 64
ROPE_FREQS_PER_AXIS = HEAD_DIM // 4
ROPE_BASE = 10000.0
EPS = 1e-6
LOG2E = 1.4426950408889634
FNET_GROUP_DIM = 64
N_HEADS = 4
HEAD_V = 128
N_EXPERTS = 16
EC_FACTOR = 2
N_MOD = 6
RET_SUB = 256
MERGE_SUB = 1024
TOKEN_BLOCK = 256
SLOT_WINDOW = 64
LANES = 128
VMEM_LIMIT = 56 * 1024 * 1024

_W_FO, _W_RQ, _W_RK, _W_RV, _W_RG, _W_DQ, _W_DK, _W_DV = 512, 256, 256, 512, 512, 512, 512, 512


def _cparams(sem):
    return pltpu.CompilerParams(dimension_semantics=sem, vmem_limit_bytes=VMEM_LIMIT)


def _sigmoid(v):
    return 1.0 / (1.0 + jnp.exp(-v))


def _nt_dot(a, b):
    return lax.dot_general(a, b, (((1,), (1,)), ((), ())), preferred_element_type=F32)


def _tn_dot(a, b):
    return lax.dot_general(a, b, (((0,), (0,)), ((), ())), preferred_element_type=F32)


def _mod_kernel(c_ref, w_ref, b_ref, o_ref):
    cv = c_ref[...]
    s = cv * _sigmoid(cv)
    o_ref[0] = jnp.dot(s.astype(BF), w_ref[0].astype(BF), preferred_element_type=F32) + b_ref[0]


def _modulation(cvecs, w_mod, b_mod):
    depth, d, wd = w_mod.shape
    rows = cvecs.shape[0]
    tn = 1536
    return pl.pallas_call(
        _mod_kernel,
        grid=(depth, wd // tn),
        in_specs=[pl.BlockSpec((rows, d), lambda l, j: (0, 0)),
                  pl.BlockSpec((1, d, tn), lambda l, j: (l, 0, j)),
                  pl.BlockSpec((1, 1, tn), lambda l, j: (l, 0, j))],
        out_specs=pl.BlockSpec((1, rows, tn), lambda l, j: (l, 0, j)),
        out_shape=jax.ShapeDtypeStruct((depth, rows, wd), F32),
        compiler_params=_cparams(("arbitrary", "arbitrary")),
        name="modulation",
    )(cvecs, w_mod, b_mod.reshape(depth, 1, wd))


def _rope(x, cos_t, sin_t):
    lane = lax.broadcasted_iota(I32, (x.shape[0], LANES), 1)
    first = (lane & 63) < 32
    outs = []
    for j in range(x.shape[1] // LANES):
        xc = x[:, j * LANES:(j + 1) * LANES]
        sw = jnp.where(first, pltpu.roll(xc, LANES - 32, 1), pltpu.roll(xc, 32, 1))
        outs.append(xc * cos_t + sw * sin_t)
    return jnp.concatenate(outs, axis=1)


def _group_rms(x, gm, gain):
    gw = gm.shape[0]
    sq = (x * x).astype(BF)
    ms = jnp.concatenate([jnp.dot(sq[:, c:c + gw], gm, preferred_element_type=F32)
                          for c in range(0, x.shape[1], gw)], axis=1)
    return x * lax.rsqrt(ms + EPS) * gain


def _inproj_kernel(x_ref, shift_ref, scale_ref, g_ref, w_ref, cos_ref, sin_ref, qg_ref, kg_ref, gm_ref,
                   *out_refs, kv_only):
    if kv_only:
        rk_ref, rv_ref, dk_ref, dv_ref = out_refs
    else:
        fo_ref, rq_ref, rk_ref, rv_ref, sg_ref, dq_ref, dk_ref, dv_ref, gt_ref = out_refs
    x = x_ref[...]
    d = x.shape[1]
    ms = jnp.mean(x * x, axis=-1, keepdims=True)
    h = x * lax.rsqrt(ms + EPS) * g_ref[...]
    h = h * (1.0 + scale_ref[0]) + shift_ref[0]
    hb = h.astype(BF)
    cos_t = cos_ref[...]
    sin_t = sin_ref[...]
    gm = gm_ref[...]

    def proj(a, width):
        return jnp.dot(hb, w_ref[:, a:a + width], preferred_element_type=F32)

    a = 0
    if not kv_only:
        fo_ref[...] = proj(a, _W_FO).astype(BF)
    a += _W_FO
    if not kv_only:
        rq_ref[...] = _rope(proj(a, _W_RQ) * (HEAD_DIM ** -0.5), cos_t, sin_t).astype(BF)
    a += _W_RQ
    rk_ref[...] = _rope(proj(a, _W_RK), cos_t, sin_t).astype(BF)
    a += _W_RK
    rv_ref[...] = proj(a, _W_RV).astype(BF)
    a += _W_RV
    if not kv_only:
        rg = proj(a, _W_RG)
        sg_ref[...] = (rg * _sigmoid(rg)).astype(BF)
    a += _W_RG
    if not kv_only:
        dq = _group_rms(proj(a, _W_DQ), gm, qg_ref[...])
        dq_ref[...] = (_rope(dq, cos_t, sin_t) * (HEAD_DIM ** -0.5 * LOG2E)).astype(BF)
    a += _W_DQ
    dk = _group_rms(proj(a, _W_DK), gm, kg_ref[...])
    dk_ref[...] = _rope(dk, cos_t, sin_t).astype(BF)
    a += _W_DK
    dv = proj(a, _W_DV).astype(BF)
    ones = jnp.ones((dv.shape[0], HEAD_V), BF)
    for hh in range(N_HEADS):
        dv_ref[:, 2 * hh * HEAD_V:(2 * hh + 1) * HEAD_V] = dv[:, hh * HEAD_V:(hh + 1) * HEAD_V]
        dv_ref[:, (2 * hh + 1) * HEAD_V:(2 * hh + 2) * HEAD_V] = ones
    a += _W_DV
    if not kv_only:
        for j in range(3):
            gl = proj(a + j * d, d)
            gt_ref[:, j * d:(j + 1) * d] = _sigmoid(gl).astype(BF)


def _in_projection(x2d, shift, scale, g, w_in_bf, cos_t, sin_t, qg, kg, gm, n_seq, tm, kv_only=False):
    rows, d = x2d.shape
    tiles_per_b = n_seq // tm
    nb = shift.shape[0]
    win = w_in_bf.shape[1]
    if kv_only:
        widths = (_W_RK, _W_RV, _W_DK, 2 * _W_DV)
    else:
        widths = (_W_FO, _W_RQ, _W_RK, _W_RV, _W_RG, _W_DQ, _W_DK, 2 * _W_DV, 3 * d)

    def mod_map(i):
        return ((i // tiles_per_b) if nb > 1 else 0, 0, 0)

    def pos_map(i):
        return (i % tiles_per_b, 0)

    const2 = lambda i: (0, 0)
    return pl.pallas_call(
        functools.partial(_inproj_kernel, kv_only=kv_only),
        grid=(rows // tm,),
        in_specs=[pl.BlockSpec((tm, d), lambda i: (i, 0)),
                  pl.BlockSpec((1, 1, d), mod_map),
                  pl.BlockSpec((1, 1, d), mod_map),
                  pl.BlockSpec((1, d), const2),
                  pl.BlockSpec((d, win), const2, pipeline_mode=pl.Buffered(1)),
                  pl.BlockSpec((tm, LANES), pos_map),
                  pl.BlockSpec((tm, LANES), pos_map),
                  pl.BlockSpec((1, _W_DQ), const2),
                  pl.BlockSpec((1, _W_DK), const2),
                  pl.BlockSpec(gm.shape, const2)],
        out_specs=[pl.BlockSpec((tm, w), lambda i: (i, 0)) for w in widths],
        out_shape=[jax.ShapeDtypeStruct((rows, w), BF) for w in widths],
        compiler_params=_cparams(("arbitrary",)),
        name="in_projection",
    )(x2d, shift, scale, g, w_in_bf, cos_t, sin_t, qg, kg, gm)


def _fourier_kernel(*refs):
    _fourier_fold(pl.program_id(1), *refs)
    _fourier_rows(*refs)


def _fourier_fold(step, x_ref, cs_ref, bc_ref, bs_ref, jsh_ref, o_ref, z_ref, eo_ref):
    n, w = x_ref.shape[1], x_ref.shape[2]
    half = n // 2
    gw = bc_ref.shape[0]
    fb = jsh_ref.shape[0]

    @pl.when(step == 0)
    def _():
        x = x_ref[0]
        for g in range(w // gw):
            xs = x[:, g * gw:(g + 1) * gw]
            z_ref[0, :, g * gw:(g + 1) * gw] = jnp.dot(xs, bc_ref[...], preferred_element_type=F32).astype(BF)
            z_ref[1, :, g * gw:(g + 1) * gw] = jnp.dot(xs, bs_ref[...], preferred_element_type=F32).astype(BF)
        jsh = jsh_ref[...]
        n_blocks = half // fb
        for blk in range(n_blocks):
            hi_blk = 2 * n_blocks - 1 - blk
            for t in range(2):
                top = z_ref[t, hi_blk * fb:(hi_blk + 1) * fb, :]
                nxt = jnp.zeros_like(top) if blk == 0 else z_ref[t, (hi_blk + 1) * fb:(hi_blk + 2) * fb, :]
                partner = jnp.dot(jsh, jnp.concatenate([top, nxt], axis=0), preferred_element_type=F32)
                own = z_ref[t, blk * fb:(blk + 1) * fb, :].astype(F32)
                folded = own + partner if t == 0 else own - partner
                eo_ref[t * half + blk * fb:t * half + (blk + 1) * fb, :] = folded.astype(BF)


def _fourier_rows(x_ref, cs_ref, bc_ref, bs_ref, jsh_ref, o_ref, z_ref, eo_ref):
    n = x_ref.shape[1]
    half = n // 2
    tr = o_ref.shape[1]
    y = jnp.dot(cs_ref[...], eo_ref[...], preferred_element_type=F32)
    parity = lax.broadcasted_iota(I32, (tr, 1), 0) & 1
    sign = (1.0 - 2.0 * parity.astype(F32)) * (1.0 / math.sqrt(n))
    o_ref[0] = (y + sign * z_ref[0, half:half + 1, :].astype(F32)).astype(BF)


def _fourier_mix(fo, cs, bdc, bds, jsh, tr):
    b, n, w = fo.shape
    assert n % (2 * jsh.shape[0]) == 0 and tr % 2 == 0
    return pl.pallas_call(
        _fourier_kernel,
        grid=(b, n // tr),
        in_specs=[pl.BlockSpec((1, n, w), lambda i, j: (i, 0, 0)),
                  pl.BlockSpec((tr, n), lambda i, j: (j, 0)),
                  pl.BlockSpec(bdc.shape, lambda i, j: (0, 0)),
                  pl.BlockSpec(bds.shape, lambda i, j: (0, 0)),
                  pl.BlockSpec(jsh.shape, lambda i, j: (0, 0))],
        out_specs=pl.BlockSpec((1, tr, w), lambda i, j: (i, j, 0)),
        out_shape=jax.ShapeDtypeStruct((b, n, w), BF),
        scratch_shapes=[pltpu.VMEM((2, n, w), BF), pltpu.VMEM((n, w), BF)],
        compiler_params=_cparams(("arbitrary", "arbitrary")),
        name="fourier_mix",
    )(fo, cs, bdc, bds, jsh)


def _ret_kernel(*refs, chunk, n_chunks, sub):
    _ret_body(pl.program_id(0), pl.program_id(1), pl.program_id(2), *refs, chunk=chunk, n_chunks=n_chunks, sub=sub)


def _ret_body(b, p, c, lg_ref, q_ref, k_ref, v_ref, sg_ref, sf_ref, sb_ref, o_ref, sfo_ref, sbo_ref,
              acc_ref, tmp_ref, st_ref, dm_ref, qd_ref, qm_ref, kd_ref, cd_ref, *, chunk, n_chunks, sub):
    sf = float(sub)
    n_sub = chunk // sub

    @pl.when((b == 0) & (p == 0) & (c == 0))
    def _tables():
        ii = lax.broadcasted_iota(I32, (sub, sub), 0).astype(F32)
        jj = lax.broadcasted_iota(I32, (sub, sub), 1).astype(F32)
        pos = lax.broadcasted_iota(I32, (sub, LANES), 0).astype(F32)
        lane = lax.broadcasted_iota(I32, (sub, LANES), 1)
        for dr in range(2):
            if dr == 0:
                rel, keep = ii - jj, ii >= jj
                qe, ke = pos + 1.0, (sf - 1.0) - pos
            else:
                rel, keep = jj - ii, jj > ii
                qe, ke = sf - pos, pos
            for hp in range(N_HEADS // 2):
                lg0 = jnp.full((sub, LANES), lg_ref[dr, 2 * hp], F32)
                lg1 = jnp.full((sub, LANES), lg_ref[dr, 2 * hp + 1], F32)
                lgp = jnp.where(lane < HEAD_DIM, lg0, lg1)
                kd_ref[dr, hp] = jnp.exp(lgp * ke)
            for h in range(N_HEADS):
                lgs = lg_ref[dr, h]
                in_head = (lane >= (h % 2) * HEAD_DIM) & (lane < (h % 2 + 1) * HEAD_DIM)
                dm_ref[dr, h] = jnp.where(keep, jnp.exp(jnp.full((sub, sub), lgs, F32) * jnp.maximum(rel, 0.0)), 0.0)
                qd_ref[dr, h] = jnp.where(in_head, jnp.exp(jnp.full((sub, LANES), lgs, F32) * qe), 0.0)
                cd_ref[dr, h] = jnp.exp(jnp.full((LANES, LANES), lgs, F32) * sf)
        for h in range(N_HEADS):
            in_head = (lane >= (h % 2) * HEAD_DIM) & (lane < (h % 2 + 1) * HEAD_DIM)
            qm_ref[h] = jnp.where(in_head, 1.0, 0.0)

    def load_state(src_ref):
        z = jnp.zeros((HEAD_DIM, HEAD_V), F32)
        for h in range(N_HEADS):
            s = src_ref[0, h]
            st_ref[h] = jnp.concatenate([s, z], axis=0) if h % 2 == 0 else jnp.concatenate([z, s], axis=0)

    @pl.when((c == 0) & (p == 0))
    def _():
        load_state(sf_ref)

    @pl.when((c == 0) & (p == 1))
    def _():
        load_state(sb_ref)

    sts = [st_ref[h] for h in range(N_HEADS)]
    for u in range(n_sub):
        r0 = pl.multiple_of(jnp.where(p == 0, u, n_sub - 1 - u) * sub, sub)
        rows = pl.ds(r0, sub)
        q = q_ref[0, rows, :]
        k = k_ref[0, rows, :]
        v = v_ref[0, rows, :]
        outs = []
        for hp in range(N_HEADS // 2):
            qp = q[:, hp * LANES:(hp + 1) * LANES].astype(F32)
            kp = k[:, hp * LANES:(hp + 1) * LANES]
            kdp = (kp.astype(F32) * kd_ref[p, hp]).astype(BF)
            for h in (2 * hp, 2 * hp + 1):
                vh = v[:, h * HEAD_V:(h + 1) * HEAD_V]
                qm = (qp * qm_ref[h]).astype(BF)
                qdq = (qp * qd_ref[p, h]).astype(BF)
                s = _nt_dot(qm, kp) * dm_ref[p, h]
                o = (jnp.dot(s.astype(BF), vh, preferred_element_type=F32)
                     + jnp.dot(qdq, sts[h].astype(BF), preferred_element_type=F32))
                sts[h] = sts[h] * cd_ref[p, h] + _tn_dot(kdp, vh)
                outs.append(o)
        tmp_ref[rows, :] = jnp.concatenate(outs, axis=1)
    for h in range(N_HEADS):
        st_ref[h] = sts[h]
    idx = jnp.where(p == 0, c, n_chunks - 1 - c)

    @pl.when(p == 0)
    def _():
        acc_ref[idx] = tmp_ref[...]

    @pl.when(p == 1)
    def _():
        tot = acc_ref[idx] + tmp_ref[...]
        sg = sg_ref[0].astype(F32)
        ys = []
        for h in range(N_HEADS):
            oh = tot[:, h * HEAD_V:(h + 1) * HEAD_V]
            mu = jnp.mean(oh, axis=-1, keepdims=True)
            dlt = oh - mu
            var = jnp.mean(dlt * dlt, axis=-1, keepdims=True)
            ys.append(dlt * lax.rsqrt(var + EPS))
        o_ref[0] = (jnp.concatenate(ys, axis=1) * sg).astype(BF)

    def store_state(dst_ref):
        for h in range(N_HEADS):
            r0 = (h % 2) * HEAD_DIM
            dst_ref[0, h] = st_ref[h, r0:r0 + HEAD_DIM, :]

    @pl.when((c == n_chunks - 1) & (p == 0))
    def _():
        store_state(sfo_ref)

    @pl.when((c == n_chunks - 1) & (p == 1))
    def _():
        store_state(sbo_ref)


def _retention(rq, rk, rv, sg, decays, s_f, s_b, chunk):
    b, n, _ = rq.shape
    n_chunks = n // chunk
    wv = rv.shape[2]

    def seq_map(i, p, c):
        return (i, jnp.where(p == 0, c, n_chunks - 1 - c), 0)

    def out_map(i, p, c):
        return (i, jnp.where(p == 0, n_chunks - 1, n_chunks - 1 - c), 0)

    st_spec = pl.BlockSpec((1, N_HEADS, HEAD_DIM, HEAD_V), lambda i, p, c: (i, 0, 0, 0))
    st_shape = jax.ShapeDtypeStruct((b, N_HEADS, HEAD_DIM, HEAD_V), F32)
    sub = min(RET_SUB, chunk)
    kern = functools.partial(_ret_kernel, chunk=chunk, n_chunks=n_chunks, sub=sub)
    return pl.pallas_call(
        kern,
        grid=(b, 2, n_chunks),
        in_specs=[pl.BlockSpec(memory_space=pltpu.SMEM),
                  pl.BlockSpec((1, chunk, rq.shape[2]), seq_map),
                  pl.BlockSpec((1, chunk, rk.shape[2]), seq_map),
                  pl.BlockSpec((1, chunk, wv), seq_map),
                  pl.BlockSpec((1, chunk, wv), seq_map),
                  st_spec, st_spec],
        out_specs=[pl.BlockSpec((1, chunk, wv), out_map), st_spec, st_spec],
        out_shape=[jax.ShapeDtypeStruct((b, n, wv), BF), st_shape, st_shape],
        scratch_shapes=[pltpu.VMEM((n_chunks, chunk, wv), F32),
                        pltpu.VMEM((chunk, wv), F32),
                        pltpu.VMEM((N_HEADS, LANES, HEAD_V), F32),
                        pltpu.VMEM((2, N_HEADS, sub, sub), F32),
                        pltpu.VMEM((2, N_HEADS, sub, LANES), F32),
                        pltpu.VMEM((N_HEADS, sub, LANES), F32),
                        pltpu.VMEM((2, N_HEADS // 2, sub, LANES), F32),
                        pltpu.VMEM((2, N_HEADS, LANES, HEAD_V), F32)],
        compiler_params=_cparams(("arbitrary", "arbitrary", "arbitrary")),
        name="retention",
    )(decays, rq, rk, rv, sg, s_f, s_b)


def _dattn_kernel(dl_ref, q_ref, *refs, n_src, lam_init, sub_q):
    k_refs = refs[:n_src]
    v_refs = refs[n_src:2 * n_src]
    o_ref = refs[2 * n_src]
    dl = dl_ref[...]
    lam = (jnp.exp(jnp.sum(dl[0:1] * dl[1:2], axis=-1, keepdims=True))
           - jnp.exp(jnp.sum(dl[2:3] * dl[3:4], axis=-1, keepdims=True)) + lam_init)
    tq = min(sub_q, q_ref.shape[1])
    lane = lax.broadcasted_iota(I32, (tq, LANES), 1)
    for r0 in range(0, q_ref.shape[1], tq):
        q = q_ref[0, r0:r0 + tq, :].astype(F32)
        ys = []
        for h in range(N_HEADS):
            qh = q[:, h * LANES:(h + 1) * LANES]
            qs = jnp.concatenate([jnp.where(lane < HEAD_DIM, qh, 0.0), jnp.where(lane >= HEAD_DIM, qh, 0.0)],
                                 axis=0).astype(BF)
            ss = [_nt_dot(qs, kr[0, :, h * LANES:(h + 1) * LANES]) for kr in k_refs]
            m = ss[0].max(axis=-1, keepdims=True)
            for s in ss[1:]:
                m = jnp.maximum(m, s.max(axis=-1, keepdims=True))
            acc = jnp.zeros((2 * tq, 2 * HEAD_V), F32)
            for s, vr in zip(ss, v_refs):
                pexp = jnp.exp2(s - m).astype(BF)
                acc = acc + jnp.dot(pexp, vr[0, :, 2 * h * HEAD_V:(2 * h + 2) * HEAD_V],
                                    preferred_element_type=F32)
            o = acc[:, :HEAD_V] / acc[:, HEAD_V:]
            oh = o[:tq] - lam * o[tq:]
            ms = jnp.mean(oh * oh, axis=-1, keepdims=True)
            ys.append(oh * lax.rsqrt(ms + EPS) * (1.0 - lam_init))
        o_ref[0, r0:r0 + tq, :] = jnp.concatenate(ys, axis=1).astype(BF)


def _diff_attention(dq, ks, vs, dlam, lam_init, tq, sub_q=512):
    b, n, w = dq.shape
    n_src = len(ks)
    kern = functools.partial(_dattn_kernel, n_src=n_src, lam_init=lam_init, sub_q=sub_q)
    kv_specs = [pl.BlockSpec((1, a.shape[1], a.shape[2]), lambda i, j: (i, 0, 0)) for a in (*ks, *vs)]
    return pl.pallas_call(
        kern,
        grid=(b, n // tq),
        in_specs=[pl.BlockSpec(dlam.shape, lambda i, j: (0, 0)),
                  pl.BlockSpec((1, tq, w), lambda i, j: (i, j, 0))] + kv_specs,
        out_specs=pl.BlockSpec((1, tq, w), lambda i, j: (i, j, 0)),
        out_shape=jax.ShapeDtypeStruct((b, n, w), BF),
        compiler_params=_cparams(("arbitrary", "arbitrary")),
        name="diff_attention",
    )(dlam, dq, *ks, *vs)


def _merge_kernel(f_ref, ro_ref, do_ref, gt_ref, x_ref, m2_ref, m3_ref, m4_ref, g_ref,
                  wf_ref, wr_ref, wd_ref, wo_ref, rcat_ref, x1_ref, h2_ref, aff_ref):
    tm, d = x_ref.shape
    st = min(MERGE_SUB, tm)
    for r0 in range(0, tm, st):
        rows = slice(r0, r0 + st)
        t = (gt_ref[rows, 0:d].astype(F32) * jnp.dot(f_ref[rows, :], wf_ref[...], preferred_element_type=F32)
             + gt_ref[rows, d:2 * d].astype(F32) * jnp.dot(ro_ref[rows, :], wr_ref[...], preferred_element_type=F32)
             + gt_ref[rows, 2 * d:3 * d].astype(F32) * jnp.dot(do_ref[rows, :], wd_ref[...],
                                                                preferred_element_type=F32))
        mix = jnp.dot(t.astype(BF), wo_ref[...], preferred_element_type=F32)
        x1 = x_ref[rows, :] + m2_ref[0] * mix
        x1_ref[rows, :] = x1
        ms = jnp.mean(x1 * x1, axis=-1, keepdims=True)
        h2 = x1 * lax.rsqrt(ms + EPS) * g_ref[...]
        h2 = h2 * (1.0 + m4_ref[0]) + m3_ref[0]
        hi = h2.astype(BF)
        lo = (h2 - hi.astype(F32)).astype(BF)
        h2_ref[rows, :] = hi
        l1 = jnp.dot(hi, rcat_ref[...], preferred_element_type=F32)
        lt = l1[:, :LANES] + l1[:, LANES:] + jnp.dot(lo, rcat_ref[:, :LANES], preferred_element_type=F32)
        lane = lax.broadcasted_iota(I32, lt.shape, 1)
        lt = jnp.where(lane < N_EXPERTS, lt, -jnp.inf)
        ex = jnp.exp(lt - lt.max(axis=-1, keepdims=True))
        aff = ex / jnp.sum(ex, axis=-1, keepdims=True)
        aff_ref[0, :, rows] = aff.T[:N_EXPERTS, :]


def _merge(f, ro, do, gt, x2d, m2, m3, m4, g, wf, wr, wd, wo, rcat, n_seq, tm):
    rows, d = x2d.shape
    tiles_per_b = n_seq // tm
    nb = m2.shape[0]
    bsz = rows // n_seq
    ne = N_EXPERTS

    def mod_map(i):
        return ((i // tiles_per_b) if nb > 1 else 0, 0, 0)

    const2 = lambda i: (0, 0)
    row_spec = lambda w: pl.BlockSpec((tm, w), lambda i: (i, 0))
    return pl.pallas_call(
        _merge_kernel,
        grid=(rows // tm,),
        in_specs=[row_spec(f.shape[1]), row_spec(ro.shape[1]), row_spec(do.shape[1]), row_spec(gt.shape[1]),
                  row_spec(d),
                  pl.BlockSpec((1, 1, d), mod_map), pl.BlockSpec((1, 1, d), mod_map), pl.BlockSpec((1, 1, d), mod_map),
                  pl.BlockSpec((1, d), const2),
                  pl.BlockSpec(wf.shape, const2), pl.BlockSpec(wr.shape, const2), pl.BlockSpec(wd.shape, const2),
                  pl.BlockSpec(wo.shape, const2), pl.BlockSpec(rcat.shape, const2)],
        out_specs=[row_spec(d), row_spec(d),
                   pl.BlockSpec((1, ne, tm), lambda i: (i // tiles_per_b, 0, i % tiles_per_b))],
        out_shape=[jax.ShapeDtypeStruct((rows, d), F32), jax.ShapeDtypeStruct((rows, d), BF),
                   jax.ShapeDtypeStruct((bsz, ne, n_seq), F32)],
        compiler_params=_cparams(("arbitrary",)),
        name="merge_router",
    )(f, ro, do, gt, x2d, m2, m3, m4, g, wf, wr, wd, wo, rcat)


def _route_kernel(a_ref, slot_ref, offs_ref, *, cap, blk):
    a = a_ref[...]
    ne, n = a.shape
    capf = float(cap)

    def enough(t):
        return jnp.sum(jnp.where(a >= t, 1.0, 0.0), axis=-1, keepdims=True) >= capf

    tiny = jnp.full((ne, 1), 2.0 ** -126, F32)
    found = enough(tiny)
    cur = tiny
    for step in (64, 32, 16, 8, 4, 2, 1):
        cand = cur * (2.0 ** step)
        cur = jnp.where(enough(cand), cand, cur)
    base = cur

    def mantissa_bit(_, carry):
        cur, stepv = carry
        cand = cur + stepv
        return jnp.where(enough(cand), cand, cur), stepv * 0.5

    cur, ulp = lax.fori_loop(0, 23, mantissa_bit, (cur, base * 0.5))
    lo = jnp.where(found, cur, 0.0)
    hi = jnp.where(found, cur + ulp * 2.0, tiny)

    def refine(_, carry):
        lo, hi = carry
        mid = lo + (hi - lo) * 0.5
        ok = enough(mid)
        return jnp.where(ok, mid, lo), jnp.where(ok, hi, mid)

    lo, hi = lax.fori_loop(0, 24, refine, (lo, hi))
    ri = lax.broadcasted_iota(I32, (blk, blk), 0)
    ci = lax.broadcasted_iota(I32, (blk, blk), 1)
    upper = jnp.where(ri < ci, 1.0, 0.0).astype(BF)

    def excl_cumsum(m):
        carry = jnp.zeros((ne, 1), F32)
        outs = []
        for j in range(n // blk):
            mb = m[:, j * blk:(j + 1) * blk]
            outs.append(jnp.dot(mb.astype(BF), upper, preferred_element_type=F32) + carry)
            carry = carry + jnp.sum(mb, axis=-1, keepdims=True)
        return jnp.concatenate(outs, axis=1)

    gt = a >= hi
    tie = (a >= lo) & (a < hi)
    need = capf - jnp.sum(jnp.where(gt, 1.0, 0.0), axis=-1, keepdims=True)
    sel = gt | (tie & (excl_cumsum(jnp.where(tie, 1.0, 0.0)) < need))
    selm = jnp.where(sel, 1.0, 0.0)
    slot = excl_cumsum(selm)
    slot_ref[...] = jnp.where(sel, slot.astype(I32), -1)
    tok = lax.broadcasted_iota(I32, (n, LANES), 0)
    blk_start = lax.broadcasted_iota(I32, (n, LANES), 1) * blk
    before = jnp.where((tok < blk_start) & (blk_start <= n), 1.0, 0.0).astype(BF)
    offs_ref[...] = jnp.dot(selm.astype(BF), before, preferred_element_type=F32).astype(I32)


def _route(aff_t, cap):
    b, ne, n = aff_t.shape
    kern = functools.partial(_route_kernel, cap=cap, blk=min(TOKEN_BLOCK, n))
    slot, offs = pl.pallas_call(
        kern,
        grid=(1,),
        in_specs=[pl.BlockSpec((b * ne, n), lambda i: (0, 0))],
        out_specs=[pl.BlockSpec((b * ne, n), lambda i: (0, 0)), pl.BlockSpec((b * ne, LANES), lambda i: (0, 0))],
        out_shape=[jax.ShapeDtypeStruct((b * ne, n), I32), jax.ShapeDtypeStruct((b * ne, LANES), I32)],
        compiler_params=_cparams(("arbitrary",)),
        name="route",
    )(aff_t.reshape(b * ne, n))
    return slot.reshape(b, ne, n), offs


def _window_start(off, cap, win):
    return jnp.minimum(lax.shift_left(lax.shift_right_logical(off, 4), 4), cap - win)


def _windows_fit(offs_ref, row0, ne, blk0, nblk, cap, win):
    bad = jnp.int32(0)
    for e in range(ne):
        for jj in range(nblk):
            j = blk0 + jj
            off = offs_ref[row0 + e, j]
            end = offs_ref[row0 + e, j + 1]
            bad = bad | (end - _window_start(off, cap, win) > win).astype(I32)
    return bad == 0


def _gather_kernel(offs_ref, slot_ref, h_ref, xs_ref, *, cap, win, tb):
    ne, n = slot_ref.shape[1], slot_ref.shape[3]
    nblk = n // tb
    row0 = pl.program_id(0) * ne
    fits = _windows_fit(offs_ref, row0, ne, 0, nblk, cap, win)

    @pl.when(fits)
    def _windowed():
        xs_ref[...] = jnp.zeros(xs_ref.shape, BF)
        rows = lax.broadcasted_iota(I32, (win, tb), 0)
        for j in range(nblk):
            starts, lhs = [], []
            for e in range(ne):
                w0 = pl.multiple_of(_window_start(offs_ref[row0 + e, j], cap, win), 16)
                starts.append(w0)
                lhs.append(jnp.where(slot_ref[0, e, :, j * tb:(j + 1) * tb] == rows + w0, 1.0, 0.0).astype(BF))
            got = jnp.dot(jnp.concatenate(lhs, axis=0), h_ref[0, j * tb:(j + 1) * tb, :],
                          preferred_element_type=F32)
            for e in range(ne):
                dst = xs_ref.at[0, e, pl.ds(starts[e], win), :]
                dst[...] = (dst[...].astype(F32) + got[e * win:(e + 1) * win]).astype(BF)

    @pl.when(jnp.logical_not(fits))
    def _dense():
        sub = lax.broadcasted_iota(I32, (cap, n), 0)
        hb = h_ref[0]
        for e in range(ne):
            onehot = jnp.where(sub == slot_ref[0, e], 1.0, 0.0).astype(BF)
            xs_ref[0, e] = jnp.dot(onehot, hb, preferred_element_type=F32).astype(BF)


def _gather(slot, offs, h2, cap):
    b, ne, n = slot.shape
    d = h2.shape[2]
    tb = min(TOKEN_BLOCK, n)
    kern = functools.partial(_gather_kernel, cap=cap, win=min(SLOT_WINDOW, cap), tb=tb)
    grid_spec = pltpu.PrefetchScalarGridSpec(
        num_scalar_prefetch=1,
        grid=(b,),
        in_specs=[pl.BlockSpec((1, ne, 1, n), lambda i, offs: (i, 0, 0, 0)),
                  pl.BlockSpec((1, n, d), lambda i, offs: (i, 0, 0))],
        out_specs=pl.BlockSpec((1, ne, cap, d), lambda i, offs: (i, 0, 0, 0)))
    return pl.pallas_call(
        kern,
        grid_spec=grid_spec,
        out_shape=jax.ShapeDtypeStruct((b, ne, cap, d), BF),
        compiler_params=_cparams(("arbitrary",)),
        name="moe_gather",
    )(offs, slot.reshape(b, ne, 1, n), h2)


def _ffn_kernel(*refs, n_src, n_fc):
    xs_refs = refs[:n_src]
    wg_ref, wu_ref, wd_ref = refs[n_src:n_src + 3]
    ye_refs = refs[n_src + 3:2 * n_src + 3]
    x_ref, hid_ref = refs[2 * n_src + 3:]
    j = pl.program_id(1)
    fc = wg_ref.shape[3]
    row_spans = []
    r0 = 0
    for xr in xs_refs:
        rows = xr.shape[0] * xr.shape[2]
        row_spans.append((r0, rows))
        r0 += rows

    @pl.when(j == 0)
    def _():
        for xr, (s0, rows) in zip(xs_refs, row_spans):
            x_ref[s0:s0 + rows, :] = xr[:, 0].reshape(rows, xr.shape[3])

    @pl.when(j < n_fc)
    def _():
        x = x_ref[...]
        g = jnp.dot(x, wg_ref[0, 0].astype(BF), preferred_element_type=F32)
        u = jnp.dot(x, wu_ref[0, 0].astype(BF), preferred_element_type=F32)
        hid_ref[j] = (g * _sigmoid(g) * u).astype(BF)

    @pl.when(j >= n_fc)
    def _():
        y = jnp.dot(hid_ref[0], wd_ref[0, 0, 0:fc, :].astype(BF), preferred_element_type=F32)
        for k in range(1, n_fc):
            y = y + jnp.dot(hid_ref[k], wd_ref[0, 0, k * fc:(k + 1) * fc, :].astype(BF), preferred_element_type=F32)
        for yr, (s0, rows) in zip(ye_refs, row_spans):
            yr[:, 0] = y[s0:s0 + rows].reshape(yr.shape[0], yr.shape[2], yr.shape[3]).astype(BF)


def _expert_ffn(xs_list, w_gate, w_up, w_down, layer, fchunk, ochunk):
    ne, d = xs_list[0].shape[1], xs_list[0].shape[3]
    f = w_gate.shape[3]
    n_fc, n_oc = f // fchunk, d // ochunk
    n_src = len(xs_list)
    total_rows = sum(x.shape[0] * x.shape[2] for x in xs_list)
    kern = functools.partial(_ffn_kernel, n_src=n_src, n_fc=n_fc)
    up_map = lambda e, j: (layer, e, 0, jnp.minimum(j, n_fc - 1))
    down_map = lambda e, j: (layer, e, 0, jnp.maximum(j - n_fc, 0))
    out_map = lambda e, j: (0, e, 0, jnp.maximum(j - n_fc, 0))
    outs = pl.pallas_call(
        kern,
        grid=(ne, n_fc + n_oc),
        in_specs=[pl.BlockSpec((x.shape[0], 1, x.shape[2], d), lambda e, j: (0, e, 0, 0), pipeline_mode=pl.Buffered(1))
                  for x in xs_list]
        + [pl.BlockSpec((1, 1, d, fchunk), up_map),
           pl.BlockSpec((1, 1, d, fchunk), up_map),
           pl.BlockSpec((1, 1, f, ochunk), down_map)],
        out_specs=[pl.BlockSpec((x.shape[0], 1, x.shape[2], ochunk), out_map) for x in xs_list],
        out_shape=[jax.ShapeDtypeStruct(x.shape, BF) for x in xs_list],
        scratch_shapes=[pltpu.VMEM((total_rows, d), BF), pltpu.VMEM((n_fc, total_rows, fchunk), BF)],
        compiler_params=_cparams(("arbitrary", "arbitrary")),
        name="expert_ffn",
    )(*xs_list, w_gate, w_up, w_down)
    return list(outs)


def _combine_kernel(offs_ref, x1_ref, m5_ref, slot_ref, aff_ref, ye_ref, o_ref, ywin_ref, *, cap, win, tb):
    ne, tn = slot_ref.shape[1], slot_ref.shape[3]
    row0 = pl.program_id(0) * ne
    blk0 = pl.program_id(1) * (tn // tb)
    fits = _windows_fit(offs_ref, row0, ne, blk0, tn // tb, cap, win)

    @pl.when(fits)
    def _windowed():
        rows = lax.broadcasted_iota(I32, (win, tb), 0)
        for jj in range(tn // tb):
            tok = slice(jj * tb, (jj + 1) * tb)
            pieces = []
            for e in range(ne):
                w0 = pl.multiple_of(_window_start(offs_ref[row0 + e, blk0 + jj], cap, win), 16)
                ywin_ref[e * win:(e + 1) * win, :] = ye_ref[0, e, pl.ds(w0, win), :]
                pieces.append(jnp.where(slot_ref[0, e, :, tok] == rows + w0, aff_ref[0, e, :, tok], 0.0).astype(BF))
            acc = _tn_dot(jnp.concatenate(pieces, axis=0), ywin_ref[...])
            o_ref[0, tok, :] = x1_ref[0, tok, :] + m5_ref[0] * acc

    @pl.when(jnp.logical_not(fits))
    def _dense():
        rows = lax.broadcasted_iota(I32, (cap, tn), 0)
        acc = jnp.zeros(x1_ref.shape[1:], F32)
        for e in range(ne):
            weighted = jnp.where(slot_ref[0, e] == rows, aff_ref[0, e], 0.0).astype(BF)
            acc = acc + _tn_dot(weighted, ye_ref[0, e])
        o_ref[0] = x1_ref[0] + m5_ref[0] * acc


def _combine(x1, m5, slot, aff_t, offs, ye, tn):
    b, n, d = x1.shape
    ne, cap = ye.shape[1], ye.shape[2]
    nb = m5.shape[0]
    tb = min(TOKEN_BLOCK, n)
    win = min(SLOT_WINDOW, cap)
    kern = functools.partial(_combine_kernel, cap=cap, win=win, tb=tb)
    grid_spec = pltpu.PrefetchScalarGridSpec(
        num_scalar_prefetch=1,
        grid=(b, n // tn),
        in_specs=[pl.BlockSpec((1, tn, d), lambda i, j, offs: (i, j, 0)),
                  pl.BlockSpec((1, 1, d), lambda i, j, offs: (i if nb > 1 else 0, 0, 0)),
                  pl.BlockSpec((1, ne, 1, tn), lambda i, j, offs: (i, 0, 0, j)),
                  pl.BlockSpec((1, ne, 1, tn), lambda i, j, offs: (i, 0, 0, j)),
                  pl.BlockSpec((1, ne, cap, d), lambda i, j, offs: (i, 0, 0, 0))],
        out_specs=pl.BlockSpec((1, tn, d), lambda i, j, offs: (i, j, 0)),
        scratch_shapes=[pltpu.VMEM((ne * win, d), BF)])
    return pl.pallas_call(
        kern,
        grid_spec=grid_spec,
        out_shape=jax.ShapeDtypeStruct((b, n, d), F32),
        compiler_params=_cparams(("arbitrary", "arbitrary")),
        name="moe_combine",
    )(offs, x1, m5, slot.reshape(b, ne, 1, n), aff_t.reshape(b, ne, 1, n), ye)


def _rope_tables(n, use_rope):
    if not use_rope:
        return jnp.ones((n, LANES), F32), jnp.zeros((n, LANES), F32)
    rows = n // GRID_W
    row = jnp.repeat(jnp.arange(rows, dtype=F32), GRID_W)
    col = jnp.tile(jnp.arange(GRID_W, dtype=F32), rows)
    inv = ROPE_BASE ** (-jnp.arange(ROPE_FREQS_PER_AXIS, dtype=F32) / ROPE_FREQS_PER_AXIS)
    ang = jnp.concatenate([row[:, None] * inv, col[:, None] * inv], axis=-1)
    cos, sin = jnp.cos(ang), jnp.sin(ang)
    cos_t = jnp.tile(cos, (1, LANES // (HEAD_DIM // 2)))
    sin_t = jnp.tile(jnp.concatenate([-sin, sin], axis=-1), (1, LANES // HEAD_DIM))
    return cos_t, sin_t


def _dft_tables(n):
    n0 = 64
    n1 = n // n0
    k = np.arange(n, dtype=np.int64)[:, None]
    ang1 = jnp.asarray(2.0 * np.pi * ((k * np.arange(n1 // 2)[None, :]) % n1) / n1, F32)
    ang0 = jnp.asarray(2.0 * np.pi * ((k * np.arange(n0)[None, :]) % n) / n, F32)
    c1, s1, c0, s0 = jnp.cos(ang1), jnp.sin(ang1), jnp.cos(ang0), jnp.sin(ang0)
    scale = 1.0 / math.sqrt(n)
    cos_n = (c1[:, :, None] * c0[:, None, :] - s1[:, :, None] * s0[:, None, :]).reshape(n, n // 2) * scale
    sin_n = (s1[:, :, None] * c0[:, None, :] + c1[:, :, None] * s0[:, None, :]).reshape(n, n // 2) * scale
    return jnp.concatenate([cos_n, -sin_n], axis=1).astype(BF)


def _fold_matrix(fb):
    i = np.arange(fb)[:, None]
    s = np.arange(2 * fb)[None, :]
    return jnp.asarray(np.where(s == fb - i, 1.0, 0.0), BF)


def _group_dft_tables(width):
    g = FNET_GROUP_DIM
    idx = np.arange(width)
    same = (idx[:, None] // g) == (idx[None, :] // g)
    ang = 2.0 * np.pi * (((idx[:, None] % g) * (idx[None, :] % g)) % g) / g
    scale = 1.0 / math.sqrt(g)
    bdc = np.where(same, np.cos(ang), 0.0) * scale
    bds = np.where(same, np.sin(ang), 0.0) * scale
    return jnp.asarray(bdc, F32).astype(BF), jnp.asarray(bds, F32).astype(BF)


def _group_mean_matrix(width):
    idx = np.arange(width)
    same = (idx[:, None] // HEAD_DIM) == (idx[None, :] // HEAD_DIM)
    return jnp.asarray(np.where(same, 1.0 / HEAD_DIM, 0.0), BF)


def _mixer_inputs(x3, mods, g_attn_l, w_in_bf, tables, qg, kg, gm, kv_only=False):
    b, n, d = x3.shape
    tm = min(512, n)
    outs = _in_projection(x3.reshape(b * n, d), mods[0], mods[1], g_attn_l, w_in_bf, tables[0], tables[1],
                          qg, kg, gm, n, tm, kv_only)
    return [o.reshape(b, n, o.shape[1]) for o in outs]


def _moe(sets, w_gate, w_up, w_down, layer):
    slots, offss, xss = [], [], []
    for x1, h2, aff_t, _ in sets:
        cap = EC_FACTOR * x1.shape[1] // N_EXPERTS
        slot, offs = _route(aff_t, cap)
        offs = offs[:, :16]
        slots.append(slot)
        offss.append(offs)
        xss.append(_gather(slot, offs, h2, cap))
    yes = _expert_ffn(xss, w_gate, w_up, w_down, layer, 512, 512)
    return [_combine(x1, m5, slot, aff_t, offs, ye, min(512, x1.shape[1]))
            for (x1, _, aff_t, m5), slot, offs, ye in zip(sets, slots, offss, yes)]


def kernel(x, c, ctx, c_ctx, w_mod, b_mod, g_attn, g_ffn, w_in, ret_decay, diff_qn, diff_kn, diff_lambda,
           w_fnet_o, w_ret_o, w_diff_o, w_out, w_router, w_exp_gate, w_exp_up, w_exp_down):
    bsz, n, d = x.shape
    n_ctx = ctx.shape[1]
    depth = w_mod.shape[0]

    pad = (-(bsz + 1)) % 8
    cvecs = jnp.concatenate([c, c_ctx[None, :], jnp.zeros((pad, d), F32)], axis=0)
    mods = _modulation(cvecs, w_mod, b_mod)

    rope_lat = _rope_tables(n, True)
    rope_ctx = _rope_tables(n_ctx, False)
    cs_lat, cs_ctx = _dft_tables(n), _dft_tables(n_ctx)
    bdc, bds = _group_dft_tables(2 * LANES)
    jsh = _fold_matrix(LANES)
    gm = _group_mean_matrix(2 * LANES)
    zero_state = jnp.zeros((bsz, N_HEADS, HEAD_DIM, HEAD_V), F32)
    ret_chunk = 1024

    xc = ctx
    for layer in range(depth):
        last = layer == depth - 1
        lam_init = 0.8 - 0.6 * math.exp(-0.3 * layer)
        mx = [mods[layer, :bsz, j * d:(j + 1) * d].reshape(bsz, 1, d) for j in range(N_MOD)]
        mc = [mods[layer, bsz:bsz + 1, j * d:(j + 1) * d].reshape(1, 1, d) for j in range(N_MOD)]
        w_in_bf = w_in[layer].astype(BF)
        qg = jnp.tile(diff_qn[layer], _W_DQ // HEAD_DIM)[None, :]
        kg = jnp.tile(diff_kn[layer], _W_DK // HEAD_DIM)[None, :]
        g_a = g_attn[layer][None, :]
        g_f = g_ffn[layer][None, :]
        decays = ret_decay[layer]
        dlam = diff_lambda[layer]
        branch_w = (w_fnet_o[layer].astype(BF), w_ret_o[layer].astype(BF), w_diff_o[layer].astype(BF),
                    w_out[layer].astype(BF))
        wr32 = w_router[layer]
        wr_hi = wr32.astype(BF)
        wr_lo = (wr32 - wr_hi.astype(F32)).astype(BF)
        lane_pad = jnp.zeros((d, LANES - N_EXPERTS), BF)
        rcat = jnp.concatenate([wr_hi, lane_pad, wr_lo, lane_pad], axis=1)
        moe_sets = []

        if last:
            rk_c, rv_c, dk_c, dv_c = _mixer_inputs(xc, mc, g_a, w_in_bf, rope_ctx, qg, kg, gm, kv_only=True)
            rq_c, sg_c = rk_c, rv_c
        else:
            fo_c, rq_c, rk_c, rv_c, sg_c, dq_c, dk_c, dv_c, gt_c = _mixer_inputs(xc, mc, g_a, w_in_bf, rope_ctx,
                                                                               qg, kg, gm)
        ro_c, s_f, s_b = _retention(rq_c, rk_c, rv_c, sg_c, decays, zero_state, zero_state, min(ret_chunk, n_ctx))
        if not last:
            f_c = _fourier_mix(fo_c, cs_ctx, bdc, bds, jsh, min(512, n_ctx))
            do_c = _diff_attention(dq_c, [dk_c], [dv_c], dlam, lam_init, min(256, n_ctx))
            rows_c = bsz * n_ctx
            x1_c, h2_c, aff_c = _merge(f_c.reshape(rows_c, -1), ro_c.reshape(rows_c, -1), do_c.reshape(rows_c, -1),
                                       gt_c.reshape(rows_c, -1), xc.reshape(rows_c, d), mc[2], mc[3], mc[4], g_f,
                                       *branch_w, rcat, n_ctx, min(512, n_ctx))
            moe_sets.append((x1_c.reshape(bsz, n_ctx, d), h2_c.reshape(bsz, n_ctx, d), aff_c, mc[5]))

        fo, rq, rk, rv, sg, dq, dk, dv, gt = _mixer_inputs(x, mx, g_a, w_in_bf, rope_lat, qg, kg, gm)
        ro, _, _ = _retention(rq, rk, rv, sg, decays, s_f, s_b, min(ret_chunk, n))
        f = _fourier_mix(fo, cs_lat, bdc, bds, jsh, min(512, n))
        do = _diff_attention(dq, [dk, dk_c], [dv, dv_c], dlam, lam_init, min(1024, n))
        rows = bsz * n
        x1, h2, aff = _merge(f.reshape(rows, -1), ro.reshape(rows, -1), do.reshape(rows, -1), gt.reshape(rows, -1),
                             x.reshape(rows, d), mx[2], mx[3], mx[4], g_f, *branch_w, rcat, n, min(1024, n))
        moe_sets.append((x1.reshape(bsz, n, d), h2.reshape(bsz, n, d), aff, mx[5]))
        moe_out = _moe(moe_sets, w_exp_gate, w_exp_up, w_exp_down, layer)
        x = moe_out[-1]
        if not last:
            xc = moe_out[0]
    return x
```

```python
import functools
import math

import jax
import jax.numpy as jnp
import numpy as np
from jax import lax
from jax.experimental import pallas as pl
from jax.experimental.pallas import tpu as pltpu

F32 = jnp.float32
BF = jnp.bfloat16
I32 = jnp.int32

GRID_W = 64
HEAD_DIM = 64
ROPE_FREQS_PER_AXIS = HEAD_DIM // 4
ROPE_BASE = 10000.0
EPS = 1e-6
LOG2E = 1.4426950408889634
FNET_GROUP_DIM = 64
N_HEADS = 4
HEAD_V = 128
N_EXPERTS = 16
EC_FACTOR = 2
N_MOD = 6
RET_SUB = 256
MERGE_SUB = 1024
TOKEN_BLOCK = 256
SLOT_WINDOW = 64
LANES = 128
VMEM_LIMIT = 56 * 1024 * 1024

_W_FO, _W_RQ, _W_RK, _W_RV, _W_RG, _W_DQ, _W_DK, _W_DV = 512, 256, 256, 512, 512, 512, 512, 512


def _cparams(sem):
    return pltpu.CompilerParams(dimension_semantics=sem, vmem_limit_bytes=VMEM_LIMIT)


def _sigmoid(v):
    return 1.0 / (1.0 + jnp.exp(-v))


def _nt_dot(a, b):
    return lax.dot_general(a, b, (((1,), (1,)), ((), ())), preferred_element_type=F32)


def _tn_dot(a, b):
    return lax.dot_general(a, b, (((0,), (0,)), ((), ())), preferred_element_type=F32)


def _mod_kernel(c_ref, w_ref, b_ref, o_ref):
    cv = c_ref[...]
    s = cv * _sigmoid(cv)
    o_ref[0] = jnp.dot(s.astype(BF), w_ref[0].astype(BF), preferred_element_type=F32) + b_ref[0]


def _modulation(cvecs, w_mod, b_mod):
    depth, d, wd = w_mod.shape
    rows = cvecs.shape[0]
    tn = 1536
    return pl.pallas_call(
        _mod_kernel,
        grid=(depth, wd // tn),
        in_specs=[pl.BlockSpec((rows, d), lambda l, j: (0, 0)),
                  pl.BlockSpec((1, d, tn), lambda l, j: (l, 0, j)),
                  pl.BlockSpec((1, 1, tn), lambda l, j: (l, 0, j))],
        out_specs=pl.BlockSpec((1, rows, tn), lambda l, j: (l, 0, j)),
        out_shape=jax.ShapeDtypeStruct((depth, rows, wd), F32),
        compiler_params=_cparams(("arbitrary", "arbitrary")),
        name="modulation",
    )(cvecs, w_mod, b_mod.reshape(depth, 1, wd))


def _rope(x, cos_t, sin_t):
    lane = lax.broadcasted_iota(I32, (x.shape[0], LANES), 1)
    first = (lane & 63) < 32
    outs = []
    for j in range(x.shape[1] // LANES):
        xc = x[:, j * LANES:(j + 1) * LANES]
        sw = jnp.where(first, pltpu.roll(xc, LANES - 32, 1), pltpu.roll(xc, 32, 1))
        outs.append(xc * cos_t + sw * sin_t)
    return jnp.concatenate(outs, axis=1)


def _group_rms(x, gm, gain):
    gw = gm.shape[0]
    sq = (x * x).astype(BF)
    ms = jnp.concatenate([jnp.dot(sq[:, c:c + gw], gm, preferred_element_type=F32)
                          for c in range(0, x.shape[1], gw)], axis=1)
    return x * lax.rsqrt(ms + EPS) * gain


def _inproj_kernel(x_ref, shift_ref, scale_ref, g_ref, w_ref, cos_ref, sin_ref, qg_ref, kg_ref, gm_ref,
                   *out_refs, kv_only):
    if kv_only:
        rk_ref, rv_ref, dk_ref, dv_ref = out_refs
    else:
        fo_ref, rq_ref, rk_ref, rv_ref, sg_ref, dq_ref, dk_ref, dv_ref, gt_ref = out_refs
    x = x_ref[...]
    d = x.shape[1]
    ms = jnp.mean(x * x, axis=-1, keepdims=True)
    h = x * lax.rsqrt(ms + EPS) * g_ref[...]
    h = h * (1.0 + scale_ref[0]) + shift_ref[0]
    hb = h.astype(BF)
    cos_t = cos_ref[...]
    sin_t = sin_ref[...]
    gm = gm_ref[...]

    def proj(a, width):
        return jnp.dot(hb, w_ref[:, a:a + width], preferred_element_type=F32)

    a = 0
    if not kv_only:
        fo_ref[...] = proj(a, _W_FO).astype(BF)
    a += _W_FO
    if not kv_only:
        rq_ref[...] = _rope(proj(a, _W_RQ) * (HEAD_DIM ** -0.5), cos_t, sin_t).astype(BF)
    a += _W_RQ
    rk_ref[...] = _rope(proj(a, _W_RK), cos_t, sin_t).astype(BF)
    a += _W_RK
    rv_ref[...] = proj(a, _W_RV).astype(BF)
    a += _W_RV
    if not kv_only:
        rg = proj(a, _W_RG)
        sg_ref[...] = (rg * _sigmoid(rg)).astype(BF)
    a += _W_RG
    if not kv_only:
        dq = _group_rms(proj(a, _W_DQ), gm, qg_ref[...])
        dq_ref[...] = (_rope(dq, cos_t, sin_t) * (HEAD_DIM ** -0.5 * LOG2E)).astype(BF)
    a += _W_DQ
    dk = _group_rms(proj(a, _W_DK), gm, kg_ref[...])
    dk_ref[...] = _rope(dk, cos_t, sin_t).astype(BF)
    a += _W_DK
    dv = proj(a, _W_DV).astype(BF)
    ones = jnp.ones((dv.shape[0], HEAD_V), BF)
    for hh in range(N_HEADS):
        dv_ref[:, 2 * hh * HEAD_V:(2 * hh + 1) * HEAD_V] = dv[:, hh * HEAD_V:(hh + 1) * HEAD_V]
        dv_ref[:, (2 * hh + 1) * HEAD_V:(2 * hh + 2) * HEAD_V] = ones
    a += _W_DV
    if not kv_only:
        for j in range(3):
            gl = proj(a + j * d, d)
            gt_ref[:, j * d:(j + 1) * d] = _sigmoid(gl).astype(BF)


def _in_projection(x2d, shift, scale, g, w_in_bf, cos_t, sin_t, qg, kg, gm, n_seq, tm, kv_only=False):
    rows, d = x2d.shape
    tiles_per_b = n_seq // tm
    nb = shift.shape[0]
    win = w_in_bf.shape[1]
    if kv_only:
        widths = (_W_RK, _W_RV, _W_DK, 2 * _W_DV)
    else:
        widths = (_W_FO, _W_RQ, _W_RK, _W_RV, _W_RG, _W_DQ, _W_DK, 2 * _W_DV, 3 * d)

    def mod_map(i):
        return ((i // tiles_per_b) if nb > 1 else 0, 0, 0)

    def pos_map(i):
        return (i % tiles_per_b, 0)

    const2 = lambda i: (0, 0)
    return pl.pallas_call(
        functools.partial(_inproj_kernel, kv_only=kv_only),
        grid=(rows // tm,),
        in_specs=[pl.BlockSpec((tm, d), lambda i: (i, 0)),
                  pl.BlockSpec((1, 1, d), mod_map),
                  pl.BlockSpec((1, 1, d), mod_map),
                  pl.BlockSpec((1, d), const2),
                  pl.BlockSpec((d, win), const2, pipeline_mode=pl.Buffered(1)),
                  pl.BlockSpec((tm, LANES), pos_map),
                  pl.BlockSpec((tm, LANES), pos_map),
                  pl.BlockSpec((1, _W_DQ), const2),
                  pl.BlockSpec((1, _W_DK), const2),
                  pl.BlockSpec(gm.shape, const2)],
        out_specs=[pl.BlockSpec((tm, w), lambda i: (i, 0)) for w in widths],
        out_shape=[jax.ShapeDtypeStruct((rows, w), BF) for w in widths],
        compiler_params=_cparams(("arbitrary",)),
        name="in_projection",
    )(x2d, shift, scale, g, w_in_bf, cos_t, sin_t, qg, kg, gm)


def _fourier_kernel(*refs):
    _fourier_fold(pl.program_id(1), *refs)
    _fourier_rows(*refs)


def _fourier_fold(step, x_ref, cs_ref, bc_ref, bs_ref, jsh_ref, o_ref, z_ref, eo_ref):
    n, w = x_ref.shape[1], x_ref.shape[2]
    half = n // 2
    gw = bc_ref.shape[0]
    fb = jsh_ref.shape[0]

    @pl.when(step == 0)
    def _():
        x = x_ref[0]
        for g in range(w // gw):
            xs = x[:, g * gw:(g + 1) * gw]
            z_ref[0, :, g * gw:(g + 1) * gw] = jnp.dot(xs, bc_ref[...], preferred_element_type=F32).astype(BF)
            z_ref[1, :, g * gw:(g + 1) * gw] = jnp.dot(xs, bs_ref[...], preferred_element_type=F32).astype(BF)
        jsh = jsh_ref[...]
        n_blocks = half // fb
        for blk in range(n_blocks):
            hi_blk = 2 * n_blocks - 1 - blk
            for t in range(2):
                top = z_ref[t, hi_blk * fb:(hi_blk + 1) * fb, :]
                nxt = jnp.zeros_like(top) if blk == 0 else z_ref[t, (hi_blk + 1) * fb:(hi_blk + 2) * fb, :]
                partner = jnp.dot(jsh, jnp.concatenate([top, nxt], axis=0), preferred_element_type=F32)
                own = z_ref[t, blk * fb:(blk + 1) * fb, :].astype(F32)
                folded = own + partner if t == 0 else own - partner
                eo_ref[t * half + blk * fb:t * half + (blk + 1) * fb, :] = folded.astype(BF)


def _fourier_rows(x_ref, cs_ref, bc_ref, bs_ref, jsh_ref, o_ref, z_ref, eo_ref):
    n = x_ref.shape[1]
    half = n // 2
    tr = o_ref.shape[1]
    y = jnp.dot(cs_ref[...], eo_ref[...], preferred_element_type=F32)
    parity = lax.broadcasted_iota(I32, (tr, 1), 0) & 1
    sign = (1.0 - 2.0 * parity.astype(F32)) * (1.0 / math.sqrt(n))
    o_ref[0] = (y + sign * z_ref[0, half:half + 1, :].astype(F32)).astype(BF)


def _fourier_mix(fo, cs, bdc, bds, jsh, tr):
    b, n, w = fo.shape
    assert n % (2 * jsh.shape[0]) == 0 and tr % 2 == 0
    return pl.pallas_call(
        _fourier_kernel,
        grid=(b, n // tr),
        in_specs=[pl.BlockSpec((1, n, w), lambda i, j: (i, 0, 0)),
                  pl.BlockSpec((tr, n), lambda i, j: (j, 0)),
                  pl.BlockSpec(bdc.shape, lambda i, j: (0, 0)),
                  pl.BlockSpec(bds.shape, lambda i, j: (0, 0)),
                  pl.BlockSpec(jsh.shape, lambda i, j: (0, 0))],
        out_specs=pl.BlockSpec((1, tr, w), lambda i, j: (i, j, 0)),
        out_shape=jax.ShapeDtypeStruct((b, n, w), BF),
        scratch_shapes=[pltpu.VMEM((2, n, w), BF), pltpu.VMEM((n, w), BF)],
        compiler_params=_cparams(("arbitrary", "arbitrary")),
        name="fourier_mix",
    )(fo, cs, bdc, bds, jsh)


def _ret_kernel(*refs, chunk, n_chunks, sub):
    _ret_body(pl.program_id(0), pl.program_id(1), pl.program_id(2), *refs, chunk=chunk, n_chunks=n_chunks, sub=sub)


def _ret_body(b, p, c, lg_ref, q_ref, k_ref, v_ref, sg_ref, sf_ref, sb_ref, o_ref, sfo_ref, sbo_ref,
              acc_ref, tmp_ref, st_ref, dm_ref, qd_ref, qm_ref, kd_ref, cd_ref, *, chunk, n_chunks, sub):
    sf = float(sub)
    n_sub = chunk // sub

    @pl.when((b == 0) & (p == 0) & (c == 0))
    def _tables():
        ii = lax.broadcasted_iota(I32, (sub, sub), 0).astype(F32)
        jj = lax.broadcasted_iota(I32, (sub, sub), 1).astype(F32)
        pos = lax.broadcasted_iota(I32, (sub, LANES), 0).astype(F32)
        lane = lax.broadcasted_iota(I32, (sub, LANES), 1)
        for dr in range(2):
            if dr == 0:
                rel, keep = ii - jj, ii >= jj
                qe, ke = pos + 1.0, (sf - 1.0) - pos
            else:
                rel, keep = jj - ii, jj > ii
                qe, ke = sf - pos, pos
            for hp in range(N_HEADS // 2):
                lg0 = jnp.full((sub, LANES), lg_ref[dr, 2 * hp], F32)
                lg1 = jnp.full((sub, LANES), lg_ref[dr, 2 * hp + 1], F32)
                lgp = jnp.where(lane < HEAD_DIM, lg0, lg1)
                kd_ref[dr, hp] = jnp.exp(lgp * ke)
            for h in range(N_HEADS):
                lgs = lg_ref[dr, h]
                in_head = (lane >= (h % 2) * HEAD_DIM) & (lane < (h % 2 + 1) * HEAD_DIM)
                dm_ref[dr, h] = jnp.where(keep, jnp.exp(jnp.full((sub, sub), lgs, F32) * jnp.maximum(rel, 0.0)), 0.0)
                qd_ref[dr, h] = jnp.where(in_head, jnp.exp(jnp.full((sub, LANES), lgs, F32) * qe), 0.0)
                cd_ref[dr, h] = jnp.exp(jnp.full((LANES, LANES), lgs, F32) * sf)
        for h in range(N_HEADS):
            in_head = (lane >= (h % 2) * HEAD_DIM) & (lane < (h % 2 + 1) * HEAD_DIM)
            qm_ref[h] = jnp.where(in_head, 1.0, 0.0)

    nbb = q_ref.shape[0]

    def load_state(src_ref):
        z = jnp.zeros((HEAD_DIM, HEAD_V), F32)
        for bb in range(nbb):
            for h in range(N_HEADS):
                s = src_ref[bb, h]
                st_ref[bb, h] = jnp.concatenate([s, z], axis=0) if h % 2 == 0 else jnp.concatenate([z, s], axis=0)

    @pl.when((c == 0) & (p == 0))
    def _():
        load_state(sf_ref)

    @pl.when((c == 0) & (p == 1))
    def _():
        load_state(sb_ref)

    for bb in range(nbb):
        sts = [st_ref[bb, h] for h in range(N_HEADS)]
        for u in range(n_sub):
            r0 = pl.multiple_of(jnp.where(p == 0, u, n_sub - 1 - u) * sub, sub)
            rows = pl.ds(r0, sub)
            q = q_ref[bb, rows, :]
            k = k_ref[bb, rows, :]
            v = v_ref[bb, rows, :]
            outs = []
            for hp in range(N_HEADS // 2):
                qp = q[:, hp * LANES:(hp + 1) * LANES].astype(F32)
                kp = k[:, hp * LANES:(hp + 1) * LANES]
                kdp = (kp.astype(F32) * kd_ref[p, hp]).astype(BF)
                for h in (2 * hp, 2 * hp + 1):
                    vh = v[:, h * HEAD_V:(h + 1) * HEAD_V]
                    qm = (qp * qm_ref[h]).astype(BF)
                    qdq = (qp * qd_ref[p, h]).astype(BF)
                    s = _nt_dot(qm, kp) * dm_ref[p, h]
                    o = (jnp.dot(s.astype(BF), vh, preferred_element_type=F32)
                         + jnp.dot(qdq, sts[h].astype(BF), preferred_element_type=F32))
                    sts[h] = sts[h] * cd_ref[p, h] + _tn_dot(kdp, vh)
                    outs.append(o)
            tmp_ref[bb, rows, :] = jnp.concatenate(outs, axis=1)
        for h in range(N_HEADS):
            st_ref[bb, h] = sts[h]
    idx = jnp.where(p == 0, c, n_chunks - 1 - c)

    @pl.when(p == 0)
    def _():
        for bb in range(nbb):
            acc_ref[bb, idx] = tmp_ref[bb]

    @pl.when(p == 1)
    def _():
        for bb in range(nbb):
            tot = acc_ref[bb, idx] + tmp_ref[bb]
            sg = sg_ref[bb].astype(F32)
            ys = []
            for h in range(N_HEADS):
                oh = tot[:, h * HEAD_V:(h + 1) * HEAD_V]
                mu = jnp.mean(oh, axis=-1, keepdims=True)
                dlt = oh - mu
                var = jnp.mean(dlt * dlt, axis=-1, keepdims=True)
                ys.append(dlt * lax.rsqrt(var + EPS))
            o_ref[bb] = (jnp.concatenate(ys, axis=1) * sg).astype(BF)

    def store_state(dst_ref):
        for bb in range(nbb):
            for h in range(N_HEADS):
                s0 = (h % 2) * HEAD_DIM
                dst_ref[bb, h] = st_ref[bb, h, s0:s0 + HEAD_DIM, :]

    @pl.when((c == n_chunks - 1) & (p == 0))
    def _():
        store_state(sfo_ref)

    @pl.when((c == n_chunks - 1) & (p == 1))
    def _():
        store_state(sbo_ref)


def _retention(rq, rk, rv, sg, decays, s_f, s_b, chunk):
    b, n, _ = rq.shape
    n_chunks = n // chunk
    wv = rv.shape[2]

    def seq_map(i, p, c):
        return (i, jnp.where(p == 0, c, n_chunks - 1 - c), 0)

    def out_map(i, p, c):
        return (i, jnp.where(p == 0, n_chunks - 1, n_chunks - 1 - c), 0)

    nbb = 2 if b % 2 == 0 else 1
    st_spec = pl.BlockSpec((nbb, N_HEADS, HEAD_DIM, HEAD_V), lambda i, p, c: (i, 0, 0, 0))
    st_shape = jax.ShapeDtypeStruct((b, N_HEADS, HEAD_DIM, HEAD_V), F32)
    sub = min(RET_SUB, chunk)
    kern = functools.partial(_ret_kernel, chunk=chunk, n_chunks=n_chunks, sub=sub)
    return pl.pallas_call(
        kern,
        grid=(b // nbb, 2, n_chunks),
        in_specs=[pl.BlockSpec(memory_space=pltpu.SMEM),
                  pl.BlockSpec((nbb, chunk, rq.shape[2]), seq_map),
                  pl.BlockSpec((nbb, chunk, rk.shape[2]), seq_map),
                  pl.BlockSpec((nbb, chunk, wv), seq_map),
                  pl.BlockSpec((nbb, chunk, wv), seq_map),
                  st_spec, st_spec],
        out_specs=[pl.BlockSpec((nbb, chunk, wv), out_map), st_spec, st_spec],
        out_shape=[jax.ShapeDtypeStruct((b, n, wv), BF), st_shape, st_shape],
        scratch_shapes=[pltpu.VMEM((nbb, n_chunks, chunk, wv), F32),
                        pltpu.VMEM((nbb, chunk, wv), F32),
                        pltpu.VMEM((nbb, N_HEADS, LANES, HEAD_V), F32),
                        pltpu.VMEM((2, N_HEADS, sub, sub), F32),
                        pltpu.VMEM((2, N_HEADS, sub, LANES), F32),
                        pltpu.VMEM((N_HEADS, sub, LANES), F32),
                        pltpu.VMEM((2, N_HEADS // 2, sub, LANES), F32),
                        pltpu.VMEM((2, N_HEADS, LANES, HEAD_V), F32)],
        compiler_params=_cparams(("arbitrary", "arbitrary", "arbitrary")),
        name="retention",
    )(decays, rq, rk, rv, sg, s_f, s_b)


def _dattn_kernel(dl_ref, q_ref, *refs, n_src, lam_init, sub_q):
    k_refs = refs[:n_src]
    v_refs = refs[n_src:2 * n_src]
    o_ref = refs[2 * n_src]
    dl = dl_ref[...]
    lam = (jnp.exp(jnp.sum(dl[0:1] * dl[1:2], axis=-1, keepdims=True))
           - jnp.exp(jnp.sum(dl[2:3] * dl[3:4], axis=-1, keepdims=True)) + lam_init)
    tq = min(sub_q, q_ref.shape[1])
    lane = lax.broadcasted_iota(I32, (tq, LANES), 1)
    for r0 in range(0, q_ref.shape[1], tq):
        q = q_ref[0, r0:r0 + tq, :].astype(F32)
        ys = []
        for h in range(N_HEADS):
            qh = q[:, h * LANES:(h + 1) * LANES]
            qs = jnp.concatenate([jnp.where(lane < HEAD_DIM, qh, 0.0), jnp.where(lane >= HEAD_DIM, qh, 0.0)],
                                 axis=0).astype(BF)
            ss = [_nt_dot(qs, kr[0, :, h * LANES:(h + 1) * LANES]) for kr in k_refs]
            m = ss[0].max(axis=-1, keepdims=True)
            for s in ss[1:]:
                m = jnp.maximum(m, s.max(axis=-1, keepdims=True))
            acc = jnp.zeros((2 * tq, 2 * HEAD_V), F32)
            for s, vr in zip(ss, v_refs):
                pexp = jnp.exp2(s - m).astype(BF)
                acc = acc + jnp.dot(pexp, vr[0, :, 2 * h * HEAD_V:(2 * h + 2) * HEAD_V],
                                    preferred_element_type=F32)
            o = acc[:, :HEAD_V] / acc[:, HEAD_V:]
            oh = o[:tq] - lam * o[tq:]
            ms = jnp.mean(oh * oh, axis=-1, keepdims=True)
            ys.append(oh * lax.rsqrt(ms + EPS) * (1.0 - lam_init))
        o_ref[0, r0:r0 + tq, :] = jnp.concatenate(ys, axis=1).astype(BF)


def _diff_attention(dq, ks, vs, dlam, lam_init, tq, sub_q=256):
    b, n, w = dq.shape
    n_src = len(ks)
    kern = functools.partial(_dattn_kernel, n_src=n_src, lam_init=lam_init, sub_q=sub_q)
    kv_specs = [pl.BlockSpec((1, a.shape[1], a.shape[2]), lambda i, j: (i, 0, 0)) for a in (*ks, *vs)]
    return pl.pallas_call(
        kern,
        grid=(b, n // tq),
        in_specs=[pl.BlockSpec(dlam.shape, lambda i, j: (0, 0)),
                  pl.BlockSpec((1, tq, w), lambda i, j: (i, j, 0))] + kv_specs,
        out_specs=pl.BlockSpec((1, tq, w), lambda i, j: (i, j, 0)),
        out_shape=jax.ShapeDtypeStruct((b, n, w), BF),
        compiler_params=_cparams(("arbitrary", "arbitrary")),
        name="diff_attention",
    )(dlam, dq, *ks, *vs)


def _merge_kernel(f_ref, ro_ref, do_ref, gt_ref, x_ref, m2_ref, m3_ref, m4_ref, g_ref,
                  wf_ref, wr_ref, wd_ref, wo_ref, rcat_ref, x1_ref, h2_ref, aff_ref):
    tm, d = x_ref.shape
    st = min(MERGE_SUB, tm)
    for r0 in range(0, tm, st):
        rows = slice(r0, r0 + st)
        t = (gt_ref[rows, 0:d].astype(F32) * jnp.dot(f_ref[rows, :], wf_ref[...], preferred_element_type=F32)
             + gt_ref[rows, d:2 * d].astype(F32) * jnp.dot(ro_ref[rows, :], wr_ref[...], preferred_element_type=F32)
             + gt_ref[rows, 2 * d:3 * d].astype(F32) * jnp.dot(do_ref[rows, :], wd_ref[...],
                                                                preferred_element_type=F32))
        mix = jnp.dot(t.astype(BF), wo_ref[...], preferred_element_type=F32)
        x1 = x_ref[rows, :] + m2_ref[0] * mix
        x1_ref[rows, :] = x1
        ms = jnp.mean(x1 * x1, axis=-1, keepdims=True)
        h2 = x1 * lax.rsqrt(ms + EPS) * g_ref[...]
        h2 = h2 * (1.0 + m4_ref[0]) + m3_ref[0]
        hi = h2.astype(BF)
        lo = (h2 - hi.astype(F32)).astype(BF)
        h2_ref[rows, :] = hi
        l1 = jnp.dot(hi, rcat_ref[...], preferred_element_type=F32)
        lt = l1[:, :LANES] + l1[:, LANES:] + jnp.dot(lo, rcat_ref[:, :LANES], preferred_element_type=F32)
        lane = lax.broadcasted_iota(I32, lt.shape, 1)
        lt = jnp.where(lane < N_EXPERTS, lt, -jnp.inf)
        ex = jnp.exp(lt - lt.max(axis=-1, keepdims=True))
        aff = ex / jnp.sum(ex, axis=-1, keepdims=True)
        aff_ref[0, :, rows] = aff.T[:N_EXPERTS, :]


def _merge(f, ro, do, gt, x2d, m2, m3, m4, g, wf, wr, wd, wo, rcat, n_seq, tm):
    rows, d = x2d.shape
    tiles_per_b = n_seq // tm
    nb = m2.shape[0]
    bsz = rows // n_seq
    ne = N_EXPERTS

    def mod_map(i):
        return ((i // tiles_per_b) if nb > 1 else 0, 0, 0)

    const2 = lambda i: (0, 0)
    row_spec = lambda w: pl.BlockSpec((tm, w), lambda i: (i, 0))
    return pl.pallas_call(
        _merge_kernel,
        grid=(rows // tm,),
        in_specs=[row_spec(f.shape[1]), row_spec(ro.shape[1]), row_spec(do.shape[1]), row_spec(gt.shape[1]),
                  row_spec(d),
                  pl.BlockSpec((1, 1, d), mod_map), pl.BlockSpec((1, 1, d), mod_map), pl.BlockSpec((1, 1, d), mod_map),
                  pl.BlockSpec((1, d), const2),
                  pl.BlockSpec(wf.shape, const2), pl.BlockSpec(wr.shape, const2), pl.BlockSpec(wd.shape, const2),
                  pl.BlockSpec(wo.shape, const2), pl.BlockSpec(rcat.shape, const2)],
        out_specs=[row_spec(d), row_spec(d),
                   pl.BlockSpec((1, ne, tm), lambda i: (i // tiles_per_b, 0, i % tiles_per_b))],
        out_shape=[jax.ShapeDtypeStruct((rows, d), F32), jax.ShapeDtypeStruct((rows, d), BF),
                   jax.ShapeDtypeStruct((bsz, ne, n_seq), F32)],
        compiler_params=_cparams(("arbitrary",)),
        name="merge_router",
    )(f, ro, do, gt, x2d, m2, m3, m4, g, wf, wr, wd, wo, rcat)


def _route_kernel(a_ref, slot_ref, offs_ref, *, cap, blk):
    a = a_ref[...]
    ne, n = a.shape
    capf = float(cap)

    def enough(t):
        return jnp.sum(jnp.where(a >= t, 1.0, 0.0), axis=-1, keepdims=True) >= capf

    tiny = jnp.full((ne, 1), 2.0 ** -126, F32)
    found = enough(tiny)
    cur = tiny
    for step in (64, 32, 16, 8, 4, 2, 1):
        cand = cur * (2.0 ** step)
        cur = jnp.where(enough(cand), cand, cur)
    base = cur

    def mantissa_bit(_, carry):
        cur, stepv = carry
        cand = cur + stepv
        return jnp.where(enough(cand), cand, cur), stepv * 0.5

    cur, ulp = lax.fori_loop(0, 23, mantissa_bit, (cur, base * 0.5))
    lo = jnp.where(found, cur, 0.0)
    hi = jnp.where(found, cur + ulp * 2.0, tiny)

    def refine(_, carry):
        lo, hi = carry
        mid = lo + (hi - lo) * 0.5
        ok = enough(mid)
        return jnp.where(ok, mid, lo), jnp.where(ok, hi, mid)

    lo, hi = lax.fori_loop(0, 24, refine, (lo, hi))
    ri = lax.broadcasted_iota(I32, (blk, blk), 0)
    ci = lax.broadcasted_iota(I32, (blk, blk), 1)
    upper = jnp.where(ri < ci, 1.0, 0.0).astype(BF)

    def excl_cumsum(m):
        carry = jnp.zeros((ne, 1), F32)
        outs = []
        for j in range(n // blk):
            mb = m[:, j * blk:(j + 1) * blk]
            outs.append(jnp.dot(mb.astype(BF), upper, preferred_element_type=F32) + carry)
            carry = carry + jnp.sum(mb, axis=-1, keepdims=True)
        return jnp.concatenate(outs, axis=1)

    gt = a >= hi
    tie = (a >= lo) & (a < hi)
    need = capf - jnp.sum(jnp.where(gt, 1.0, 0.0), axis=-1, keepdims=True)
    sel = gt | (tie & (excl_cumsum(jnp.where(tie, 1.0, 0.0)) < need))
    selm = jnp.where(sel, 1.0, 0.0)
    slot = excl_cumsum(selm)
    slot_ref[...] = jnp.where(sel, slot.astype(I32), -1)
    tok = lax.broadcasted_iota(I32, (n, LANES), 0)
    blk_start = lax.broadcasted_iota(I32, (n, LANES), 1) * blk
    before = jnp.where((tok < blk_start) & (blk_start <= n), 1.0, 0.0).astype(BF)
    offs_ref[...] = jnp.dot(selm.astype(BF), before, preferred_element_type=F32).astype(I32)


def _route(aff_t, cap):
    b, ne, n = aff_t.shape
    kern = functools.partial(_route_kernel, cap=cap, blk=min(TOKEN_BLOCK, n))
    slot, offs = pl.pallas_call(
        kern,
        grid=(1,),
        in_specs=[pl.BlockSpec((b * ne, n), lambda i: (0, 0))],
        out_specs=[pl.BlockSpec((b * ne, n), lambda i: (0, 0)), pl.BlockSpec((b * ne, LANES), lambda i: (0, 0))],
        out_shape=[jax.ShapeDtypeStruct((b * ne, n), I32), jax.ShapeDtypeStruct((b * ne, LANES), I32)],
        compiler_params=_cparams(("arbitrary",)),
        name="route",
    )(aff_t.reshape(b * ne, n))
    return slot.reshape(b, ne, n), offs


def _window_start(off, cap, win):
    return jnp.minimum(lax.shift_left(lax.shift_right_logical(off, 4), 4), cap - win)


def _windows_fit(offs_ref, row0, ne, blk0, nblk, cap, win):
    bad = jnp.int32(0)
    for e in range(ne):
        for jj in range(nblk):
            j = blk0 + jj
            off = offs_ref[row0 + e, j]
            end = offs_ref[row0 + e, j + 1]
            bad = bad | (end - _window_start(off, cap, win) > win).astype(I32)
    return bad == 0


def _gather_kernel(offs_ref, slot_ref, h_ref, xs_ref, *, cap, win, tb):
    ne, n = slot_ref.shape[1], slot_ref.shape[3]
    nblk = n // tb
    row0 = pl.program_id(0) * ne
    fits = _windows_fit(offs_ref, row0, ne, 0, nblk, cap, win)

    @pl.when(fits)
    def _windowed():
        xs_ref[...] = jnp.zeros(xs_ref.shape, BF)
        rows = lax.broadcasted_iota(I32, (win, tb), 0)
        for j in range(nblk):
            starts, lhs = [], []
            for e in range(ne):
                w0 = pl.multiple_of(_window_start(offs_ref[row0 + e, j], cap, win), 16)
                starts.append(w0)
                lhs.append(jnp.where(slot_ref[0, e, :, j * tb:(j + 1) * tb] == rows + w0, 1.0, 0.0).astype(BF))
            got = jnp.dot(jnp.concatenate(lhs, axis=0), h_ref[0, j * tb:(j + 1) * tb, :],
                          preferred_element_type=F32)
            for e in range(ne):
                dst = xs_ref.at[0, e, pl.ds(starts[e], win), :]
                dst[...] = (dst[...].astype(F32) + got[e * win:(e + 1) * win]).astype(BF)

    @pl.when(jnp.logical_not(fits))
    def _dense():
        sub = lax.broadcasted_iota(I32, (cap, n), 0)
        hb = h_ref[0]
        for e in range(ne):
            onehot = jnp.where(sub == slot_ref[0, e], 1.0, 0.0).astype(BF)
            xs_ref[0, e] = jnp.dot(onehot, hb, preferred_element_type=F32).astype(BF)


def _gather(slot, offs, h2, cap):
    b, ne, n = slot.shape
    d = h2.shape[2]
    tb = min(TOKEN_BLOCK, n)
    kern = functools.partial(_gather_kernel, cap=cap, win=min(SLOT_WINDOW, cap), tb=tb)
    grid_spec = pltpu.PrefetchScalarGridSpec(
        num_scalar_prefetch=1,
        grid=(b,),
        in_specs=[pl.BlockSpec((1, ne, 1, n), lambda i, offs: (i, 0, 0, 0)),
                  pl.BlockSpec((1, n, d), lambda i, offs: (i, 0, 0))],
        out_specs=pl.BlockSpec((1, ne, cap, d), lambda i, offs: (i, 0, 0, 0)))
    return pl.pallas_call(
        kern,
        grid_spec=grid_spec,
        out_shape=jax.ShapeDtypeStruct((b, ne, cap, d), BF),
        compiler_params=_cparams(("arbitrary",)),
        name="moe_gather",
    )(offs, slot.reshape(b, ne, 1, n), h2)


def _ffn_kernel(*refs, n_src, n_fc):
    xs_refs = refs[:n_src]
    wg_ref, wu_ref, wd_ref = refs[n_src:n_src + 3]
    ye_refs = refs[n_src + 3:2 * n_src + 3]
    x_ref, hid_ref = refs[2 * n_src + 3:]
    j = pl.program_id(1)
    fc = wg_ref.shape[3]
    row_spans = []
    r0 = 0
    for xr in xs_refs:
        rows = xr.shape[0] * xr.shape[2]
        row_spans.append((r0, rows))
        r0 += rows

    @pl.when(j == 0)
    def _():
        for xr, (s0, rows) in zip(xs_refs, row_spans):
            x_ref[s0:s0 + rows, :] = xr[:, 0].reshape(rows, xr.shape[3])

    @pl.when(j < n_fc)
    def _():
        x = x_ref[...]
        g = jnp.dot(x, wg_ref[0, 0].astype(BF), preferred_element_type=F32)
        u = jnp.dot(x, wu_ref[0, 0].astype(BF), preferred_element_type=F32)
        hid_ref[j] = (g * _sigmoid(g) * u).astype(BF)

    @pl.when(j >= n_fc)
    def _():
        y = jnp.dot(hid_ref[0], wd_ref[0, 0, 0:fc, :].astype(BF), preferred_element_type=F32)
        for k in range(1, n_fc):
            y = y + jnp.dot(hid_ref[k], wd_ref[0, 0, k * fc:(k + 1) * fc, :].astype(BF), preferred_element_type=F32)
        for yr, (s0, rows) in zip(ye_refs, row_spans):
            yr[:, 0] = y[s0:s0 + rows].reshape(yr.shape[0], yr.shape[2], yr.shape[3]).astype(BF)


def _expert_ffn(xs_list, w_gate, w_up, w_down, layer, fchunk, ochunk):
    ne, d = xs_list[0].shape[1], xs_list[0].shape[3]
    f = w_gate.shape[3]
    n_fc, n_oc = f // fchunk, d // ochunk
    n_src = len(xs_list)
    total_rows = sum(x.shape[0] * x.shape[2] for x in xs_list)
    kern = functools.partial(_ffn_kernel, n_src=n_src, n_fc=n_fc)
    up_map = lambda e, j: (layer, e, 0, jnp.minimum(j, n_fc - 1))
    down_map = lambda e, j: (layer, e, 0, jnp.maximum(j - n_fc, 0))
    out_map = lambda e, j: (0, e, 0, jnp.maximum(j - n_fc, 0))
    outs = pl.pallas_call(
        kern,
        grid=(ne, n_fc + n_oc),
        in_specs=[pl.BlockSpec((x.shape[0], 1, x.shape[2], d), lambda e, j: (0, e, 0, 0), pipeline_mode=pl.Buffered(1))
                  for x in xs_list]
        + [pl.BlockSpec((1, 1, d, fchunk), up_map),
           pl.BlockSpec((1, 1, d, fchunk), up_map),
           pl.BlockSpec((1, 1, f, ochunk), down_map)],
        out_specs=[pl.BlockSpec((x.shape[0], 1, x.shape[2], ochunk), out_map) for x in xs_list],
        out_shape=[jax.ShapeDtypeStruct(x.shape, BF) for x in xs_list],
        scratch_shapes=[pltpu.VMEM((total_rows, d), BF), pltpu.VMEM((n_fc, total_rows, fchunk), BF)],
        compiler_params=_cparams(("arbitrary", "arbitrary")),
        name="expert_ffn",
    )(*xs_list, w_gate, w_up, w_down)
    return list(outs)


def _combine_kernel(offs_ref, x1_ref, m5_ref, slot_ref, aff_ref, ye_ref, o_ref, ywin_ref, *, cap, win, tb):
    ne, tn = slot_ref.shape[1], slot_ref.shape[3]
    row0 = pl.program_id(0) * ne
    blk0 = pl.program_id(1) * (tn // tb)
    fits = _windows_fit(offs_ref, row0, ne, blk0, tn // tb, cap, win)

    @pl.when(fits)
    def _windowed():
        rows = lax.broadcasted_iota(I32, (win, tb), 0)
        for jj in range(tn // tb):
            tok = slice(jj * tb, (jj + 1) * tb)
            pieces = []
            for e in range(ne):
                w0 = pl.multiple_of(_window_start(offs_ref[row0 + e, blk0 + jj], cap, win), 16)
                ywin_ref[e * win:(e + 1) * win, :] = ye_ref[0, e, pl.ds(w0, win), :]
                pieces.append(jnp.where(slot_ref[0, e, :, tok] == rows + w0, aff_ref[0, e, :, tok], 0.0).astype(BF))
            acc = _tn_dot(jnp.concatenate(pieces, axis=0), ywin_ref[...])
            o_ref[0, tok, :] = x1_ref[0, tok, :] + m5_ref[0] * acc

    @pl.when(jnp.logical_not(fits))
    def _dense():
        rows = lax.broadcasted_iota(I32, (cap, tn), 0)
        acc = jnp.zeros(x1_ref.shape[1:], F32)
        for e in range(ne):
            weighted = jnp.where(slot_ref[0, e] == rows, aff_ref[0, e], 0.0).astype(BF)
            acc = acc + _tn_dot(weighted, ye_ref[0, e])
        o_ref[0] = x1_ref[0] + m5_ref[0] * acc


def _combine(x1, m5, slot, aff_t, offs, ye, tn):
    b, n, d = x1.shape
    ne, cap = ye.shape[1], ye.shape[2]
    nb = m5.shape[0]
    tb = min(TOKEN_BLOCK, n)
    win = min(SLOT_WINDOW, cap)
    kern = functools.partial(_combine_kernel, cap=cap, win=win, tb=tb)
    grid_spec = pltpu.PrefetchScalarGridSpec(
        num_scalar_prefetch=1,
        grid=(b, n // tn),
        in_specs=[pl.BlockSpec((1, tn, d), lambda i, j, offs: (i, j, 0)),
                  pl.BlockSpec((1, 1, d), lambda i, j, offs: (i if nb > 1 else 0, 0, 0)),
                  pl.BlockSpec((1, ne, 1, tn), lambda i, j, offs: (i, 0, 0, j)),
                  pl.BlockSpec((1, ne, 1, tn), lambda i, j, offs: (i, 0, 0, j)),
                  pl.BlockSpec((1, ne, cap, d), lambda i, j, offs: (i, 0, 0, 0))],
        out_specs=pl.BlockSpec((1, tn, d), lambda i, j, offs: (i, j, 0)),
        scratch_shapes=[pltpu.VMEM((ne * win, d), BF)])
    return pl.pallas_call(
        kern,
        grid_spec=grid_spec,
        out_shape=jax.ShapeDtypeStruct((b, n, d), F32),
        compiler_params=_cparams(("arbitrary", "arbitrary")),
        name="moe_combine",
    )(offs, x1, m5, slot.reshape(b, ne, 1, n), aff_t.reshape(b, ne, 1, n), ye)


def _rope_tables(n, use_rope):
    if not use_rope:
        return jnp.ones((n, LANES), F32), jnp.zeros((n, LANES), F32)
    rows = n // GRID_W
    row = jnp.repeat(jnp.arange(rows, dtype=F32), GRID_W)
    col = jnp.tile(jnp.arange(GRID_W, dtype=F32), rows)
    inv = ROPE_BASE ** (-jnp.arange(ROPE_FREQS_PER_AXIS, dtype=F32) / ROPE_FREQS_PER_AXIS)
    ang = jnp.concatenate([row[:, None] * inv, col[:, None] * inv], axis=-1)
    cos, sin = jnp.cos(ang), jnp.sin(ang)
    cos_t = jnp.tile(cos, (1, LANES // (HEAD_DIM // 2)))
    sin_t = jnp.tile(jnp.concatenate([-sin, sin], axis=-1), (1, LANES // HEAD_DIM))
    return cos_t, sin_t


def _dft_tables(n):
    n0 = 64
    n1 = n // n0
    k = np.arange(n, dtype=np.int64)[:, None]
    ang1 = jnp.asarray(2.0 * np.pi * ((k * np.arange(n1 // 2)[None, :]) % n1) / n1, F32)
    ang0 = jnp.asarray(2.0 * np.pi * ((k * np.arange(n0)[None, :]) % n) / n, F32)
    c1, s1, c0, s0 = jnp.cos(ang1), jnp.sin(ang1), jnp.cos(ang0), jnp.sin(ang0)
    scale = 1.0 / math.sqrt(n)
    cos_n = (c1[:, :, None] * c0[:, None, :] - s1[:, :, None] * s0[:, None, :]).reshape(n, n // 2) * scale
    sin_n = (s1[:, :, None] * c0[:, None, :] + c1[:, :, None] * s0[:, None, :]).reshape(n, n // 2) * scale
    return jnp.concatenate([cos_n, -sin_n], axis=1).astype(BF)


def _fold_matrix(fb):
    i = np.arange(fb)[:, None]
    s = np.arange(2 * fb)[None, :]
    return jnp.asarray(np.where(s == fb - i, 1.0, 0.0), BF)


def _group_dft_tables(width):
    g = FNET_GROUP_DIM
    idx = np.arange(width)
    same = (idx[:, None] // g) == (idx[None, :] // g)
    ang = 2.0 * np.pi * (((idx[:, None] % g) * (idx[None, :] % g)) % g) / g
    scale = 1.0 / math.sqrt(g)
    bdc = np.where(same, np.cos(ang), 0.0) * scale
    bds = np.where(same, np.sin(ang), 0.0) * scale
    return jnp.asarray(bdc, F32).astype(BF), jnp.asarray(bds, F32).astype(BF)


def _group_mean_matrix(width):
    idx = np.arange(width)
    same = (idx[:, None] // HEAD_DIM) == (idx[None, :] // HEAD_DIM)
    return jnp.asarray(np.where(same, 1.0 / HEAD_DIM, 0.0), BF)


def _mixer_inputs(x3, mods, g_attn_l, w_in_bf, tables, qg, kg, gm, kv_only=False):
    b, n, d = x3.shape
    tm = min(512, n)
    outs = _in_projection(x3.reshape(b * n, d), mods[0], mods[1], g_attn_l, w_in_bf, tables[0], tables[1],
                          qg, kg, gm, n, tm, kv_only)
    return [o.reshape(b, n, o.shape[1]) for o in outs]


def _moe(sets, w_gate, w_up, w_down, layer):
    slots, offss, xss = [], [], []
    for x1, h2, aff_t, _ in sets:
        cap = EC_FACTOR * x1.shape[1] // N_EXPERTS
        slot, offs = _route(aff_t, cap)
        offs = offs[:, :16]
        slots.append(slot)
        offss.append(offs)
        xss.append(_gather(slot, offs, h2, cap))
    yes = _expert_ffn(xss, w_gate, w_up, w_down, layer, 512, 512)
    return [_combine(x1, m5, slot, aff_t, offs, ye, min(512, x1.shape[1]))
            for (x1, _, aff_t, m5), slot, offs, ye in zip(sets, slots, offss, yes)]


def kernel(x, c, ctx, c_ctx, w_mod, b_mod, g_attn, g_ffn, w_in, ret_decay, diff_qn, diff_kn, diff_lambda,
           w_fnet_o, w_ret_o, w_diff_o, w_out, w_router, w_exp_gate, w_exp_up, w_exp_down):
    bsz, n, d = x.shape
    n_ctx = ctx.shape[1]
    depth = w_mod.shape[0]

    pad = (-(bsz + 1)) % 8
    cvecs = jnp.concatenate([c, c_ctx[None, :], jnp.zeros((pad, d), F32)], axis=0)
    mods = _modulation(cvecs, w_mod, b_mod)

    rope_lat = _rope_tables(n, True)
    rope_ctx = _rope_tables(n_ctx, False)
    cs_lat, cs_ctx = _dft_tables(n), _dft_tables(n_ctx)
    bdc, bds = _group_dft_tables(2 * LANES)
    jsh = _fold_matrix(LANES)
    gm = _group_mean_matrix(2 * LANES)
    zero_state = jnp.zeros((bsz, N_HEADS, HEAD_DIM, HEAD_V), F32)
    ret_chunk = 1024

    xc = ctx
    for layer in range(depth):
        last = layer == depth - 1
        lam_init = 0.8 - 0.6 * math.exp(-0.3 * layer)
        mx = [mods[layer, :bsz, j * d:(j + 1) * d].reshape(bsz, 1, d) for j in range(N_MOD)]
        mc = [mods[layer, bsz:bsz + 1, j * d:(j + 1) * d].reshape(1, 1, d) for j in range(N_MOD)]
        w_in_bf = w_in[layer].astype(BF)
        qg = jnp.tile(diff_qn[layer], _W_DQ // HEAD_DIM)[None, :]
        kg = jnp.tile(diff_kn[layer], _W_DK // HEAD_DIM)[None, :]
        g_a = g_attn[layer][None, :]
        g_f = g_ffn[layer][None, :]
        decays = ret_decay[layer]
        dlam = diff_lambda[layer]
        branch_w = (w_fnet_o[layer].astype(BF), w_ret_o[layer].astype(BF), w_diff_o[layer].astype(BF),
                    w_out[layer].astype(BF))
        wr32 = w_router[layer]
        wr_hi = wr32.astype(BF)
        wr_lo = (wr32 - wr_hi.astype(F32)).astype(BF)
        lane_pad = jnp.zeros((d, LANES - N_EXPERTS), BF)
        rcat = jnp.concatenate([wr_hi, lane_pad, wr_lo, lane_pad], axis=1)
        moe_sets = []

        if last:
            rk_c, rv_c, dk_c, dv_c = _mixer_inputs(xc, mc, g_a, w_in_bf, rope_ctx, qg, kg, gm, kv_only=True)
            rq_c, sg_c = rk_c, rv_c
        else:
            fo_c, rq_c, rk_c, rv_c, sg_c, dq_c, dk_c, dv_c, gt_c = _mixer_inputs(xc, mc, g_a, w_in_bf, rope_ctx,
                                                                               qg, kg, gm)
        ro_c, s_f, s_b = _retention(rq_c, rk_c, rv_c, sg_c, decays, zero_state, zero_state, min(ret_chunk, n_ctx))
        if not last:
            f_c = _fourier_mix(fo_c, cs_ctx, bdc, bds, jsh, min(512, n_ctx))
            do_c = _diff_attention(dq_c, [dk_c], [dv_c], dlam, lam_init, min(256, n_ctx))
            rows_c = bsz * n_ctx
            x1_c, h2_c, aff_c = _merge(f_c.reshape(rows_c, -1), ro_c.reshape(rows_c, -1), do_c.reshape(rows_c, -1),
                                       gt_c.reshape(rows_c, -1), xc.reshape(rows_c, d), mc[2], mc[3], mc[4], g_f,
                                       *branch_w, rcat, n_ctx, min(512, n_ctx))
            moe_sets.append((x1_c.reshape(bsz, n_ctx, d), h2_c.reshape(bsz, n_ctx, d), aff_c, mc[5]))

        fo, rq, rk, rv, sg, dq, dk, dv, gt = _mixer_inputs(x, mx, g_a, w_in_bf, rope_lat, qg, kg, gm)
        ro, _, _ = _retention(rq, rk, rv, sg, decays, s_f, s_b, min(ret_chunk, n))
        f = _fourier_mix(fo, cs_lat, bdc, bds, jsh, min(512, n))
        do = _diff_attention(dq, [dk, dk_c], [dv, dv_c], dlam, lam_init, min(1024, n))
        rows = bsz * n
        x1, h2, aff = _merge(f.reshape(rows, -1), ro.reshape(rows, -1), do.reshape(rows, -1), gt.reshape(rows, -1),
                             x.reshape(rows, d), mx[2], mx[3], mx[4], g_f, *branch_w, rcat, n, min(1024, n))
        moe_sets.append((x1.reshape(bsz, n, d), h2.reshape(bsz, n, d), aff, mx[5]))
        moe_out = _moe(moe_sets, w_exp_gate, w_exp_up, w_exp_down, layer)
        x = moe_out[-1]
        if not last:
            xc = moe_out[0]
    return x
```

```python
import functools
import math

import jax
import jax.numpy as jnp
import numpy as np
from jax import lax
from jax.experimental import pallas as pl
from jax.experimental.pallas import tpu as pltpu

F32 = jnp.float32
BF = jnp.bfloat16
I32 = jnp.int32

GRID_W = 64
HEAD_DIM = 64
ROPE_FREQS_PER_AXIS = HEAD_DIM // 4
ROPE_BASE = 10000.0
EPS = 1e-6
LOG2E = 1.4426950408889634
FNET_GROUP_DIM = 64
N_HEADS = 4
HEAD_V = 128
N_EXPERTS = 16
EC_FACTOR = 2
N_MOD = 6
RET_SUB = 256
MERGE_SUB = 1024
BF16_ROWS_LOG2 = 4
F32_MIN_NORMAL = 2.0 ** -126
OFFS_LANES = 16
TOKEN_BLOCK = 256
SLOT_WINDOW = 64
LANES = 128
VMEM_LIMIT = 56 * 1024 * 1024

_W_FO, _W_RQ, _W_RK, _W_RV, _W_RG, _W_DQ, _W_DK, _W_DV = 512, 256, 256, 512, 512, 512, 512, 512


def _cparams(sem):
    return pltpu.CompilerParams(dimension_semantics=sem, vmem_limit_bytes=VMEM_LIMIT)


def _sigmoid(v):
    return 1.0 / (1.0 + jnp.exp(-v))


def _nt_dot(a, b):
    return lax.dot_general(a, b, (((1,), (1,)), ((), ())), preferred_element_type=F32)


def _tn_dot(a, b):
    return lax.dot_general(a, b, (((0,), (0,)), ((), ())), preferred_element_type=F32)


def _mod_kernel(c_ref, w_ref, b_ref, o_ref):
    cv = c_ref[...]
    s = cv * _sigmoid(cv)
    o_ref[0] = jnp.dot(s.astype(BF), w_ref[0].astype(BF), preferred_element_type=F32) + b_ref[0]


def _modulation(cvecs, w_mod, b_mod):
    depth, d, wd = w_mod.shape
    rows = cvecs.shape[0]
    tn = 1536
    return pl.pallas_call(
        _mod_kernel,
        grid=(depth, wd // tn),
        in_specs=[pl.BlockSpec((rows, d), lambda l, j: (0, 0)),
                  pl.BlockSpec((1, d, tn), lambda l, j: (l, 0, j)),
                  pl.BlockSpec((1, 1, tn), lambda l, j: (l, 0, j))],
        out_specs=pl.BlockSpec((1, rows, tn), lambda l, j: (l, 0, j)),
        out_shape=jax.ShapeDtypeStruct((depth, rows, wd), F32),
        compiler_params=_cparams(("arbitrary", "arbitrary")),
        name="modulation",
    )(cvecs, w_mod, b_mod.reshape(depth, 1, wd))


def _rope(x, cos_t, sin_t):
    lane = lax.broadcasted_iota(I32, (x.shape[0], LANES), 1)
    first = (lane & 63) < 32
    outs = []
    for j in range(x.shape[1] // LANES):
        xc = x[:, j * LANES:(j + 1) * LANES]
        sw = jnp.where(first, pltpu.roll(xc, LANES - 32, 1), pltpu.roll(xc, 32, 1))
        outs.append(xc * cos_t + sw * sin_t)
    return jnp.concatenate(outs, axis=1)


def _group_rms(x, gm, gain):
    gw = gm.shape[0]
    sq = (x * x).astype(BF)
    ms = jnp.concatenate([jnp.dot(sq[:, c:c + gw], gm, preferred_element_type=F32)
                          for c in range(0, x.shape[1], gw)], axis=1)
    return x * lax.rsqrt(ms + EPS) * gain


def _inproj_kernel(x_ref, shift_ref, scale_ref, g_ref, w_ref, cos_ref, sin_ref, qg_ref, kg_ref, gm_ref,
                   *out_refs, kv_only):
    if kv_only:
        rk_ref, rv_ref, dk_ref, dv_ref = out_refs
    else:
        fo_ref, rq_ref, rk_ref, rv_ref, sg_ref, dq_ref, dk_ref, dv_ref, gt_ref = out_refs
    x = x_ref[...]
    d = x.shape[1]
    ms = jnp.mean(x * x, axis=-1, keepdims=True)
    h = x * lax.rsqrt(ms + EPS) * g_ref[...]
    h = h * (1.0 + scale_ref[0]) + shift_ref[0]
    hb = h.astype(BF)
    cos_t = cos_ref[...]
    sin_t = sin_ref[...]
    gm = gm_ref[...]

    def proj(a, width):
        return jnp.dot(hb, w_ref[:, a:a + width], preferred_element_type=F32)

    a = 0
    if not kv_only:
        fo_ref[...] = proj(a, _W_FO).astype(BF)
    a += _W_FO
    if not kv_only:
        rq_ref[...] = _rope(proj(a, _W_RQ) * (HEAD_DIM ** -0.5), cos_t, sin_t).astype(BF)
    a += _W_RQ
    rk_ref[...] = _rope(proj(a, _W_RK), cos_t, sin_t).astype(BF)
    a += _W_RK
    rv_ref[...] = proj(a, _W_RV).astype(BF)
    a += _W_RV
    if not kv_only:
        rg = proj(a, _W_RG)
        sg_ref[...] = (rg * _sigmoid(rg)).astype(BF)
    a += _W_RG
    if not kv_only:
        dq = _group_rms(proj(a, _W_DQ), gm, qg_ref[...])
        dq_ref[...] = (_rope(dq, cos_t, sin_t) * (HEAD_DIM ** -0.5 * LOG2E)).astype(BF)
    a += _W_DQ
    dk = _group_rms(proj(a, _W_DK), gm, kg_ref[...])
    dk_ref[...] = _rope(dk, cos_t, sin_t).astype(BF)
    a += _W_DK
    dv = proj(a, _W_DV).astype(BF)
    ones = jnp.ones((dv.shape[0], HEAD_V), BF)
    for hh in range(N_HEADS):
        dv_ref[:, 2 * hh * HEAD_V:(2 * hh + 1) * HEAD_V] = dv[:, hh * HEAD_V:(hh + 1) * HEAD_V]
        dv_ref[:, (2 * hh + 1) * HEAD_V:(2 * hh + 2) * HEAD_V] = ones
    a += _W_DV
    if not kv_only:
        for j in range(3):
            gl = proj(a + j * d, d)
            gt_ref[:, j * d:(j + 1) * d] = _sigmoid(gl).astype(BF)


def _in_projection(x2d, shift, scale, g, w_in_bf, cos_t, sin_t, qg, kg, gm, n_seq, tm, kv_only=False):
    rows, d = x2d.shape
    tiles_per_b = n_seq // tm
    nb = shift.shape[0]
    win = w_in_bf.shape[1]
    if kv_only:
        widths = (_W_RK, _W_RV, _W_DK, 2 * _W_DV)
    else:
        widths = (_W_FO, _W_RQ, _W_RK, _W_RV, _W_RG, _W_DQ, _W_DK, 2 * _W_DV, 3 * d)

    def mod_map(i):
        return ((i // tiles_per_b) if nb > 1 else 0, 0, 0)

    def pos_map(i):
        return (i % tiles_per_b, 0)

    const2 = lambda i: (0, 0)
    return pl.pallas_call(
        functools.partial(_inproj_kernel, kv_only=kv_only),
        grid=(rows // tm,),
        in_specs=[pl.BlockSpec((tm, d), lambda i: (i, 0)),
                  pl.BlockSpec((1, 1, d), mod_map),
                  pl.BlockSpec((1, 1, d), mod_map),
                  pl.BlockSpec((1, d), const2),
                  pl.BlockSpec((d, win), const2, pipeline_mode=pl.Buffered(1)),
                  pl.BlockSpec((tm, LANES), pos_map),
                  pl.BlockSpec((tm, LANES), pos_map),
                  pl.BlockSpec((1, _W_DQ), const2),
                  pl.BlockSpec((1, _W_DK), const2),
                  pl.BlockSpec(gm.shape, const2)],
        out_specs=[pl.BlockSpec((tm, w), lambda i: (i, 0)) for w in widths],
        out_shape=[jax.ShapeDtypeStruct((rows, w), BF) for w in widths],
        compiler_params=_cparams(("arbitrary",)),
        name="in_projection",
    )(x2d, shift, scale, g, w_in_bf, cos_t, sin_t, qg, kg, gm)


def _fourier_kernel(*refs):
    _fourier_fold(pl.program_id(1), *refs)
    _fourier_rows(*refs)


def _fourier_fold(step, x_ref, cs_ref, bc_ref, bs_ref, jsh_ref, o_ref, z_ref, eo_ref):
    n, w = x_ref.shape[1], x_ref.shape[2]
    half = n // 2
    gw = bc_ref.shape[0]
    fb = jsh_ref.shape[0]

    @pl.when(step == 0)
    def _():
        x = x_ref[0]
        for g in range(w // gw):
            xs = x[:, g * gw:(g + 1) * gw]
            z_ref[0, :, g * gw:(g + 1) * gw] = jnp.dot(xs, bc_ref[...], preferred_element_type=F32).astype(BF)
            z_ref[1, :, g * gw:(g + 1) * gw] = jnp.dot(xs, bs_ref[...], preferred_element_type=F32).astype(BF)
        jsh = jsh_ref[...]
        n_blocks = half // fb
        for blk in range(n_blocks):
            hi_blk = 2 * n_blocks - 1 - blk
            for t in range(2):
                top = z_ref[t, hi_blk * fb:(hi_blk + 1) * fb, :]
                nxt = jnp.zeros_like(top) if blk == 0 else z_ref[t, (hi_blk + 1) * fb:(hi_blk + 2) * fb, :]
                partner = jnp.dot(jsh, jnp.concatenate([top, nxt], axis=0), preferred_element_type=F32)
                own = z_ref[t, blk * fb:(blk + 1) * fb, :].astype(F32)
                folded = own + partner if t == 0 else own - partner
                eo_ref[t * half + blk * fb:t * half + (blk + 1) * fb, :] = folded.astype(BF)


def _fourier_rows(x_ref, cs_ref, bc_ref, bs_ref, jsh_ref, o_ref, z_ref, eo_ref):
    n = x_ref.shape[1]
    half = n // 2
    tr = o_ref.shape[1]
    y = jnp.dot(cs_ref[...], eo_ref[...], preferred_element_type=F32)
    parity = lax.broadcasted_iota(I32, (tr, 1), 0) & 1
    sign = (1.0 - 2.0 * parity.astype(F32)) * (1.0 / math.sqrt(n))
    o_ref[0] = (y + sign * z_ref[0, half:half + 1, :].astype(F32)).astype(BF)


def _fourier_mix(fo, cs, bdc, bds, jsh, tr):
    b, n, w = fo.shape
    assert n % (2 * jsh.shape[0]) == 0 and tr % 2 == 0
    return pl.pallas_call(
        _fourier_kernel,
        grid=(b, n // tr),
        in_specs=[pl.BlockSpec((1, n, w), lambda i, j: (i, 0, 0)),
                  pl.BlockSpec((tr, n), lambda i, j: (j, 0)),
                  pl.BlockSpec(bdc.shape, lambda i, j: (0, 0)),
                  pl.BlockSpec(bds.shape, lambda i, j: (0, 0)),
                  pl.BlockSpec(jsh.shape, lambda i, j: (0, 0))],
        out_specs=pl.BlockSpec((1, tr, w), lambda i, j: (i, j, 0)),
        out_shape=jax.ShapeDtypeStruct((b, n, w), BF),
        scratch_shapes=[pltpu.VMEM((2, n, w), BF), pltpu.VMEM((n, w), BF)],
        compiler_params=_cparams(("arbitrary", "arbitrary")),
        name="fourier_mix",
    )(fo, cs, bdc, bds, jsh)


def _ret_kernel(*refs, chunk, n_chunks, sub):
    _ret_body(pl.program_id(0), pl.program_id(1), pl.program_id(2), *refs, chunk=chunk, n_chunks=n_chunks, sub=sub)


def _ret_body(b, p, c, lg_ref, q_ref, k_ref, v_ref, sg_ref, sf_ref, sb_ref, o_ref, sfo_ref, sbo_ref,
              acc_ref, tmp_ref, st_ref, dm_ref, qd_ref, qm_ref, kd_ref, cd_ref, *, chunk, n_chunks, sub):
    sf = float(sub)
    n_sub = chunk // sub

    @pl.when((b == 0) & (p == 0) & (c == 0))
    def _tables():
        ii = lax.broadcasted_iota(I32, (sub, sub), 0).astype(F32)
        jj = lax.broadcasted_iota(I32, (sub, sub), 1).astype(F32)
        pos = lax.broadcasted_iota(I32, (sub, LANES), 0).astype(F32)
        lane = lax.broadcasted_iota(I32, (sub, LANES), 1)
        for dr in range(2):
            if dr == 0:
                rel, keep = ii - jj, ii >= jj
                qe, ke = pos + 1.0, (sf - 1.0) - pos
            else:
                rel, keep = jj - ii, jj > ii
                qe, ke = sf - pos, pos
            for hp in range(N_HEADS // 2):
                lg0 = jnp.full((sub, LANES), lg_ref[dr, 2 * hp], F32)
                lg1 = jnp.full((sub, LANES), lg_ref[dr, 2 * hp + 1], F32)
                lgp = jnp.where(lane < HEAD_DIM, lg0, lg1)
                kd_ref[dr, hp] = jnp.exp(lgp * ke)
            for h in range(N_HEADS):
                lgs = lg_ref[dr, h]
                in_head = (lane >= (h % 2) * HEAD_DIM) & (lane < (h % 2 + 1) * HEAD_DIM)
                dm_ref[dr, h] = jnp.where(keep, jnp.exp(jnp.full((sub, sub), lgs, F32) * jnp.maximum(rel, 0.0)), 0.0)
                qd_ref[dr, h] = jnp.where(in_head, jnp.exp(jnp.full((sub, LANES), lgs, F32) * qe), 0.0)
                cd_ref[dr, h] = jnp.exp(jnp.full((LANES, LANES), lgs, F32) * sf)
        for h in range(N_HEADS):
            in_head = (lane >= (h % 2) * HEAD_DIM) & (lane < (h % 2 + 1) * HEAD_DIM)
            qm_ref[h] = jnp.where(in_head, 1.0, 0.0)

    nbb = q_ref.shape[0]

    def load_state(src_ref):
        z = jnp.zeros((HEAD_DIM, HEAD_V), F32)
        for bb in range(nbb):
            for h in range(N_HEADS):
                s = src_ref[bb, h]
                st_ref[bb, h] = jnp.concatenate([s, z], axis=0) if h % 2 == 0 else jnp.concatenate([z, s], axis=0)

    @pl.when((c == 0) & (p == 0))
    def _():
        load_state(sf_ref)

    @pl.when((c == 0) & (p == 1))
    def _():
        load_state(sb_ref)

    for bb in range(nbb):
        sts = [st_ref[bb, h] for h in range(N_HEADS)]
        for u in range(n_sub):
            r0 = pl.multiple_of(jnp.where(p == 0, u, n_sub - 1 - u) * sub, sub)
            rows = pl.ds(r0, sub)
            q = q_ref[bb, rows, :]
            k = k_ref[bb, rows, :]
            v = v_ref[bb, rows, :]
            outs = []
            for hp in range(N_HEADS // 2):
                qp = q[:, hp * LANES:(hp + 1) * LANES].astype(F32)
                kp = k[:, hp * LANES:(hp + 1) * LANES]
                kdp = (kp.astype(F32) * kd_ref[p, hp]).astype(BF)
                for h in (2 * hp, 2 * hp + 1):
                    vh = v[:, h * HEAD_V:(h + 1) * HEAD_V]
                    qm = (qp * qm_ref[h]).astype(BF)
                    qdq = (qp * qd_ref[p, h]).astype(BF)
                    s = _nt_dot(qm, kp) * dm_ref[p, h]
                    o = (jnp.dot(s.astype(BF), vh, preferred_element_type=F32)
                         + jnp.dot(qdq, sts[h].astype(BF), preferred_element_type=F32))
                    sts[h] = sts[h] * cd_ref[p, h] + _tn_dot(kdp, vh)
                    outs.append(o)
            tmp_ref[bb, rows, :] = jnp.concatenate(outs, axis=1)
        for h in range(N_HEADS):
            st_ref[bb, h] = sts[h]
    idx = jnp.where(p == 0, c, n_chunks - 1 - c)

    @pl.when(p == 0)
    def _():
        for bb in range(nbb):
            acc_ref[bb, idx] = tmp_ref[bb]

    @pl.when(p == 1)
    def _():
        for bb in range(nbb):
            tot = acc_ref[bb, idx] + tmp_ref[bb]
            sg = sg_ref[bb].astype(F32)
            ys = []
            for h in range(N_HEADS):
                oh = tot[:, h * HEAD_V:(h + 1) * HEAD_V]
                mu = jnp.mean(oh, axis=-1, keepdims=True)
                dlt = oh - mu
                var = jnp.mean(dlt * dlt, axis=-1, keepdims=True)
                ys.append(dlt * lax.rsqrt(var + EPS))
            o_ref[bb] = (jnp.concatenate(ys, axis=1) * sg).astype(BF)

    def store_state(dst_ref):
        for bb in range(nbb):
            for h in range(N_HEADS):
                s0 = (h % 2) * HEAD_DIM
                dst_ref[bb, h] = st_ref[bb, h, s0:s0 + HEAD_DIM, :]

    @pl.when((c == n_chunks - 1) & (p == 0))
    def _():
        store_state(sfo_ref)

    @pl.when((c == n_chunks - 1) & (p == 1))
    def _():
        store_state(sbo_ref)


def _retention(rq, rk, rv, sg, decays, s_f, s_b, chunk):
    b, n, _ = rq.shape
    n_chunks = n // chunk
    wv = rv.shape[2]

    def seq_map(i, p, c):
        return (i, jnp.where(p == 0, c, n_chunks - 1 - c), 0)

    def out_map(i, p, c):
        return (i, jnp.where(p == 0, n_chunks - 1, n_chunks - 1 - c), 0)

    nbb = 2 if b % 2 == 0 else 1
    st_spec = pl.BlockSpec((nbb, N_HEADS, HEAD_DIM, HEAD_V), lambda i, p, c: (i, 0, 0, 0))
    st_shape = jax.ShapeDtypeStruct((b, N_HEADS, HEAD_DIM, HEAD_V), F32)
    sub = min(RET_SUB, chunk)
    kern = functools.partial(_ret_kernel, chunk=chunk, n_chunks=n_chunks, sub=sub)
    return pl.pallas_call(
        kern,
        grid=(b // nbb, 2, n_chunks),
        in_specs=[pl.BlockSpec(memory_space=pltpu.SMEM),
                  pl.BlockSpec((nbb, chunk, rq.shape[2]), seq_map),
                  pl.BlockSpec((nbb, chunk, rk.shape[2]), seq_map),
                  pl.BlockSpec((nbb, chunk, wv), seq_map),
                  pl.BlockSpec((nbb, chunk, wv), seq_map),
                  st_spec, st_spec],
        out_specs=[pl.BlockSpec((nbb, chunk, wv), out_map), st_spec, st_spec],
        out_shape=[jax.ShapeDtypeStruct((b, n, wv), BF), st_shape, st_shape],
        scratch_shapes=[pltpu.VMEM((nbb, n_chunks, chunk, wv), F32),
                        pltpu.VMEM((nbb, chunk, wv), F32),
                        pltpu.VMEM((nbb, N_HEADS, LANES, HEAD_V), F32),
                        pltpu.VMEM((2, N_HEADS, sub, sub), F32),
                        pltpu.VMEM((2, N_HEADS, sub, LANES), F32),
                        pltpu.VMEM((N_HEADS, sub, LANES), F32),
                        pltpu.VMEM((2, N_HEADS // 2, sub, LANES), F32),
                        pltpu.VMEM((2, N_HEADS, LANES, HEAD_V), F32)],
        compiler_params=_cparams(("arbitrary", "arbitrary", "arbitrary")),
        name="retention",
    )(decays, rq, rk, rv, sg, s_f, s_b)


def _dattn_kernel(dl_ref, q_ref, *refs, n_src, lam_init, sub_q):
    k_refs = refs[:n_src]
    v_refs = refs[n_src:2 * n_src]
    o_ref = refs[2 * n_src]
    dl = dl_ref[...]
    lam = (jnp.exp(jnp.sum(dl[0:1] * dl[1:2], axis=-1, keepdims=True))
           - jnp.exp(jnp.sum(dl[2:3] * dl[3:4], axis=-1, keepdims=True)) + lam_init)
    tq = min(sub_q, q_ref.shape[1])
    lane = lax.broadcasted_iota(I32, (tq, LANES), 1)
    for r0 in range(0, q_ref.shape[1], tq):
        q = q_ref[0, r0:r0 + tq, :].astype(F32)
        ys = []
        for h in range(N_HEADS):
            qh = q[:, h * LANES:(h + 1) * LANES]
            qs = jnp.concatenate([jnp.where(lane < HEAD_DIM, qh, 0.0), jnp.where(lane >= HEAD_DIM, qh, 0.0)],
                                 axis=0).astype(BF)
            ss = [_nt_dot(qs, kr[0, :, h * LANES:(h + 1) * LANES]) for kr in k_refs]
            m = ss[0].max(axis=-1, keepdims=True)
            for s in ss[1:]:
                m = jnp.maximum(m, s.max(axis=-1, keepdims=True))
            acc = jnp.zeros((2 * tq, 2 * HEAD_V), F32)
            for s, vr in zip(ss, v_refs):
                pexp = jnp.exp2(s - m).astype(BF)
                acc = acc + jnp.dot(pexp, vr[0, :, 2 * h * HEAD_V:(2 * h + 2) * HEAD_V],
                                    preferred_element_type=F32)
            o = acc[:, :HEAD_V] / acc[:, HEAD_V:]
            oh = o[:tq] - lam * o[tq:]
            ms = jnp.mean(oh * oh, axis=-1, keepdims=True)
            ys.append(oh * lax.rsqrt(ms + EPS) * (1.0 - lam_init))
        o_ref[0, r0:r0 + tq, :] = jnp.concatenate(ys, axis=1).astype(BF)


def _diff_attention(dq, ks, vs, dlam, lam_init, tq, sub_q=256):
    b, n, w = dq.shape
    n_src = len(ks)
    kern = functools.partial(_dattn_kernel, n_src=n_src, lam_init=lam_init, sub_q=sub_q)
    kv_specs = [pl.BlockSpec((1, a.shape[1], a.shape[2]), lambda i, j: (i, 0, 0)) for a in (*ks, *vs)]
    return pl.pallas_call(
        kern,
        grid=(b, n // tq),
        in_specs=[pl.BlockSpec(dlam.shape, lambda i, j: (0, 0)),
                  pl.BlockSpec((1, tq, w), lambda i, j: (i, j, 0))] + kv_specs,
        out_specs=pl.BlockSpec((1, tq, w), lambda i, j: (i, j, 0)),
        out_shape=jax.ShapeDtypeStruct((b, n, w), BF),
        compiler_params=_cparams(("arbitrary", "arbitrary")),
        name="diff_attention",
    )(dlam, dq, *ks, *vs)


def _merge_kernel(f_ref, ro_ref, do_ref, gt_ref, x_ref, m2_ref, m3_ref, m4_ref, g_ref,
                  wf_ref, wr_ref, wd_ref, wo_ref, rcat_ref, x1_ref, h2_ref, aff_ref):
    tm, d = x_ref.shape
    st = min(MERGE_SUB, tm)
    for r0 in range(0, tm, st):
        rows = slice(r0, r0 + st)
        t = (gt_ref[rows, 0:d].astype(F32) * jnp.dot(f_ref[rows, :], wf_ref[...], preferred_element_type=F32)
             + gt_ref[rows, d:2 * d].astype(F32) * jnp.dot(ro_ref[rows, :], wr_ref[...], preferred_element_type=F32)
             + gt_ref[rows, 2 * d:3 * d].astype(F32) * jnp.dot(do_ref[rows, :], wd_ref[...],
                                                                preferred_element_type=F32))
        mix = jnp.dot(t.astype(BF), wo_ref[...], preferred_element_type=F32)
        x1 = x_ref[rows, :] + m2_ref[0] * mix
        x1_ref[rows, :] = x1
        ms = jnp.mean(x1 * x1, axis=-1, keepdims=True)
        h2 = x1 * lax.rsqrt(ms + EPS) * g_ref[...]
        h2 = h2 * (1.0 + m4_ref[0]) + m3_ref[0]
        hi = h2.astype(BF)
        lo = (h2 - hi.astype(F32)).astype(BF)
        h2_ref[rows, :] = hi
        l1 = jnp.dot(hi, rcat_ref[...], preferred_element_type=F32)
        lt = l1[:, :LANES] + l1[:, LANES:] + jnp.dot(lo, rcat_ref[:, :LANES], preferred_element_type=F32)
        lane = lax.broadcasted_iota(I32, lt.shape, 1)
        lt = jnp.where(lane < N_EXPERTS, lt, -jnp.inf)
        ex = jnp.exp(lt - lt.max(axis=-1, keepdims=True))
        aff = ex / jnp.sum(ex, axis=-1, keepdims=True)
        aff_ref[0, :, rows] = aff.T[:N_EXPERTS, :]


def _merge(f, ro, do, gt, x2d, m2, m3, m4, g, wf, wr, wd, wo, rcat, n_seq, tm):
    rows, d = x2d.shape
    tiles_per_b = n_seq // tm
    nb = m2.shape[0]
    bsz = rows // n_seq
    ne = N_EXPERTS

    def mod_map(i):
        return ((i // tiles_per_b) if nb > 1 else 0, 0, 0)

    const2 = lambda i: (0, 0)
    row_spec = lambda w: pl.BlockSpec((tm, w), lambda i: (i, 0))
    return pl.pallas_call(
        _merge_kernel,
        grid=(rows // tm,),
        in_specs=[row_spec(f.shape[1]), row_spec(ro.shape[1]), row_spec(do.shape[1]), row_spec(gt.shape[1]),
                  row_spec(d),
                  pl.BlockSpec((1, 1, d), mod_map), pl.BlockSpec((1, 1, d), mod_map), pl.BlockSpec((1, 1, d), mod_map),
                  pl.BlockSpec((1, d), const2),
                  pl.BlockSpec(wf.shape, const2), pl.BlockSpec(wr.shape, const2), pl.BlockSpec(wd.shape, const2),
                  pl.BlockSpec(wo.shape, const2), pl.BlockSpec(rcat.shape, const2)],
        out_specs=[row_spec(d), row_spec(d),
                   pl.BlockSpec((1, ne, tm), lambda i: (i // tiles_per_b, 0, i % tiles_per_b))],
        out_shape=[jax.ShapeDtypeStruct((rows, d), F32), jax.ShapeDtypeStruct((rows, d), BF),
                   jax.ShapeDtypeStruct((bsz, ne, n_seq), F32)],
        compiler_params=_cparams(("arbitrary",)),
        name="merge_router",
    )(f, ro, do, gt, x2d, m2, m3, m4, g, wf, wr, wd, wo, rcat)


def _route_kernel(a_ref, slot_ref, offs_ref, *, cap, blk):
    a = a_ref[...]
    ne, n = a.shape
    capf = float(cap)

    def enough(t):
        return jnp.sum(jnp.where(a >= t, 1.0, 0.0), axis=-1, keepdims=True) >= capf

    tiny = jnp.full((ne, 1), F32_MIN_NORMAL, F32)
    found = enough(tiny)
    cur = tiny
    for step in (64, 32, 16, 8, 4, 2, 1):
        cand = cur * (2.0 ** step)
        cur = jnp.where(enough(cand), cand, cur)
    base = cur

    def mantissa_bit(_, carry):
        cur, stepv = carry
        cand = cur + stepv
        return jnp.where(enough(cand), cand, cur), stepv * 0.5

    cur, ulp = lax.fori_loop(0, 23, mantissa_bit, (cur, base * 0.5))
    lo = jnp.where(found, cur, 0.0)
    hi = jnp.where(found, cur + ulp * 2.0, tiny)

    def refine(_, carry):
        lo, hi = carry
        mid = lo + (hi - lo) * 0.5
        ok = enough(mid)
        return jnp.where(ok, mid, lo), jnp.where(ok, hi, mid)

    lo, hi = lax.fori_loop(0, 24, refine, (lo, hi))
    ri = lax.broadcasted_iota(I32, (blk, blk), 0)
    ci = lax.broadcasted_iota(I32, (blk, blk), 1)
    upper = jnp.where(ri < ci, 1.0, 0.0).astype(BF)

    def excl_cumsum(m):
        carry = jnp.zeros((ne, 1), F32)
        outs = []
        for j in range(n // blk):
            mb = m[:, j * blk:(j + 1) * blk]
            outs.append(jnp.dot(mb.astype(BF), upper, preferred_element_type=F32) + carry)
            carry = carry + jnp.sum(mb, axis=-1, keepdims=True)
        return jnp.concatenate(outs, axis=1)

    gt = a >= hi
    tie = (a >= lo) & (a < hi)
    need = capf - jnp.sum(jnp.where(gt, 1.0, 0.0), axis=-1, keepdims=True)
    sel = gt | (tie & (excl_cumsum(jnp.where(tie, 1.0, 0.0)) < need))
    selm = jnp.where(sel, 1.0, 0.0)
    slot = excl_cumsum(selm)
    slot_ref[...] = jnp.where(sel, slot.astype(I32), -1)
    tok = lax.broadcasted_iota(I32, (n, LANES), 0)
    blk_start = lax.broadcasted_iota(I32, (n, LANES), 1) * blk
    before = jnp.where((tok < blk_start) & (blk_start <= n), 1.0, 0.0).astype(BF)
    offs_ref[...] = jnp.dot(selm.astype(BF), before, preferred_element_type=F32).astype(I32)


def _route(aff_t, cap):
    b, ne, n = aff_t.shape
    kern = functools.partial(_route_kernel, cap=cap, blk=min(TOKEN_BLOCK, n))
    slot, offs = pl.pallas_call(
        kern,
        grid=(1,),
        in_specs=[pl.BlockSpec((b * ne, n), lambda i: (0, 0))],
        out_specs=[pl.BlockSpec((b * ne, n), lambda i: (0, 0)), pl.BlockSpec((b * ne, LANES), lambda i: (0, 0))],
        out_shape=[jax.ShapeDtypeStruct((b * ne, n), I32), jax.ShapeDtypeStruct((b * ne, LANES), I32)],
        compiler_params=_cparams(("arbitrary",)),
        name="route",
    )(aff_t.reshape(b * ne, n))
    return slot.reshape(b, ne, n), offs


def _window_start(off, cap, win):
    aligned = lax.shift_left(lax.shift_right_logical(off, BF16_ROWS_LOG2), BF16_ROWS_LOG2)
    return jnp.minimum(aligned, cap - win)


def _windows_fit(offs_ref, row0, ne, blk0, nblk, cap, win):
    bad = jnp.int32(0)
    for e in range(ne):
        for jj in range(nblk):
            j = blk0 + jj
            off = offs_ref[row0 + e, j]
            end = offs_ref[row0 + e, j + 1]
            bad = bad | (end - _window_start(off, cap, win) > win).astype(I32)
    return bad == 0


def _gather_kernel(offs_ref, slot_ref, h_ref, xs_ref, *, cap, win, tb):
    ne, n = slot_ref.shape[1], slot_ref.shape[3]
    nblk = n // tb
    row0 = pl.program_id(0) * ne
    fits = _windows_fit(offs_ref, row0, ne, 0, nblk, cap, win)

    @pl.when(fits)
    def _windowed():
        xs_ref[...] = jnp.zeros(xs_ref.shape, BF)
        rows = lax.broadcasted_iota(I32, (win, tb), 0)
        for j in range(nblk):
            starts, lhs = [], []
            for e in range(ne):
                w0 = pl.multiple_of(_window_start(offs_ref[row0 + e, j], cap, win), 1 << BF16_ROWS_LOG2)
                starts.append(w0)
                lhs.append(jnp.where(slot_ref[0, e, :, j * tb:(j + 1) * tb] == rows + w0, 1.0, 0.0).astype(BF))
            got = jnp.dot(jnp.concatenate(lhs, axis=0), h_ref[0, j * tb:(j + 1) * tb, :],
                          preferred_element_type=F32)
            for e in range(ne):
                dst = xs_ref.at[0, e, pl.ds(starts[e], win), :]
                dst[...] = (dst[...].astype(F32) + got[e * win:(e + 1) * win]).astype(BF)

    @pl.when(jnp.logical_not(fits))
    def _dense():
        sub = lax.broadcasted_iota(I32, (cap, n), 0)
        hb = h_ref[0]
        for e in range(ne):
            onehot = jnp.where(sub == slot_ref[0, e], 1.0, 0.0).astype(BF)
            xs_ref[0, e] = jnp.dot(onehot, hb, preferred_element_type=F32).astype(BF)


def _gather(slot, offs, h2, cap):
    b, ne, n = slot.shape
    d = h2.shape[2]
    tb = min(TOKEN_BLOCK, n)
    kern = functools.partial(_gather_kernel, cap=cap, win=min(SLOT_WINDOW, cap), tb=tb)
    grid_spec = pltpu.PrefetchScalarGridSpec(
        num_scalar_prefetch=1,
        grid=(b,),
        in_specs=[pl.BlockSpec((1, ne, 1, n), lambda i, offs: (i, 0, 0, 0)),
                  pl.BlockSpec((1, n, d), lambda i, offs: (i, 0, 0))],
        out_specs=pl.BlockSpec((1, ne, cap, d), lambda i, offs: (i, 0, 0, 0)))
    return pl.pallas_call(
        kern,
        grid_spec=grid_spec,
        out_shape=jax.ShapeDtypeStruct((b, ne, cap, d), BF),
        compiler_params=_cparams(("arbitrary",)),
        name="moe_gather",
    )(offs, slot.reshape(b, ne, 1, n), h2)


def _ffn_kernel(*refs, n_src, n_fc):
    xs_refs = refs[:n_src]
    wg_ref, wu_ref, wd_ref = refs[n_src:n_src + 3]
    ye_refs = refs[n_src + 3:2 * n_src + 3]
    x_ref, hid_ref = refs[2 * n_src + 3:]
    j = pl.program_id(1)
    fc = wg_ref.shape[3]
    row_spans = []
    r0 = 0
    for xr in xs_refs:
        rows = xr.shape[0] * xr.shape[2]
        row_spans.append((r0, rows))
        r0 += rows

    @pl.when(j == 0)
    def _():
        for xr, (s0, rows) in zip(xs_refs, row_spans):
            x_ref[s0:s0 + rows, :] = xr[:, 0].reshape(rows, xr.shape[3])

    @pl.when(j < n_fc)
    def _():
        x = x_ref[...]
        g = jnp.dot(x, wg_ref[0, 0].astype(BF), preferred_element_type=F32)
        u = jnp.dot(x, wu_ref[0, 0].astype(BF), preferred_element_type=F32)
        hid_ref[j] = (g * _sigmoid(g) * u).astype(BF)

    @pl.when(j >= n_fc)
    def _():
        y = jnp.dot(hid_ref[0], wd_ref[0, 0, 0:fc, :].astype(BF), preferred_element_type=F32)
        for k in range(1, n_fc):
            y = y + jnp.dot(hid_ref[k], wd_ref[0, 0, k * fc:(k + 1) * fc, :].astype(BF), preferred_element_type=F32)
        for yr, (s0, rows) in zip(ye_refs, row_spans):
            yr[:, 0] = y[s0:s0 + rows].reshape(yr.shape[0], yr.shape[2], yr.shape[3]).astype(BF)


def _expert_ffn(xs_list, w_gate, w_up, w_down, layer, fchunk, ochunk):
    ne, d = xs_list[0].shape[1], xs_list[0].shape[3]
    f = w_gate.shape[3]
    n_fc, n_oc = f // fchunk, d // ochunk
    n_src = len(xs_list)
    total_rows = sum(x.shape[0] * x.shape[2] for x in xs_list)
    kern = functools.partial(_ffn_kernel, n_src=n_src, n_fc=n_fc)
    up_map = lambda e, j: (layer, e, 0, jnp.minimum(j, n_fc - 1))
    down_map = lambda e, j: (layer, e, 0, jnp.maximum(j - n_fc, 0))
    out_map = lambda e, j: (0, e, 0, jnp.maximum(j - n_fc, 0))
    outs = pl.pallas_call(
        kern,
        grid=(ne, n_fc + n_oc),
        in_specs=[pl.BlockSpec((x.shape[0], 1, x.shape[2], d), lambda e, j: (0, e, 0, 0), pipeline_mode=pl.Buffered(1))
                  for x in xs_list]
        + [pl.BlockSpec((1, 1, d, fchunk), up_map),
           pl.BlockSpec((1, 1, d, fchunk), up_map),
           pl.BlockSpec((1, 1, f, ochunk), down_map)],
        out_specs=[pl.BlockSpec((x.shape[0], 1, x.shape[2], ochunk), out_map) for x in xs_list],
        out_shape=[jax.ShapeDtypeStruct(x.shape, BF) for x in xs_list],
        scratch_shapes=[pltpu.VMEM((total_rows, d), BF), pltpu.VMEM((n_fc, total_rows, fchunk), BF)],
        compiler_params=_cparams(("arbitrary", "arbitrary")),
        name="expert_ffn",
    )(*xs_list, w_gate, w_up, w_down)
    return list(outs)


def _combine_kernel(offs_ref, x1_ref, m5_ref, slot_ref, aff_ref, ye_ref, o_ref, ywin_ref, *, cap, win, tb):
    ne, tn = slot_ref.shape[1], slot_ref.shape[3]
    row0 = pl.program_id(0) * ne
    blk0 = pl.program_id(1) * (tn // tb)
    fits = _windows_fit(offs_ref, row0, ne, blk0, tn // tb, cap, win)

    @pl.when(fits)
    def _windowed():
        rows = lax.broadcasted_iota(I32, (win, tb), 0)
        for jj in range(tn // tb):
            tok = slice(jj * tb, (jj + 1) * tb)
            pieces = []
            for e in range(ne):
                w0 = pl.multiple_of(_window_start(offs_ref[row0 + e, blk0 + jj], cap, win), 1 << BF16_ROWS_LOG2)
                ywin_ref[e * win:(e + 1) * win, :] = ye_ref[0, e, pl.ds(w0, win), :]
                pieces.append(jnp.where(slot_ref[0, e, :, tok] == rows + w0, aff_ref[0, e, :, tok], 0.0).astype(BF))
            acc = _tn_dot(jnp.concatenate(pieces, axis=0), ywin_ref[...])
            o_ref[0, tok, :] = x1_ref[0, tok, :] + m5_ref[0] * acc

    @pl.when(jnp.logical_not(fits))
    def _dense():
        rows = lax.broadcasted_iota(I32, (cap, tn), 0)
        acc = jnp.zeros(x1_ref.shape[1:], F32)
        for e in range(ne):
            weighted = jnp.where(slot_ref[0, e] == rows, aff_ref[0, e], 0.0).astype(BF)
            acc = acc + _tn_dot(weighted, ye_ref[0, e])
        o_ref[0] = x1_ref[0] + m5_ref[0] * acc


def _combine(x1, m5, slot, aff_t, offs, ye, tn):
    b, n, d = x1.shape
    ne, cap = ye.shape[1], ye.shape[2]
    nb = m5.shape[0]
    tb = min(TOKEN_BLOCK, n)
    win = min(SLOT_WINDOW, cap)
    kern = functools.partial(_combine_kernel, cap=cap, win=win, tb=tb)
    grid_spec = pltpu.PrefetchScalarGridSpec(
        num_scalar_prefetch=1,
        grid=(b, n // tn),
        in_specs=[pl.BlockSpec((1, tn, d), lambda i, j, offs: (i, j, 0)),
                  pl.BlockSpec((1, 1, d), lambda i, j, offs: (i if nb > 1 else 0, 0, 0)),
                  pl.BlockSpec((1, ne, 1, tn), lambda i, j, offs: (i, 0, 0, j)),
                  pl.BlockSpec((1, ne, 1, tn), lambda i, j, offs: (i, 0, 0, j)),
                  pl.BlockSpec((1, ne, cap, d), lambda i, j, offs: (i, 0, 0, 0))],
        out_specs=pl.BlockSpec((1, tn, d), lambda i, j, offs: (i, j, 0)),
        scratch_shapes=[pltpu.VMEM((ne * win, d), BF)])
    return pl.pallas_call(
        kern,
        grid_spec=grid_spec,
        out_shape=jax.ShapeDtypeStruct((b, n, d), F32),
        compiler_params=_cparams(("arbitrary", "arbitrary")),
        name="moe_combine",
    )(offs, x1, m5, slot.reshape(b, ne, 1, n), aff_t.reshape(b, ne, 1, n), ye)


def _rope_tables(n, use_rope):
    if not use_rope:
        return jnp.ones((n, LANES), F32), jnp.zeros((n, LANES), F32)
    rows = n // GRID_W
    row = jnp.repeat(jnp.arange(rows, dtype=F32), GRID_W)
    col = jnp.tile(jnp.arange(GRID_W, dtype=F32), rows)
    inv = ROPE_BASE ** (-jnp.arange(ROPE_FREQS_PER_AXIS, dtype=F32) / ROPE_FREQS_PER_AXIS)
    ang = jnp.concatenate([row[:, None] * inv, col[:, None] * inv], axis=-1)
    cos, sin = jnp.cos(ang), jnp.sin(ang)
    cos_t = jnp.tile(cos, (1, LANES // (HEAD_DIM // 2)))
    sin_t = jnp.tile(jnp.concatenate([-sin, sin], axis=-1), (1, LANES // HEAD_DIM))
    return cos_t, sin_t


def _dft_tables(n):
    n0 = 64
    n1 = n // n0
    k = np.arange(n, dtype=np.int64)[:, None]
    ang1 = jnp.asarray(2.0 * np.pi * ((k * np.arange(n1 // 2)[None, :]) % n1) / n1, F32)
    ang0 = jnp.asarray(2.0 * np.pi * ((k * np.arange(n0)[None, :]) % n) / n, F32)
    c1, s1, c0, s0 = jnp.cos(ang1), jnp.sin(ang1), jnp.cos(ang0), jnp.sin(ang0)
    scale = 1.0 / math.sqrt(n)
    cos_n = (c1[:, :, None] * c0[:, None, :] - s1[:, :, None] * s0[:, None, :]).reshape(n, n // 2) * scale
    sin_n = (s1[:, :, None] * c0[:, None, :] + c1[:, :, None] * s0[:, None, :]).reshape(n, n // 2) * scale
    return jnp.concatenate([cos_n, -sin_n], axis=1).astype(BF)


def _fold_matrix(fb):
    i = np.arange(fb)[:, None]
    s = np.arange(2 * fb)[None, :]
    return jnp.asarray(np.where(s == fb - i, 1.0, 0.0), BF)


def _group_dft_tables(width):
    g = FNET_GROUP_DIM
    idx = np.arange(width)
    same = (idx[:, None] // g) == (idx[None, :] // g)
    ang = 2.0 * np.pi * (((idx[:, None] % g) * (idx[None, :] % g)) % g) / g
    scale = 1.0 / math.sqrt(g)
    bdc = np.where(same, np.cos(ang), 0.0) * scale
    bds = np.where(same, np.sin(ang), 0.0) * scale
    return jnp.asarray(bdc, F32).astype(BF), jnp.asarray(bds, F32).astype(BF)


def _group_mean_matrix(width):
    idx = np.arange(width)
    same = (idx[:, None] // HEAD_DIM) == (idx[None, :] // HEAD_DIM)
    return jnp.asarray(np.where(same, 1.0 / HEAD_DIM, 0.0), BF)


def _mixer_inputs(x3, mods, g_attn_l, w_in_bf, tables, qg, kg, gm, kv_only=False):
    b, n, d = x3.shape
    tm = min(512, n)
    outs = _in_projection(x3.reshape(b * n, d), mods[0], mods[1], g_attn_l, w_in_bf, tables[0], tables[1],
                          qg, kg, gm, n, tm, kv_only)
    return [o.reshape(b, n, o.shape[1]) for o in outs]


def _moe(sets, w_gate, w_up, w_down, layer):
    slots, offss, xss = [], [], []
    for x1, h2, aff_t, _ in sets:
        cap = EC_FACTOR * x1.shape[1] // N_EXPERTS
        slot, offs = _route(aff_t, cap)
        offs = offs[:, :OFFS_LANES]
        slots.append(slot)
        offss.append(offs)
        xss.append(_gather(slot, offs, h2, cap))
    yes = _expert_ffn(xss, w_gate, w_up, w_down, layer, 512, 512)
    return [_combine(x1, m5, slot, aff_t, offs, ye, min(512, x1.shape[1]))
            for (x1, _, aff_t, m5), slot, offs, ye in zip(sets, slots, offss, yes)]


def kernel(x, c, ctx, c_ctx, w_mod, b_mod, g_attn, g_ffn, w_in, ret_decay, diff_qn, diff_kn, diff_lambda,
           w_fnet_o, w_ret_o, w_diff_o, w_out, w_router, w_exp_gate, w_exp_up, w_exp_down):
    bsz, n, d = x.shape
    n_ctx = ctx.shape[1]
    depth = w_mod.shape[0]

    pad = (-(bsz + 1)) % 8
    cvecs = jnp.concatenate([c, c_ctx[None, :], jnp.zeros((pad, d), F32)], axis=0)
    mods = _modulation(cvecs, w_mod, b_mod)

    rope_lat = _rope_tables(n, True)
    rope_ctx = _rope_tables(n_ctx, False)
    cs_lat, cs_ctx = _dft_tables(n), _dft_tables(n_ctx)
    bdc, bds = _group_dft_tables(2 * LANES)
    jsh = _fold_matrix(LANES)
    gm = _group_mean_matrix(2 * LANES)
    zero_state = jnp.zeros((bsz, N_HEADS, HEAD_DIM, HEAD_V), F32)
    ret_chunk = 1024

    xc = ctx
    for layer in range(depth):
        last = layer == depth - 1
        lam_init = 0.8 - 0.6 * math.exp(-0.3 * layer)
        mx = [mods[layer, :bsz, j * d:(j + 1) * d].reshape(bsz, 1, d) for j in range(N_MOD)]
        mc = [mods[layer, bsz:bsz + 1, j * d:(j + 1) * d].reshape(1, 1, d) for j in range(N_MOD)]
        w_in_bf = w_in[layer].astype(BF)
        qg = jnp.tile(diff_qn[layer], _W_DQ // HEAD_DIM)[None, :]
        kg = jnp.tile(diff_kn[layer], _W_DK // HEAD_DIM)[None, :]
        g_a = g_attn[layer][None, :]
        g_f = g_ffn[layer][None, :]
        decays = ret_decay[layer]
        dlam = diff_lambda[layer]
        branch_w = (w_fnet_o[layer].astype(BF), w_ret_o[layer].astype(BF), w_diff_o[layer].astype(BF),
                    w_out[layer].astype(BF))
        wr32 = w_router[layer]
        wr_hi = wr32.astype(BF)
        wr_lo = (wr32 - wr_hi.astype(F32)).astype(BF)
        lane_pad = jnp.zeros((d, LANES - N_EXPERTS), BF)
        rcat = jnp.concatenate([wr_hi, lane_pad, wr_lo, lane_pad], axis=1)
        moe_sets = []

        if last:
            rk_c, rv_c, dk_c, dv_c = _mixer_inputs(xc, mc, g_a, w_in_bf, rope_ctx, qg, kg, gm, kv_only=True)
            rq_c, sg_c = rk_c, rv_c
        else:
            fo_c, rq_c, rk_c, rv_c, sg_c, dq_c, dk_c, dv_c, gt_c = _mixer_inputs(xc, mc, g_a, w_in_bf, rope_ctx,
                                                                               qg, kg, gm)
        ro_c, s_f, s_b = _retention(rq_c, rk_c, rv_c, sg_c, decays, zero_state, zero_state, min(ret_chunk, n_ctx))
        if not last:
            f_c = _fourier_mix(fo_c, cs_ctx, bdc, bds, jsh, min(512, n_ctx))
            do_c = _diff_attention(dq_c, [dk_c], [dv_c], dlam, lam_init, min(256, n_ctx))
            rows_c = bsz * n_ctx
            x1_c, h2_c, aff_c = _merge(f_c.reshape(rows_c, -1), ro_c.reshape(rows_c, -1), do_c.reshape(rows_c, -1),
                                       gt_c.reshape(rows_c, -1), xc.reshape(rows_c, d), mc[2], mc[3], mc[4], g_f,
                                       *branch_w, rcat, n_ctx, min(512, n_ctx))
            moe_sets.append((x1_c.reshape(bsz, n_ctx, d), h2_c.reshape(bsz, n_ctx, d), aff_c, mc[5]))

        fo, rq, rk, rv, sg, dq, dk, dv, gt = _mixer_inputs(x, mx, g_a, w_in_bf, rope_lat, qg, kg, gm)
        ro, _, _ = _retention(rq, rk, rv, sg, decays, s_f, s_b, min(ret_chunk, n))
        f = _fourier_mix(fo, cs_lat, bdc, bds, jsh, min(512, n))
        do = _diff_attention(dq, [dk, dk_c], [dv, dv_c], dlam, lam_init, min(1024, n))
        rows = bsz * n
        x1, h2, aff = _merge(f.reshape(rows, -1), ro.reshape(rows, -1), do.reshape(rows, -1), gt.reshape(rows, -1),
                             x.reshape(rows, d), mx[2], mx[3], mx[4], g_f, *branch_w, rcat, n, min(1024, n))
        moe_sets.append((x1.reshape(bsz, n, d), h2.reshape(bsz, n, d), aff, mx[5]))
        moe_out = _moe(moe_sets, w_exp_gate, w_exp_up, w_exp_down, layer)
        x = moe_out[-1]
        if not last:
            xc = moe_out[0]
    return x
```

```python
import functools
import math

import jax
import jax.numpy as jnp
import numpy as np
from jax import lax
from jax.experimental import pallas as pl
from jax.experimental.pallas import tpu as pltpu

F32 = jnp.float32
BF = jnp.bfloat16
I32 = jnp.int32

GRID_W = 64
HEAD_DIM = 64
ROPE_FREQS_PER_AXIS = HEAD_DIM // 4
ROPE_BASE = 10000.0
EPS = 1e-6
LOG2E = 1.4426950408889634
FNET_GROUP_DIM = 64
N_HEADS = 4
HEAD_V = 128
N_EXPERTS = 16
EC_FACTOR = 2
N_MOD = 6
RET_SUB = 256
MERGE_SUB = 1024
BF16_ROWS_LOG2 = 4
F32_MIN_NORMAL = 2.0 ** -126
OFFS_LANES = 16
TOKEN_BLOCK = 256
SLOT_WINDOW = 64
LANES = 128
VMEM_LIMIT = 56 * 1024 * 1024

_W_FO, _W_RQ, _W_RK, _W_RV, _W_RG, _W_DQ, _W_DK, _W_DV = 512, 256, 256, 512, 512, 512, 512, 512


def _cparams(sem):
    return pltpu.CompilerParams(dimension_semantics=sem, vmem_limit_bytes=VMEM_LIMIT)


def _sigmoid(v):
    return 1.0 / (1.0 + jnp.exp(-v))


def _nt_dot(a, b):
    return lax.dot_general(a, b, (((1,), (1,)), ((), ())), preferred_element_type=F32)


def _tn_dot(a, b):
    return lax.dot_general(a, b, (((0,), (0,)), ((), ())), preferred_element_type=F32)


def _mod_kernel(c_ref, w_ref, b_ref, o_ref):
    cv = c_ref[...]
    s = cv * _sigmoid(cv)
    o_ref[0] = jnp.dot(s.astype(BF), w_ref[0].astype(BF), preferred_element_type=F32) + b_ref[0]


def _modulation(cvecs, w_mod, b_mod):
    depth, d, wd = w_mod.shape
    rows = cvecs.shape[0]
    tn = 1536
    return pl.pallas_call(
        _mod_kernel,
        grid=(depth, wd // tn),
        in_specs=[pl.BlockSpec((rows, d), lambda l, j: (0, 0)),
                  pl.BlockSpec((1, d, tn), lambda l, j: (l, 0, j)),
                  pl.BlockSpec((1, 1, tn), lambda l, j: (l, 0, j))],
        out_specs=pl.BlockSpec((1, rows, tn), lambda l, j: (l, 0, j)),
        out_shape=jax.ShapeDtypeStruct((depth, rows, wd), F32),
        compiler_params=_cparams(("arbitrary", "arbitrary")),
        name="modulation",
    )(cvecs, w_mod, b_mod.reshape(depth, 1, wd))


def _rope(x, cos_t, sin_t):
    lane = lax.broadcasted_iota(I32, (x.shape[0], LANES), 1)
    first = (lane & 63) < 32
    outs = []
    for j in range(x.shape[1] // LANES):
        xc = x[:, j * LANES:(j + 1) * LANES]
        sw = jnp.where(first, pltpu.roll(xc, LANES - 32, 1), pltpu.roll(xc, 32, 1))
        outs.append(xc * cos_t + sw * sin_t)
    return jnp.concatenate(outs, axis=1)


def _group_rms(x, gm, gain):
    gw = gm.shape[0]
    sq = (x * x).astype(BF)
    ms = jnp.concatenate([jnp.dot(sq[:, c:c + gw], gm, preferred_element_type=F32)
                          for c in range(0, x.shape[1], gw)], axis=1)
    return x * lax.rsqrt(ms + EPS) * gain


def _inproj_kernel(x_ref, shift_ref, scale_ref, g_ref, w_ref, cos_ref, sin_ref, qg_ref, kg_ref, gm_ref,
                   *out_refs, kv_only):
    if kv_only:
        rk_ref, rv_ref, dk_ref, dv_ref = out_refs
    else:
        fo_ref, rq_ref, rk_ref, rv_ref, sg_ref, dq_ref, dk_ref, dv_ref, gt_ref = out_refs
    x = x_ref[...]
    d = x.shape[1]
    ms = jnp.mean(x * x, axis=-1, keepdims=True)
    h = x * lax.rsqrt(ms + EPS) * g_ref[...]
    h = h * (1.0 + scale_ref[0]) + shift_ref[0]
    hb = h.astype(BF)
    cos_t = cos_ref[...]
    sin_t = sin_ref[...]
    gm = gm_ref[...]

    def proj(a, width):
        return jnp.dot(hb, w_ref[:, a:a + width], preferred_element_type=F32)

    a = 0
    if not kv_only:
        fo_ref[...] = proj(a, _W_FO).astype(BF)
    a += _W_FO
    if not kv_only:
        rq_ref[...] = _rope(proj(a, _W_RQ) * (HEAD_DIM ** -0.5), cos_t, sin_t).astype(BF)
    a += _W_RQ
    rk_ref[...] = _rope(proj(a, _W_RK), cos_t, sin_t).astype(BF)
    a += _W_RK
    rv_ref[...] = proj(a, _W_RV).astype(BF)
    a += _W_RV
    if not kv_only:
        rg = proj(a, _W_RG)
        sg_ref[...] = (rg * _sigmoid(rg)).astype(BF)
    a += _W_RG
    if not kv_only:
        dq = _group_rms(proj(a, _W_DQ), gm, qg_ref[...])
        dq_ref[...] = (_rope(dq, cos_t, sin_t) * (HEAD_DIM ** -0.5 * LOG2E)).astype(BF)
    a += _W_DQ
    dk = _group_rms(proj(a, _W_DK), gm, kg_ref[...])
    dk_ref[...] = _rope(dk, cos_t, sin_t).astype(BF)
    a += _W_DK
    dv = proj(a, _W_DV).astype(BF)
    ones = jnp.ones((dv.shape[0], HEAD_V), BF)
    for hh in range(N_HEADS):
        dv_ref[:, 2 * hh * HEAD_V:(2 * hh + 1) * HEAD_V] = dv[:, hh * HEAD_V:(hh + 1) * HEAD_V]
        dv_ref[:, (2 * hh + 1) * HEAD_V:(2 * hh + 2) * HEAD_V] = ones
    a += _W_DV
    if not kv_only:
        for j in range(3):
            gl = proj(a + j * d, d)
            gt_ref[:, j * d:(j + 1) * d] = _sigmoid(gl).astype(BF)


def _in_projection(x2d, shift, scale, g, w_in_bf, cos_t, sin_t, qg, kg, gm, n_seq, tm, kv_only=False):
    rows, d = x2d.shape
    tiles_per_b = n_seq // tm
    nb = shift.shape[0]
    win = w_in_bf.shape[1]
    if kv_only:
        widths = (_W_RK, _W_RV, _W_DK, 2 * _W_DV)
    else:
        widths = (_W_FO, _W_RQ, _W_RK, _W_RV, _W_RG, _W_DQ, _W_DK, 2 * _W_DV, 3 * d)

    def mod_map(i):
        return ((i // tiles_per_b) if nb > 1 else 0, 0, 0)

    def pos_map(i):
        return (i % tiles_per_b, 0)

    const2 = lambda i: (0, 0)
    return pl.pallas_call(
        functools.partial(_inproj_kernel, kv_only=kv_only),
        grid=(rows // tm,),
        in_specs=[pl.BlockSpec((tm, d), lambda i: (i, 0)),
                  pl.BlockSpec((1, 1, d), mod_map),
                  pl.BlockSpec((1, 1, d), mod_map),
                  pl.BlockSpec((1, d), const2),
                  pl.BlockSpec((d, win), const2, pipeline_mode=pl.Buffered(1)),
                  pl.BlockSpec((tm, LANES), pos_map),
                  pl.BlockSpec((tm, LANES), pos_map),
                  pl.BlockSpec((1, _W_DQ), const2),
                  pl.BlockSpec((1, _W_DK), const2),
                  pl.BlockSpec(gm.shape, const2)],
        out_specs=[pl.BlockSpec((tm, w), lambda i: (i, 0)) for w in widths],
        out_shape=[jax.ShapeDtypeStruct((rows, w), BF) for w in widths],
        compiler_params=_cparams(("arbitrary",)),
        name="in_projection",
    )(x2d, shift, scale, g, w_in_bf, cos_t, sin_t, qg, kg, gm)


def _fourier_kernel(*refs):
    _fourier_fold(pl.program_id(1), *refs)
    _fourier_rows(*refs)


def _fourier_fold(step, x_ref, cs_ref, bc_ref, bs_ref, jsh_ref, o_ref, z_ref, eo_ref):
    n, w = x_ref.shape[1], x_ref.shape[2]
    half = n // 2
    gw = bc_ref.shape[0]
    fb = jsh_ref.shape[0]

    @pl.when(step == 0)
    def _():
        x = x_ref[0]
        for g in range(w // gw):
            xs = x[:, g * gw:(g + 1) * gw]
            z_ref[0, :, g * gw:(g + 1) * gw] = jnp.dot(xs, bc_ref[...], preferred_element_type=F32).astype(BF)
            z_ref[1, :, g * gw:(g + 1) * gw] = jnp.dot(xs, bs_ref[...], preferred_element_type=F32).astype(BF)
        jsh = jsh_ref[...]
        n_blocks = half // fb
        for blk in range(n_blocks):
            hi_blk = 2 * n_blocks - 1 - blk
            for t in range(2):
                top = z_ref[t, hi_blk * fb:(hi_blk + 1) * fb, :]
                nxt = jnp.zeros_like(top) if blk == 0 else z_ref[t, (hi_blk + 1) * fb:(hi_blk + 2) * fb, :]
                partner = jnp.dot(jsh, jnp.concatenate([top, nxt], axis=0), preferred_element_type=F32)
                own = z_ref[t, blk * fb:(blk + 1) * fb, :].astype(F32)
                folded = own + partner if t == 0 else own - partner
                eo_ref[t * half + blk * fb:t * half + (blk + 1) * fb, :] = folded.astype(BF)


def _fourier_rows(x_ref, cs_ref, bc_ref, bs_ref, jsh_ref, o_ref, z_ref, eo_ref):
    n = x_ref.shape[1]
    half = n // 2
    tr = o_ref.shape[1]
    y = jnp.dot(cs_ref[...], eo_ref[...], preferred_element_type=F32)
    parity = lax.broadcasted_iota(I32, (tr, 1), 0) & 1
    sign = (1.0 - 2.0 * parity.astype(F32)) * (1.0 / math.sqrt(n))
    o_ref[0] = (y + sign * z_ref[0, half:half + 1, :].astype(F32)).astype(BF)


def _fourier_mix(fo, cs, bdc, bds, jsh, tr):
    b, n, w = fo.shape
    assert n % (2 * jsh.shape[0]) == 0 and tr % 2 == 0
    return pl.pallas_call(
        _fourier_kernel,
        grid=(b, n // tr),
        in_specs=[pl.BlockSpec((1, n, w), lambda i, j: (i, 0, 0)),
                  pl.BlockSpec((tr, n), lambda i, j: (j, 0)),
                  pl.BlockSpec(bdc.shape, lambda i, j: (0, 0)),
                  pl.BlockSpec(bds.shape, lambda i, j: (0, 0)),
                  pl.BlockSpec(jsh.shape, lambda i, j: (0, 0))],
        out_specs=pl.BlockSpec((1, tr, w), lambda i, j: (i, j, 0)),
        out_shape=jax.ShapeDtypeStruct((b, n, w), BF),
        scratch_shapes=[pltpu.VMEM((2, n, w), BF), pltpu.VMEM((n, w), BF)],
        compiler_params=_cparams(("arbitrary", "arbitrary")),
        name="fourier_mix",
    )(fo, cs, bdc, bds, jsh)


def _ret_kernel(*refs, chunk, n_chunks, sub):
    _ret_body(pl.program_id(0), pl.program_id(1), pl.program_id(2), *refs, chunk=chunk, n_chunks=n_chunks, sub=sub)


def _ret_body(b, p, c, lg_ref, q_ref, k_ref, v_ref, sg_ref, sf_ref, sb_ref, o_ref, sfo_ref, sbo_ref,
              acc_ref, tmp_ref, st_ref, dm_ref, qd_ref, qm_ref, kd_ref, cd_ref, *, chunk, n_chunks, sub):
    sf = float(sub)
    n_sub = chunk // sub

    @pl.when((b == 0) & (p == 0) & (c == 0))
    def _tables():
        ii = lax.broadcasted_iota(I32, (sub, sub), 0).astype(F32)
        jj = lax.broadcasted_iota(I32, (sub, sub), 1).astype(F32)
        pos = lax.broadcasted_iota(I32, (sub, LANES), 0).astype(F32)
        lane = lax.broadcasted_iota(I32, (sub, LANES), 1)
        for dr in range(2):
            if dr == 0:
                rel, keep = ii - jj, ii >= jj
                qe, ke = pos + 1.0, (sf - 1.0) - pos
            else:
                rel, keep = jj - ii, jj > ii
                qe, ke = sf - pos, pos
            for hp in range(N_HEADS // 2):
                lg0 = jnp.full((sub, LANES), lg_ref[dr, 2 * hp], F32)
                lg1 = jnp.full((sub, LANES), lg_ref[dr, 2 * hp + 1], F32)
                lgp = jnp.where(lane < HEAD_DIM, lg0, lg1)
                kd_ref[dr, hp] = jnp.exp(lgp * ke)
            for h in range(N_HEADS):
                lgs = lg_ref[dr, h]
                in_head = (lane >= (h % 2) * HEAD_DIM) & (lane < (h % 2 + 1) * HEAD_DIM)
                dm_ref[dr, h] = jnp.where(keep, jnp.exp(jnp.full((sub, sub), lgs, F32) * jnp.maximum(rel, 0.0)), 0.0)
                qd_ref[dr, h] = jnp.where(in_head, jnp.exp(jnp.full((sub, LANES), lgs, F32) * qe), 0.0)
                cd_ref[dr, h] = jnp.exp(jnp.full((LANES, LANES), lgs, F32) * sf)
        for h in range(N_HEADS):
            in_head = (lane >= (h % 2) * HEAD_DIM) & (lane < (h % 2 + 1) * HEAD_DIM)
            qm_ref[h] = jnp.where(in_head, 1.0, 0.0)

    nbb = q_ref.shape[0]

    def load_state(src_ref):
        z = jnp.zeros((HEAD_DIM, HEAD_V), F32)
        for bb in range(nbb):
            for h in range(N_HEADS):
                s = src_ref[bb, h]
                st_ref[bb, h] = jnp.concatenate([s, z], axis=0) if h % 2 == 0 else jnp.concatenate([z, s], axis=0)

    @pl.when((c == 0) & (p == 0))
    def _():
        load_state(sf_ref)

    @pl.when((c == 0) & (p == 1))
    def _():
        load_state(sb_ref)

    for bb in range(nbb):
        sts = [st_ref[bb, h] for h in range(N_HEADS)]
        for u in range(n_sub):
            r0 = pl.multiple_of(jnp.where(p == 0, u, n_sub - 1 - u) * sub, sub)
            rows = pl.ds(r0, sub)
            q = q_ref[bb, rows, :]
            k = k_ref[bb, rows, :]
            v = v_ref[bb, rows, :]
            outs = []
            for hp in range(N_HEADS // 2):
                qp = q[:, hp * LANES:(hp + 1) * LANES].astype(F32)
                kp = k[:, hp * LANES:(hp + 1) * LANES]
                kdp = (kp.astype(F32) * kd_ref[p, hp]).astype(BF)
                for h in (2 * hp, 2 * hp + 1):
                    vh = v[:, h * HEAD_V:(h + 1) * HEAD_V]
                    qm = (qp * qm_ref[h]).astype(BF)
                    qdq = (qp * qd_ref[p, h]).astype(BF)
                    s = _nt_dot(qm, kp) * dm_ref[p, h]
                    o = (jnp.dot(s.astype(BF), vh, preferred_element_type=F32)
                         + jnp.dot(qdq, sts[h].astype(BF), preferred_element_type=F32))
                    sts[h] = sts[h] * cd_ref[p, h] + _tn_dot(kdp, vh)
                    outs.append(o)
            tmp_ref[bb, rows, :] = jnp.concatenate(outs, axis=1)
        for h in range(N_HEADS):
            st_ref[bb, h] = sts[h]
    idx = jnp.where(p == 0, c, n_chunks - 1 - c)

    @pl.when(p == 0)
    def _():
        for bb in range(nbb):
            acc_ref[bb, idx] = tmp_ref[bb]

    @pl.when(p == 1)
    def _():
        for bb in range(nbb):
            tot = acc_ref[bb, idx] + tmp_ref[bb]
            sg = sg_ref[bb].astype(F32)
            ys = []
            for h in range(N_HEADS):
                oh = tot[:, h * HEAD_V:(h + 1) * HEAD_V]
                mu = jnp.mean(oh, axis=-1, keepdims=True)
                dlt = oh - mu
                var = jnp.mean(dlt * dlt, axis=-1, keepdims=True)
                ys.append(dlt * lax.rsqrt(var + EPS))
            o_ref[bb] = (jnp.concatenate(ys, axis=1) * sg).astype(BF)

    def store_state(dst_ref):
        for bb in range(nbb):
            for h in range(N_HEADS):
                s0 = (h % 2) * HEAD_DIM
                dst_ref[bb, h] = st_ref[bb, h, s0:s0 + HEAD_DIM, :]

    @pl.when((c == n_chunks - 1) & (p == 0))
    def _():
        store_state(sfo_ref)

    @pl.when((c == n_chunks - 1) & (p == 1))
    def _():
        store_state(sbo_ref)


def _retention(rq, rk, rv, sg, decays, s_f, s_b, chunk):
    b, n, _ = rq.shape
    n_chunks = n // chunk
    wv = rv.shape[2]

    def seq_map(i, p, c):
        return (i, jnp.where(p == 0, c, n_chunks - 1 - c), 0)

    def out_map(i, p, c):
        return (i, jnp.where(p == 0, n_chunks - 1, n_chunks - 1 - c), 0)

    nbb = 2 if b % 2 == 0 else 1
    st_spec = pl.BlockSpec((nbb, N_HEADS, HEAD_DIM, HEAD_V), lambda i, p, c: (i, 0, 0, 0))
    st_shape = jax.ShapeDtypeStruct((b, N_HEADS, HEAD_DIM, HEAD_V), F32)
    sub = min(RET_SUB, chunk)
    kern = functools.partial(_ret_kernel, chunk=chunk, n_chunks=n_chunks, sub=sub)
    return pl.pallas_call(
        kern,
        grid=(b // nbb, 2, n_chunks),
        in_specs=[pl.BlockSpec(memory_space=pltpu.SMEM),
                  pl.BlockSpec((nbb, chunk, rq.shape[2]), seq_map),
                  pl.BlockSpec((nbb, chunk, rk.shape[2]), seq_map),
                  pl.BlockSpec((nbb, chunk, wv), seq_map),
                  pl.BlockSpec((nbb, chunk, wv), seq_map),
                  st_spec, st_spec],
        out_specs=[pl.BlockSpec((nbb, chunk, wv), out_map), st_spec, st_spec],
        out_shape=[jax.ShapeDtypeStruct((b, n, wv), BF), st_shape, st_shape],
        scratch_shapes=[pltpu.VMEM((nbb, n_chunks, chunk, wv), F32),
                        pltpu.VMEM((nbb, chunk, wv), F32),
                        pltpu.VMEM((nbb, N_HEADS, LANES, HEAD_V), F32),
                        pltpu.VMEM((2, N_HEADS, sub, sub), F32),
                        pltpu.VMEM((2, N_HEADS, sub, LANES), F32),
                        pltpu.VMEM((N_HEADS, sub, LANES), F32),
                        pltpu.VMEM((2, N_HEADS // 2, sub, LANES), F32),
                        pltpu.VMEM((2, N_HEADS, LANES, HEAD_V), F32)],
        compiler_params=_cparams(("arbitrary", "arbitrary", "arbitrary")),
        name="retention",
    )(decays, rq, rk, rv, sg, s_f, s_b)


def _dattn_kernel(dl_ref, q_ref, *refs, n_src, lam_init, sub_q):
    k_refs = refs[:n_src]
    v_refs = refs[n_src:2 * n_src]
    o_ref = refs[2 * n_src]
    dl = dl_ref[...]
    lam = (jnp.exp(jnp.sum(dl[0:1] * dl[1:2], axis=-1, keepdims=True))
           - jnp.exp(jnp.sum(dl[2:3] * dl[3:4], axis=-1, keepdims=True)) + lam_init)
    tq = min(sub_q, q_ref.shape[1])
    lane = lax.broadcasted_iota(I32, (tq, LANES), 1)
    for r0 in range(0, q_ref.shape[1], tq):
        q = q_ref[0, r0:r0 + tq, :].astype(F32)
        ys = []
        for h in range(N_HEADS):
            qh = q[:, h * LANES:(h + 1) * LANES]
            qs = jnp.concatenate([jnp.where(lane < HEAD_DIM, qh, 0.0), jnp.where(lane >= HEAD_DIM, qh, 0.0)],
                                 axis=0).astype(BF)
            ss = [_nt_dot(qs, kr[0, :, h * LANES:(h + 1) * LANES]) for kr in k_refs]
            m = ss[0].max(axis=-1, keepdims=True)
            for s in ss[1:]:
                m = jnp.maximum(m, s.max(axis=-1, keepdims=True))
            acc = jnp.zeros((2 * tq, 2 * HEAD_V), F32)
            for s, vr in zip(ss, v_refs):
                pexp = jnp.exp2(s - m).astype(BF)
                acc = acc + jnp.dot(pexp, vr[0, :, 2 * h * HEAD_V:(2 * h + 2) * HEAD_V],
                                    preferred_element_type=F32)
            o = acc[:, :HEAD_V] / acc[:, HEAD_V:]
            oh = o[:tq] - lam * o[tq:]
            ms = jnp.mean(oh * oh, axis=-1, keepdims=True)
            ys.append(oh * lax.rsqrt(ms + EPS) * (1.0 - lam_init))
        o_ref[0, r0:r0 + tq, :] = jnp.concatenate(ys, axis=1).astype(BF)


def _diff_attention(dq, ks, vs, dlam, lam_init, tq, sub_q=256):
    b, n, w = dq.shape
    n_src = len(ks)
    kern = functools.partial(_dattn_kernel, n_src=n_src, lam_init=lam_init, sub_q=sub_q)
    kv_specs = [pl.BlockSpec((1, a.shape[1], a.shape[2]), lambda i, j: (i, 0, 0)) for a in (*ks, *vs)]
    return pl.pallas_call(
        kern,
        grid=(b, n // tq),
        in_specs=[pl.BlockSpec(dlam.shape, lambda i, j: (0, 0)),
                  pl.BlockSpec((1, tq, w), lambda i, j: (i, j, 0))] + kv_specs,
        out_specs=pl.BlockSpec((1, tq, w), lambda i, j: (i, j, 0)),
        out_shape=jax.ShapeDtypeStruct((b, n, w), BF),
        compiler_params=_cparams(("arbitrary", "arbitrary")),
        name="diff_attention",
    )(dlam, dq, *ks, *vs)


def _merge_kernel(f_ref, ro_ref, do_ref, gt_ref, x_ref, m2_ref, m3_ref, m4_ref, g_ref,
                  wf_ref, wr_ref, wd_ref, wo_ref, rcat_ref, x1_ref, h2_ref, aff_ref):
    tm, d = x_ref.shape
    st = min(MERGE_SUB, tm)
    for r0 in range(0, tm, st):
        rows = slice(r0, r0 + st)
        t = (gt_ref[rows, 0:d].astype(F32) * jnp.dot(f_ref[rows, :], wf_ref[...], preferred_element_type=F32)
             + gt_ref[rows, d:2 * d].astype(F32) * jnp.dot(ro_ref[rows, :], wr_ref[...], preferred_element_type=F32)
             + gt_ref[rows, 2 * d:3 * d].astype(F32) * jnp.dot(do_ref[rows, :], wd_ref[...],
                                                                preferred_element_type=F32))
        mix = jnp.dot(t.astype(BF), wo_ref[...], preferred_element_type=F32)
        x1 = x_ref[rows, :] + m2_ref[0] * mix
        x1_ref[rows, :] = x1
        ms = jnp.mean(x1 * x1, axis=-1, keepdims=True)
        h2 = x1 * lax.rsqrt(ms + EPS) * g_ref[...]
        h2 = h2 * (1.0 + m4_ref[0]) + m3_ref[0]
        hi = h2.astype(BF)
        lo = (h2 - hi.astype(F32)).astype(BF)
        h2_ref[rows, :] = hi
        l1 = jnp.dot(hi, rcat_ref[...], preferred_element_type=F32)
        lt = l1[:, :LANES] + l1[:, LANES:] + jnp.dot(lo, rcat_ref[:, :LANES], preferred_element_type=F32)
        lane = lax.broadcasted_iota(I32, lt.shape, 1)
        lt = jnp.where(lane < N_EXPERTS, lt, -jnp.inf)
        ex = jnp.exp(lt - lt.max(axis=-1, keepdims=True))
        aff = ex / jnp.sum(ex, axis=-1, keepdims=True)
        aff_ref[0, :, rows] = aff.T[:N_EXPERTS, :]


def _merge(f, ro, do, gt, x2d, m2, m3, m4, g, wf, wr, wd, wo, rcat, n_seq, tm):
    rows, d = x2d.shape
    tiles_per_b = n_seq // tm
    nb = m2.shape[0]
    bsz = rows // n_seq
    ne = N_EXPERTS

    def mod_map(i):
        return ((i // tiles_per_b) if nb > 1 else 0, 0, 0)

    const2 = lambda i: (0, 0)
    row_spec = lambda w: pl.BlockSpec((tm, w), lambda i: (i, 0))
    return pl.pallas_call(
        _merge_kernel,
        grid=(rows // tm,),
        in_specs=[row_spec(f.shape[1]), row_spec(ro.shape[1]), row_spec(do.shape[1]), row_spec(gt.shape[1]),
                  row_spec(d),
                  pl.BlockSpec((1, 1, d), mod_map), pl.BlockSpec((1, 1, d), mod_map), pl.BlockSpec((1, 1, d), mod_map),
                  pl.BlockSpec((1, d), const2),
                  pl.BlockSpec(wf.shape, const2), pl.BlockSpec(wr.shape, const2), pl.BlockSpec(wd.shape, const2),
                  pl.BlockSpec(wo.shape, const2), pl.BlockSpec(rcat.shape, const2)],
        out_specs=[row_spec(d), row_spec(d),
                   pl.BlockSpec((1, ne, tm), lambda i: (i // tiles_per_b, 0, i % tiles_per_b))],
        out_shape=[jax.ShapeDtypeStruct((rows, d), F32), jax.ShapeDtypeStruct((rows, d), BF),
                   jax.ShapeDtypeStruct((bsz, ne, n_seq), F32)],
        compiler_params=_cparams(("arbitrary",)),
        name="merge_router",
    )(f, ro, do, gt, x2d, m2, m3, m4, g, wf, wr, wd, wo, rcat)


def _route_kernel(a_ref, slot_ref, offs_ref, *, cap, blk):
    a = a_ref[...]
    ne, n = a.shape
    capf = float(cap)

    def enough(t):
        return jnp.sum(jnp.where(a >= t, 1.0, 0.0), axis=-1, keepdims=True) >= capf

    tiny = jnp.full((ne, 1), F32_MIN_NORMAL, F32)
    found = enough(tiny)
    cur = tiny
    for step in (64, 32, 16, 8, 4, 2, 1):
        cand = cur * (2.0 ** step)
        cur = jnp.where(enough(cand), cand, cur)
    base = cur

    def mantissa_bit(_, carry):
        cur, stepv = carry
        cand = cur + stepv
        return jnp.where(enough(cand), cand, cur), stepv * 0.5

    cur, ulp = lax.fori_loop(0, 23, mantissa_bit, (cur, base * 0.5))
    lo = jnp.where(found, cur, 0.0)
    hi = jnp.where(found, cur + ulp * 2.0, tiny)

    def refine(_, carry):
        lo, hi = carry
        mid = lo + (hi - lo) * 0.5
        ok = enough(mid)
        return jnp.where(ok, mid, lo), jnp.where(ok, hi, mid)

    lo, hi = lax.fori_loop(0, 24, refine, (lo, hi))
    ri = lax.broadcasted_iota(I32, (blk, blk), 0)
    ci = lax.broadcasted_iota(I32, (blk, blk), 1)
    upper = jnp.where(ri < ci, 1.0, 0.0).astype(BF)

    def excl_cumsum(m):
        carry = jnp.zeros((ne, 1), F32)
        outs = []
        for j in range(n // blk):
            mb = m[:, j * blk:(j + 1) * blk]
            outs.append(jnp.dot(mb.astype(BF), upper, preferred_element_type=F32) + carry)
            carry = carry + jnp.sum(mb, axis=-1, keepdims=True)
        return jnp.concatenate(outs, axis=1)

    gt = a >= hi
    tie = (a >= lo) & (a < hi)
    need = capf - jnp.sum(jnp.where(gt, 1.0, 0.0), axis=-1, keepdims=True)
    sel = gt | (tie & (excl_cumsum(jnp.where(tie, 1.0, 0.0)) < need))
    selm = jnp.where(sel, 1.0, 0.0)
    slot = excl_cumsum(selm)
    slot_ref[...] = jnp.where(sel, slot.astype(I32), -1)
    tok = lax.broadcasted_iota(I32, (n, LANES), 0)
    blk_start = lax.broadcasted_iota(I32, (n, LANES), 1) * blk
    before = jnp.where((tok < blk_start) & (blk_start <= n), 1.0, 0.0).astype(BF)
    offs_ref[...] = jnp.dot(selm.astype(BF), before, preferred_element_type=F32).astype(I32)


def _route(aff_t, cap):
    b, ne, n = aff_t.shape
    kern = functools.partial(_route_kernel, cap=cap, blk=min(TOKEN_BLOCK, n))
    slot, offs = pl.pallas_call(
        kern,
        grid=(1,),
        in_specs=[pl.BlockSpec((b * ne, n), lambda i: (0, 0))],
        out_specs=[pl.BlockSpec((b * ne, n), lambda i: (0, 0)), pl.BlockSpec((b * ne, LANES), lambda i: (0, 0))],
        out_shape=[jax.ShapeDtypeStruct((b * ne, n), I32), jax.ShapeDtypeStruct((b * ne, LANES), I32)],
        compiler_params=_cparams(("arbitrary",)),
        name="route",
    )(aff_t.reshape(b * ne, n))
    return slot.reshape(b, ne, n), offs


def _window_start(off, cap, win):
    aligned = lax.shift_left(lax.shift_right_logical(off, BF16_ROWS_LOG2), BF16_ROWS_LOG2)
    return jnp.minimum(aligned, cap - win)


def _windows_fit(offs_ref, row0, ne, blk0, nblk, cap, win):
    bad = jnp.int32(0)
    for e in range(ne):
        for jj in range(nblk):
            j = blk0 + jj
            off = offs_ref[row0 + e, j]
            end = offs_ref[row0 + e, j + 1]
            bad = bad | (end - _window_start(off, cap, win) > win).astype(I32)
    return bad == 0


def _gather_kernel(offs_ref, slot_ref, h_ref, xs_ref, *, cap, win, tb):
    ne, n = slot_ref.shape[1], slot_ref.shape[3]
    nblk = n // tb
    row0 = pl.program_id(0) * ne
    fits = _windows_fit(offs_ref, row0, ne, 0, nblk, cap, win)

    @pl.when(fits)
    def _windowed():
        xs_ref[...] = jnp.zeros(xs_ref.shape, BF)
        rows = lax.broadcasted_iota(I32, (win, tb), 0)
        for j in range(nblk):
            starts, lhs = [], []
            for e in range(ne):
                w0 = pl.multiple_of(_window_start(offs_ref[row0 + e, j], cap, win), 1 << BF16_ROWS_LOG2)
                starts.append(w0)
                lhs.append(jnp.where(slot_ref[0, e, :, j * tb:(j + 1) * tb] == rows + w0, 1.0, 0.0).astype(BF))
            got = jnp.dot(jnp.concatenate(lhs, axis=0), h_ref[0, j * tb:(j + 1) * tb, :],
                          preferred_element_type=F32)
            for e in range(ne):
                dst = xs_ref.at[0, e, pl.ds(starts[e], win), :]
                dst[...] = (dst[...].astype(F32) + got[e * win:(e + 1) * win]).astype(BF)

    @pl.when(jnp.logical_not(fits))
    def _dense():
        sub = lax.broadcasted_iota(I32, (cap, n), 0)
        hb = h_ref[0]
        for e in range(ne):
            onehot = jnp.where(sub == slot_ref[0, e], 1.0, 0.0).astype(BF)
            xs_ref[0, e] = jnp.dot(onehot, hb, preferred_element_type=F32).astype(BF)


def _gather(slot, offs, h2, cap):
    b, ne, n = slot.shape
    d = h2.shape[2]
    tb = min(TOKEN_BLOCK, n)
    kern = functools.partial(_gather_kernel, cap=cap, win=min(SLOT_WINDOW, cap), tb=tb)
    grid_spec = pltpu.PrefetchScalarGridSpec(
        num_scalar_prefetch=1,
        grid=(b,),
        in_specs=[pl.BlockSpec((1, ne, 1, n), lambda i, offs: (i, 0, 0, 0)),
                  pl.BlockSpec((1, n, d), lambda i, offs: (i, 0, 0))],
        out_specs=pl.BlockSpec((1, ne, cap, d), lambda i, offs: (i, 0, 0, 0)))
    return pl.pallas_call(
        kern,
        grid_spec=grid_spec,
        out_shape=jax.ShapeDtypeStruct((b, ne, cap, d), BF),
        compiler_params=_cparams(("arbitrary",)),
        name="moe_gather",
    )(offs, slot.reshape(b, ne, 1, n), h2)


def _ffn_kernel(*refs, n_src, n_fc):
    xs_refs = refs[:n_src]
    wg_ref, wu_ref, wd_ref = refs[n_src:n_src + 3]
    ye_refs = refs[n_src + 3:2 * n_src + 3]
    x_ref, hid_ref = refs[2 * n_src + 3:]
    j = pl.program_id(1)
    fc = wg_ref.shape[3]
    row_spans = []
    r0 = 0
    for xr in xs_refs:
        rows = xr.shape[0] * xr.shape[2]
        row_spans.append((r0, rows))
        r0 += rows

    @pl.when(j == 0)
    def _():
        for xr, (s0, rows) in zip(xs_refs, row_spans):
            x_ref[s0:s0 + rows, :] = xr[:, 0].reshape(rows, xr.shape[3])

    @pl.when(j < n_fc)
    def _():
        x = x_ref[...]
        g = jnp.dot(x, wg_ref[0, 0].astype(BF), preferred_element_type=F32)
        u = jnp.dot(x, wu_ref[0, 0].astype(BF), preferred_element_type=F32)
        hid_ref[j] = (g * _sigmoid(g) * u).astype(BF)

    @pl.when(j >= n_fc)
    def _():
        y = jnp.dot(hid_ref[0], wd_ref[0, 0, 0:fc, :].astype(BF), preferred_element_type=F32)
        for k in range(1, n_fc):
            y = y + jnp.dot(hid_ref[k], wd_ref[0, 0, k * fc:(k + 1) * fc, :].astype(BF), preferred_element_type=F32)
        for yr, (s0, rows) in zip(ye_refs, row_spans):
            yr[:, 0] = y[s0:s0 + rows].reshape(yr.shape[0], yr.shape[2], yr.shape[3]).astype(BF)


def _expert_ffn(xs_list, w_gate, w_up, w_down, layer, fchunk, ochunk):
    ne, d = xs_list[0].shape[1], xs_list[0].shape[3]
    f = w_gate.shape[3]
    n_fc, n_oc = f // fchunk, d // ochunk
    n_src = len(xs_list)
    total_rows = sum(x.shape[0] * x.shape[2] for x in xs_list)
    kern = functools.partial(_ffn_kernel, n_src=n_src, n_fc=n_fc)
    up_map = lambda e, j: (layer, e, 0, jnp.minimum(j, n_fc - 1))
    down_map = lambda e, j: (layer, e, 0, jnp.maximum(j - n_fc, 0))
    out_map = lambda e, j: (0, e, 0, jnp.maximum(j - n_fc, 0))
    outs = pl.pallas_call(
        kern,
        grid=(ne, n_fc + n_oc),
        in_specs=[pl.BlockSpec((x.shape[0], 1, x.shape[2], d), lambda e, j: (0, e, 0, 0), pipeline_mode=pl.Buffered(1))
                  for x in xs_list]
        + [pl.BlockSpec((1, 1, d, fchunk), up_map),
           pl.BlockSpec((1, 1, d, fchunk), up_map),
           pl.BlockSpec((1, 1, f, ochunk), down_map)],
        out_specs=[pl.BlockSpec((x.shape[0], 1, x.shape[2], ochunk), out_map) for x in xs_list],
        out_shape=[jax.ShapeDtypeStruct(x.shape, BF) for x in xs_list],
        scratch_shapes=[pltpu.VMEM((total_rows, d), BF), pltpu.VMEM((n_fc, total_rows, fchunk), BF)],
        compiler_params=_cparams(("arbitrary", "arbitrary")),
        name="expert_ffn",
    )(*xs_list, w_gate, w_up, w_down)
    return list(outs)


def _combine_kernel(offs_ref, x1_ref, m5_ref, slot_ref, aff_ref, ye_ref, o_ref, ywin_ref, *, cap, win, tb):
    ne, tn = slot_ref.shape[1], slot_ref.shape[3]
    row0 = pl.program_id(0) * ne
    blk0 = pl.program_id(1) * (tn // tb)
    fits = _windows_fit(offs_ref, row0, ne, blk0, tn // tb, cap, win)

    @pl.when(fits)
    def _windowed():
        rows = lax.broadcasted_iota(I32, (win, tb), 0)
        for jj in range(tn // tb):
            tok = slice(jj * tb, (jj + 1) * tb)
            pieces = []
            for e in range(ne):
                w0 = pl.multiple_of(_window_start(offs_ref[row0 + e, blk0 + jj], cap, win), 1 << BF16_ROWS_LOG2)
                ywin_ref[e * win:(e + 1) * win, :] = ye_ref[0, e, pl.ds(w0, win), :]
                pieces.append(jnp.where(slot_ref[0, e, :, tok] == rows + w0, aff_ref[0, e, :, tok], 0.0).astype(BF))
            acc = _tn_dot(jnp.concatenate(pieces, axis=0), ywin_ref[...])
            o_ref[0, tok, :] = x1_ref[0, tok, :] + m5_ref[0] * acc

    @pl.when(jnp.logical_not(fits))
    def _dense():
        rows = lax.broadcasted_iota(I32, (cap, tn), 0)
        acc = jnp.zeros(x1_ref.shape[1:], F32)
        for e in range(ne):
            weighted = jnp.where(slot_ref[0, e] == rows, aff_ref[0, e], 0.0).astype(BF)
            acc = acc + _tn_dot(weighted, ye_ref[0, e])
        o_ref[0] = x1_ref[0] + m5_ref[0] * acc


def _combine(x1, m5, slot, aff_t, offs, ye, tn):
    b, n, d = x1.shape
    ne, cap = ye.shape[1], ye.shape[2]
    nb = m5.shape[0]
    tb = min(TOKEN_BLOCK, n)
    win = min(SLOT_WINDOW, cap)
    kern = functools.partial(_combine_kernel, cap=cap, win=win, tb=tb)
    grid_spec = pltpu.PrefetchScalarGridSpec(
        num_scalar_prefetch=1,
        grid=(b, n // tn),
        in_specs=[pl.BlockSpec((1, tn, d), lambda i, j, offs: (i, j, 0)),
                  pl.BlockSpec((1, 1, d), lambda i, j, offs: (i if nb > 1 else 0, 0, 0)),
                  pl.BlockSpec((1, ne, 1, tn), lambda i, j, offs: (i, 0, 0, j)),
                  pl.BlockSpec((1, ne, 1, tn), lambda i, j, offs: (i, 0, 0, j)),
                  pl.BlockSpec((1, ne, cap, d), lambda i, j, offs: (i, 0, 0, 0))],
        out_specs=pl.BlockSpec((1, tn, d), lambda i, j, offs: (i, j, 0)),
        scratch_shapes=[pltpu.VMEM((ne * win, d), BF)])
    return pl.pallas_call(
        kern,
        grid_spec=grid_spec,
        out_shape=jax.ShapeDtypeStruct((b, n, d), F32),
        compiler_params=_cparams(("arbitrary", "arbitrary")),
        name="moe_combine",
    )(offs, x1, m5, slot.reshape(b, ne, 1, n), aff_t.reshape(b, ne, 1, n), ye)


def _rope_tables(n, use_rope):
    if not use_rope:
        return jnp.ones((n, LANES), F32), jnp.zeros((n, LANES), F32)
    rows = n // GRID_W
    row = jnp.repeat(jnp.arange(rows, dtype=F32), GRID_W)
    col = jnp.tile(jnp.arange(GRID_W, dtype=F32), rows)
    inv = ROPE_BASE ** (-jnp.arange(ROPE_FREQS_PER_AXIS, dtype=F32) / ROPE_FREQS_PER_AXIS)
    ang = jnp.concatenate([row[:, None] * inv, col[:, None] * inv], axis=-1)
    cos, sin = jnp.cos(ang), jnp.sin(ang)
    cos_t = jnp.tile(cos, (1, LANES // (HEAD_DIM // 2)))
    sin_t = jnp.tile(jnp.concatenate([-sin, sin], axis=-1), (1, LANES // HEAD_DIM))
    return cos_t, sin_t


def _dft_tables(n):
    n0 = 64
    n1 = n // n0
    k = np.arange(n, dtype=np.int64)[:, None]
    ang1 = jnp.asarray(2.0 * np.pi * ((k * np.arange(n1 // 2)[None, :]) % n1) / n1, F32)
    ang0 = jnp.asarray(2.0 * np.pi * ((k * np.arange(n0)[None, :]) % n) / n, F32)
    c1, s1, c0, s0 = jnp.cos(ang1), jnp.sin(ang1), jnp.cos(ang0), jnp.sin(ang0)
    scale = 1.0 / math.sqrt(n)
    cos_n = (c1[:, :, None] * c0[:, None, :] - s1[:, :, None] * s0[:, None, :]).reshape(n, n // 2) * scale
    sin_n = (s1[:, :, None] * c0[:, None, :] + c1[:, :, None] * s0[:, None, :]).reshape(n, n // 2) * scale
    return jnp.concatenate([cos_n, -sin_n], axis=1).astype(BF)


def _fold_matrix(fb):
    i = np.arange(fb)[:, None]
    s = np.arange(2 * fb)[None, :]
    return jnp.asarray(np.where(s == fb - i, 1.0, 0.0), BF)


def _group_dft_tables(width):
    g = FNET_GROUP_DIM
    idx = np.arange(width)
    same = (idx[:, None] // g) == (idx[None, :] // g)
    ang = 2.0 * np.pi * (((idx[:, None] % g) * (idx[None, :] % g)) % g) / g
    scale = 1.0 / math.sqrt(g)
    bdc = np.where(same, np.cos(ang), 0.0) * scale
    bds = np.where(same, np.sin(ang), 0.0) * scale
    return jnp.asarray(bdc, F32).astype(BF), jnp.asarray(bds, F32).astype(BF)


def _group_mean_matrix(width):
    idx = np.arange(width)
    same = (idx[:, None] // HEAD_DIM) == (idx[None, :] // HEAD_DIM)
    return jnp.asarray(np.where(same, 1.0 / HEAD_DIM, 0.0), BF)


def _mixer_inputs(x3, mods, g_attn_l, w_in_bf, tables, qg, kg, gm, kv_only=False):
    b, n, d = x3.shape
    tm = min(512, n)
    outs = _in_projection(x3.reshape(b * n, d), mods[0], mods[1], g_attn_l, w_in_bf, tables[0], tables[1],
                          qg, kg, gm, n, tm, kv_only)
    return [o.reshape(b, n, o.shape[1]) for o in outs]


def _moe(sets, w_gate, w_up, w_down, layer):
    slots, offss, xss = [], [], []
    for x1, h2, aff_t, _ in sets:
        cap = EC_FACTOR * x1.shape[1] // N_EXPERTS
        slot, offs = _route(aff_t, cap)
        offs = offs[:, :OFFS_LANES]
        slots.append(slot)
        offss.append(offs)
        xss.append(_gather(slot, offs, h2, cap))
    yes = _expert_ffn(xss, w_gate, w_up, w_down, layer, 512, 512)
    return [_combine(x1, m5, slot, aff_t, offs, ye, min(1024, x1.shape[1]))
            for (x1, _, aff_t, m5), slot, offs, ye in zip(sets, slots, offss, yes)]


def kernel(x, c, ctx, c_ctx, w_mod, b_mod, g_attn, g_ffn, w_in, ret_decay, diff_qn, diff_kn, diff_lambda,
           w_fnet_o, w_ret_o, w_diff_o, w_out, w_router, w_exp_gate, w_exp_up, w_exp_down):
    bsz, n, d = x.shape
    n_ctx = ctx.shape[1]
    depth = w_mod.shape[0]

    pad = (-(bsz + 1)) % 8
    cvecs = jnp.concatenate([c, c_ctx[None, :], jnp.zeros((pad, d), F32)], axis=0)
    mods = _modulation(cvecs, w_mod, b_mod)

    rope_lat = _rope_tables(n, True)
    rope_ctx = _rope_tables(n_ctx, False)
    cs_lat, cs_ctx = _dft_tables(n), _dft_tables(n_ctx)
    bdc, bds = _group_dft_tables(2 * LANES)
    jsh = _fold_matrix(LANES)
    gm = _group_mean_matrix(2 * LANES)
    zero_state = jnp.zeros((bsz, N_HEADS, HEAD_DIM, HEAD_V), F32)
    ret_chunk = 1024

    xc = ctx
    for layer in range(depth):
        last = layer == depth - 1
        lam_init = 0.8 - 0.6 * math.exp(-0.3 * layer)
        mx = [mods[layer, :bsz, j * d:(j + 1) * d].reshape(bsz, 1, d) for j in range(N_MOD)]
        mc = [mods[layer, bsz:bsz + 1, j * d:(j + 1) * d].reshape(1, 1, d) for j in range(N_MOD)]
        w_in_bf = w_in[layer].astype(BF)
        qg = jnp.tile(diff_qn[layer], _W_DQ // HEAD_DIM)[None, :]
        kg = jnp.tile(diff_kn[layer], _W_DK // HEAD_DIM)[None, :]
        g_a = g_attn[layer][None, :]
        g_f = g_ffn[layer][None, :]
        decays = ret_decay[layer]
        dlam = diff_lambda[layer]
        branch_w = (w_fnet_o[layer].astype(BF), w_ret_o[layer].astype(BF), w_diff_o[layer].astype(BF),
                    w_out[layer].astype(BF))
        wr32 = w_router[layer]
        wr_hi = wr32.astype(BF)
        wr_lo = (wr32 - wr_hi.astype(F32)).astype(BF)
        lane_pad = jnp.zeros((d, LANES - N_EXPERTS), BF)
        rcat = jnp.concatenate([wr_hi, lane_pad, wr_lo, lane_pad], axis=1)
        moe_sets = []

        if last:
            rk_c, rv_c, dk_c, dv_c = _mixer_inputs(xc, mc, g_a, w_in_bf, rope_ctx, qg, kg, gm, kv_only=True)
            rq_c, sg_c = rk_c, rv_c
        else:
            fo_c, rq_c, rk_c, rv_c, sg_c, dq_c, dk_c, dv_c, gt_c = _mixer_inputs(xc, mc, g_a, w_in_bf, rope_ctx,
                                                                               qg, kg, gm)
        ro_c, s_f, s_b = _retention(rq_c, rk_c, rv_c, sg_c, decays, zero_state, zero_state, min(ret_chunk, n_ctx))
        if not last:
            f_c = _fourier_mix(fo_c, cs_ctx, bdc, bds, jsh, min(512, n_ctx))
            do_c = _diff_attention(dq_c, [dk_c], [dv_c], dlam, lam_init, min(256, n_ctx))
            rows_c = bsz * n_ctx
            x1_c, h2_c, aff_c = _merge(f_c.reshape(rows_c, -1), ro_c.reshape(rows_c, -1), do_c.reshape(rows_c, -1),
                                       gt_c.reshape(rows_c, -1), xc.reshape(rows_c, d), mc[2], mc[3], mc[4], g_f,
                                       *branch_w, rcat, n_ctx, min(512, n_ctx))
            moe_sets.append((x1_c.reshape(bsz, n_ctx, d), h2_c.reshape(bsz, n_ctx, d), aff_c, mc[5]))

        fo, rq, rk, rv, sg, dq, dk, dv, gt = _mixer_inputs(x, mx, g_a, w_in_bf, rope_lat, qg, kg, gm)
        ro, _, _ = _retention(rq, rk, rv, sg, decays, s_f, s_b, min(ret_chunk, n))
        f = _fourier_mix(fo, cs_lat, bdc, bds, jsh, min(1024, n))
        do = _diff_attention(dq, [dk, dk_c], [dv, dv_c], dlam, lam_init, min(1024, n))
        rows = bsz * n
        x1, h2, aff = _merge(f.reshape(rows, -1), ro.reshape(rows, -1), do.reshape(rows, -1), gt.reshape(rows, -1),
                             x.reshape(rows, d), mx[2], mx[3], mx[4], g_f, *branch_w, rcat, n, min(1024, n))
        moe_sets.append((x1.reshape(bsz, n, d), h2.reshape(bsz, n, d), aff, mx[5]))
        moe_out = _moe(moe_sets, w_exp_gate, w_exp_up, w_exp_down, layer)
        x = moe_out[-1]
        if not last:
            xc = moe_out[0]
    return x
```
